```python
import jax, jax.numpy as jnp
from jax import lax
import numpy as np

D_MODEL = 1024
BATCH = 1
SEQ = 16384
DEPTH = 4
DEC_BATCH = 16
DEC_SEQ = 4096
PAST_LEN = 128

A_HEADS = 4
A_DK = 128
A_DV = D_MODEL // A_HEADS
A_QK = A_HEADS * A_DK
D_A = A_HEADS * A_DV
A_CONV = 3
A_CHUNK = 128
B_HEADS = 4
B_DK = D_MODEL // (2 * B_HEADS)
B_DV = D_MODEL // B_HEADS
B_QK = B_HEADS * B_DK
D_B = B_HEADS * B_DV
B_GATE_RANK = 16
B_GATE_TAU = 16.0
B_CHUNK = 64
C_HEADS = 8
C_Q_LORA = 384
C_KV_LORA = 256
C_NOPE = 128
C_ROPE = 64
C_V = 128
C_QBLOCK = 128
ROPE_BASE = 10000.0
X_HEADS = 4
X_DH = D_MODEL // X_HEADS
N_MEM = 256

N_EVEN = (DEPTH + 1) // 2
N_ODD = DEPTH // 2
EPS = 1e-6

AB_IN_SIZES = (2 * A_QK, D_A, 2 * A_HEADS, 2 * A_HEADS, D_A, D_A,
               B_QK, B_QK, D_B, 2 * B_GATE_RANK, D_B)
AB_IN = sum(AB_IN_SIZES)
C_IN_SIZES = (C_Q_LORA, C_KV_LORA, C_ROPE, C_HEADS * C_V)
C_IN = sum(C_IN_SIZES)

kernel_name = 'hybrid_mlstm_gla_mla_encoder'


def _split(t, sizes):
    idx = np.cumsum(np.array(sizes))[:-1].tolist()
    return jnp.split(t, idx, axis=-1)


def _rmsnorm(x, g):
    x32 = x.astype(jnp.float32)
    y = x32 * lax.rsqrt(jnp.mean(x32 * x32, axis=-1, keepdims=True) + EPS)
    return (y * g.astype(jnp.float32)).astype(x.dtype)


def _head_rmsnorm(x, g, n_heads):
    shp = x.shape
    x32 = x.astype(jnp.float32).reshape(shp[:-1] + (n_heads, shp[-1] // n_heads))
    y = x32 * lax.rsqrt(jnp.mean(x32 * x32, axis=-1, keepdims=True) + EPS)
    return (y.reshape(shp) * g.astype(jnp.float32)).astype(x.dtype)


def _flip(t):
    return jnp.flip(t, axis=1)


def _to_chunks(t, L):
    b, s, h = t.shape[:3]
    t = t.reshape((b, s // L, L, h) + t.shape[3:])
    perm = (1, 0, 3, 2) + tuple(range(4, t.ndim))
    return jnp.transpose(t, perm)


def _from_chunks(t):
    nc, b, h, L, d = t.shape
    return jnp.transpose(t, (1, 0, 3, 2, 4)).reshape(b, nc * L, h, d)


def _centred_dwconv(x, w, bias):
    c = x.shape[-1]
    pad = (A_CONV - 1) // 2
    y = lax.conv_general_dilated(x, w.reshape(A_CONV, 1, c), window_strides=(1,),
                                 padding=[(pad, pad)],
                                 dimension_numbers=('NWC', 'WIO', 'NWC'),
                                 feature_group_count=c)
    return y + bias


def _mlstm_chunkwise(q, k, v, log_i, log_f):
    f32 = jnp.float32
    b, s, h, dk = q.shape
    dv = v.shape[-1]
    L = A_CHUNK
    qc, kc, vc = _to_chunks(q.astype(f32), L), _to_chunks(k.astype(f32), L), _to_chunks(v.astype(f32), L)
    ic, fc = _to_chunks(log_i.astype(f32), L), _to_chunks(log_f.astype(f32), L)
    mask = jnp.tril(jnp.ones((L, L), dtype=bool))

    def step(carry, inp):
        C, n, m = carry
        qb, kb, vb, ib, fb = inp
        bl = jnp.cumsum(fb, axis=-1)
        g = bl[..., -1]
        log_d = jnp.where(mask, bl[..., :, None] - bl[..., None, :] + ib[..., None, :], -jnp.inf)
        log_inter = bl + m[..., None]
        m_t = jnp.maximum(log_inter, jnp.max(log_d, axis=-1))
        sc = jnp.einsum('bhtk,bhsk->bhts', qb, kb) * jnp.exp(log_d - m_t[..., None])
        w_inter = jnp.exp(log_inter - m_t)
        num = jnp.einsum('bhts,bhsv->bhtv', sc, vb) + w_inter[..., None] * jnp.einsum('bhtk,bhkv->bhtv', qb, C)
        den = jnp.sum(sc, axis=-1) + w_inter * jnp.einsum('bhtk,bhk->bht', qb, n)
        hb = num / jnp.maximum(jnp.abs(den), jnp.exp(-m_t))[..., None]
        log_w = g[..., None] - bl + ib
        m_new = jnp.maximum(g + m, jnp.max(log_w, axis=-1))
        w = jnp.exp(log_w - m_new[..., None])
        decay = jnp.exp(g + m - m_new)
        C_new = decay[..., None, None] * C + jnp.einsum('bhs,bhsk,bhsv->bhkv', w, kb, vb)
        n_new = decay[..., None] * n + jnp.einsum('bhs,bhsk->bhk', w, kb)
        return (C_new, n_new, m_new), hb

    init = (jnp.zeros((b, h, dk, dv), f32), jnp.zeros((b, h, dk), f32), jnp.zeros((b, h), f32))
    _, hs = lax.scan(step, init, (qc, kc, vc, ic, fc))
    return _from_chunks(hs)


def _gla_chunked(q, k, v, log_a):
    f32 = jnp.float32
    b, s, h, dk = q.shape
    dv = v.shape[-1]
    L = B_CHUNK
    qc, kc, vc = _to_chunks(q.astype(f32), L), _to_chunks(k.astype(f32), L), _to_chunks(v.astype(f32), L)
    ac = _to_chunks(log_a.astype(f32), L)
    mask = jnp.tril(jnp.ones((L, L), dtype=bool))

    def step(S, inp):
        qb, kb, vb, ab = inp
        bc = jnp.cumsum(ab, axis=2)
        g = bc[:, :, -1]
        rel = jnp.where(mask[:, :, None], bc[:, :, :, None, :] - bc[:, :, None, :, :], -jnp.inf)
        att = jnp.einsum('bhtk,bhsk,bhtsk->bhts', qb, kb, jnp.exp(rel))
        o = jnp.einsum('bhts,bhsv->bhtv', att, vb) + jnp.einsum('bhtk,bhkv->bhtv', qb * jnp.exp(bc), S)
        S_new = jnp.exp(g)[..., None] * S + jnp.einsum('bhsk,bhsv->bhkv', kb * jnp.exp(g[:, :, None, :] - bc), vb)
        return S_new, o

    _, os_ = lax.scan(step, jnp.zeros((b, h, dk, dv), f32), (qc, kc, vc, ac))
    return _from_chunks(os_)


def _ab_mixer(h, w_in, conv_w, conv_b, igate_b, fgate_b, ogate_b, a_norm, alpha_w2, alpha_b, b_norm, w_out):
    f32 = jnp.float32
    b, s, _ = h.shape
    a_qk, a_v, a_i, a_f, a_o, a_z, b_q, b_k, b_v, b_low, b_z = _split(h @ w_in, AB_IN_SIZES)
    a_qk = jax.nn.silu(_centred_dwconv(a_qk, conv_w, conv_b))
    q_a = a_qk[..., :A_QK].reshape(b, s, A_HEADS, A_DK) * (A_DK ** -0.5)
    k_a = a_qk[..., A_QK:].reshape(b, s, A_HEADS, A_DK)
    v_a = a_v.reshape(b, s, A_HEADS, A_DV)
    i_pre = a_i.reshape(b, s, 2, A_HEADS).astype(f32) + igate_b.astype(f32)
    log_f = jax.nn.log_sigmoid(a_f.reshape(b, s, 2, A_HEADS).astype(f32) + fgate_b.astype(f32))
    h_fwd = _mlstm_chunkwise(q_a, k_a, v_a, i_pre[:, :, 0], log_f[:, :, 0])
    h_bwd = _flip(_mlstm_chunkwise(_flip(q_a), _flip(k_a), _flip(v_a), _flip(i_pre[:, :, 1]), _flip(log_f[:, :, 1])))
    out_a = jax.nn.sigmoid(a_o + ogate_b) * (h_fwd + h_bwd).reshape(b, s, D_A).astype(h.dtype)
    out_a = _head_rmsnorm(out_a, a_norm, A_HEADS) * jax.nn.silu(a_z)
    q_b = b_q.reshape(b, s, B_HEADS, B_DK) * (B_DK ** -0.5)
    k_b = b_k.reshape(b, s, B_HEADS, B_DK)
    v_b = b_v.reshape(b, s, B_HEADS, B_DV)
    low = b_low.reshape(b, s, 2, B_GATE_RANK)
    log_a = jax.nn.log_sigmoid((jnp.einsum('bsdr,drk->bsdk', low, alpha_w2) + alpha_b).astype(f32)) / B_GATE_TAU
    log_a = log_a.reshape(b, s, 2, B_HEADS, B_DK)
    o_fwd = _gla_chunked(q_b, k_b, v_b, log_a[:, :, 0])
    o_bwd = _flip(_gla_chunked(_flip(q_b), _flip(k_b), _flip(v_b), _flip(log_a[:, :, 1])))
    out_b = _head_rmsnorm((o_fwd + o_bwd).reshape(b, s, D_B).astype(h.dtype), b_norm, B_HEADS) * jax.nn.silu(b_z)
    return jnp.concatenate([out_a, out_b], axis=-1) @ w_out


def _rope(x, cos, sin):
    half = x.shape[-1] // 2
    x1, x2 = x[..., :half], x[..., half:]
    return jnp.concatenate([x1 * cos - x2 * sin, x2 * cos + x1 * sin], axis=-1)


def _mla_mixer(h, w_in, q_norm, kv_norm, w_q_up, w_kv_up, w_out):
    f32 = jnp.float32
    b, s, _ = h.shape
    q_lat, kv_lat, k_rope, z = _split(h @ w_in, C_IN_SIZES)
    q = (_rmsnorm(q_lat, q_norm) @ w_q_up).reshape(b, s, C_HEADS, C_NOPE + C_ROPE)
    kv = (_rmsnorm(kv_lat, kv_norm) @ w_kv_up).reshape(b, s, C_HEADS, C_NOPE + C_V)
    q_nope, q_rope = q[..., :C_NOPE], q[..., C_NOPE:]
    k_nope, v = kv[..., :C_NOPE], kv[..., C_NOPE:]
    inv = ROPE_BASE ** (-jnp.arange(0, C_ROPE, 2, dtype=f32) / C_ROPE)
    ang = jnp.arange(s, dtype=f32)[:, None] * inv[None, :]
    cos, sin = jnp.cos(ang).astype(h.dtype), jnp.sin(ang).astype(h.dtype)
    q_rope = _rope(q_rope, cos[:, None, :], sin[:, None, :])
    k_rope = _rope(k_rope, cos, sin)
    scale = (C_NOPE + C_ROPE) ** -0.5
    nb = s // C_QBLOCK
    qn_blk = q_nope.reshape(b, nb, C_QBLOCK, C_HEADS, C_NOPE).transpose(1, 0, 2, 3, 4)
    qr_blk = q_rope.reshape(b, nb, C_QBLOCK, C_HEADS, C_ROPE).transpose(1, 0, 2, 3, 4)

    def block(args):
        qn, qr = args
        sc = jnp.einsum('bqhd,bkhd->bhqk', qn, k_nope) + jnp.einsum('bqhr,bkr->bhqk', qr, k_rope)
        p = jax.nn.softmax(sc.astype(f32) * scale, axis=-1).astype(v.dtype)
        return jnp.einsum('bhqk,bkhd->bqhd', p, v)

    o = lax.map(block, (qn_blk, qr_blk))
    o = o.transpose(1, 0, 2, 3, 4).reshape(b, s, C_HEADS * C_V)
    return (o * jax.nn.silu(z)) @ w_out


def _mem_cross_attn(h, mem_n, w_q, w_kv, w_o):
    b, s, _ = h.shape
    m = mem_n.shape[1]
    q = (h @ w_q).reshape(b, s, X_HEADS, X_DH)
    kv = (mem_n @ w_kv).reshape(b, m, 2, X_HEADS, X_DH)
    sc = jnp.einsum('bshd,bmhd->bhsm', q, kv[:, :, 0]).astype(jnp.float32) * (X_DH ** -0.5)
    p = jax.nn.softmax(sc, axis=-1).astype(h.dtype)
    o = jnp.einsum('bhsm,bmhd->bshd', p, kv[:, :, 1]).reshape(b, s, X_HEADS * X_DH)
    return o @ w_o


def _trunk(x, mem, p):
    for l in range(DEPTH):
        j = l // 2
        h = _rmsnorm(x, p['norm_mix'][l])
        if l % 2 == 0:
            y = _ab_mixer(h, p['ab_w_in'][j], p['ab_conv_w'][j], p['ab_conv_b'][j], p['a_igate_b'][j],
                          p['a_fgate_b'][j], p['a_ogate_b'][j], p['a_head_norm'][j], p['b_alpha_w2'][j],
                          p['b_alpha_b'][j], p['b_head_norm'][j], p['ab_w_out'][j])
        else:
            y = _mla_mixer(h, p['c_w_in'][j], p['c_q_norm'][j], p['c_kv_norm'][j], p['c_w_q_up'][j],
                           p['c_w_kv_up'][j], p['c_w_out'][j])
        x = x + y.astype(x.dtype)
        h = _rmsnorm(x, p['norm_cross'][l])
        mem_n = _rmsnorm(mem, p['norm_mem'][l])
        x = x + _mem_cross_attn(h, mem_n, p['x_w_q'][l], p['x_w_kv'][l], p['x_w_o'][l]).astype(x.dtype)
    return _rmsnorm(x, p['final_norm'])


def setup_inputs(seed: int = 0) -> dict:
    key = jax.random.key(seed)
    ks = iter(jax.random.split(key, 40))

    def nrm(shape, scale=1.0):
        return jax.random.normal(next(ks), shape, jnp.float32) * scale

    def gain(shape):
        return 1.0 + 0.01 * jax.random.normal(next(ks), shape, jnp.float32)

    D = D_MODEL
    return {
        'x_prompt': nrm((BATCH, SEQ, D)),
        'x_sample': nrm((DEC_BATCH, DEC_SEQ, D)),
        'mem_prompt': nrm((BATCH, N_MEM, D)),
        'mem_sample': nrm((DEC_BATCH, N_MEM, D)),
        'norm_mix': gain((DEPTH, D)),
        'norm_cross': gain((DEPTH, D)),
        'norm_mem': gain((DEPTH, D)),
        'ab_w_in': nrm((N_EVEN, D, AB_IN), D ** -0.5),
        'ab_conv_w': nrm((N_EVEN, A_CONV, 2 * A_QK), A_CONV ** -0.5),
        'ab_conv_b': nrm((N_EVEN, 2 * A_QK), 0.01),
        'a_igate_b': nrm((N_EVEN, 2, A_HEADS), 0.1),
        'a_fgate_b': 3.0 + nrm((N_EVEN, 2, A_HEADS), 0.1),
        'a_ogate_b': nrm((N_EVEN, D_A), 0.01),
        'a_head_norm': gain((N_EVEN, D_A)),
        'b_alpha_w2': nrm((N_EVEN, 2, B_GATE_RANK, B_QK), B_GATE_RANK ** -0.5),
        'b_alpha_b': nrm((N_EVEN, 2, B_QK), 0.01),
        'b_head_norm': gain((N_EVEN, D_B)),
        'ab_w_out': nrm((N_EVEN, D_A + D_B, D), (D_A + D_B) ** -0.5),
        'c_w_in': nrm((N_ODD, D, C_IN), D ** -0.5),
        'c_q_norm': gain((N_ODD, C_Q_LORA)),
        'c_kv_norm': gain((N_ODD, C_KV_LORA)),
        'c_w_q_up': nrm((N_ODD, C_Q_LORA, C_HEADS * (C_NOPE + C_ROPE)), C_Q_LORA ** -0.5),
        'c_w_kv_up': nrm((N_ODD, C_KV_LORA, C_HEADS * (C_NOPE + C_V)), C_KV_LORA ** -0.5),
        'c_w_out': nrm((N_ODD, C_HEADS * C_V, D), (C_HEADS * C_V) ** -0.5),
        'x_w_q': nrm((DEPTH, D, X_HEADS * X_DH), D ** -0.5),
        'x_w_kv': nrm((DEPTH, D, 2 * X_HEADS * X_DH), D ** -0.5),
        'x_w_o': nrm((DEPTH, X_HEADS * X_DH, D), (X_HEADS * X_DH) ** -0.5),
        'final_norm': gain((D,)),
    }


def reference(x_prompt, x_sample, mem_prompt, mem_sample, norm_mix, norm_cross, norm_mem,
              ab_w_in, ab_conv_w, ab_conv_b, a_igate_b, a_fgate_b, a_ogate_b, a_head_norm,
              b_alpha_w2, b_alpha_b, b_head_norm, ab_w_out,
              c_w_in, c_q_norm, c_kv_norm, c_w_q_up, c_w_kv_up, c_w_out,
              x_w_q, x_w_kv, x_w_o, final_norm):
    params = dict(norm_mix=norm_mix, norm_cross=norm_cross, norm_mem=norm_mem,
                  ab_w_in=ab_w_in, ab_conv_w=ab_conv_w, ab_conv_b=ab_conv_b,
                  a_igate_b=a_igate_b, a_fgate_b=a_fgate_b, a_ogate_b=a_ogate_b,
                  a_head_norm=a_head_norm, b_alpha_w2=b_alpha_w2, b_alpha_b=b_alpha_b,
                  b_head_norm=b_head_norm, ab_w_out=ab_w_out,
                  c_w_in=c_w_in, c_q_norm=c_q_norm, c_kv_norm=c_kv_norm,
                  c_w_q_up=c_w_q_up, c_w_kv_up=c_w_kv_up, c_w_out=c_w_out,
                  x_w_q=x_w_q, x_w_kv=x_w_kv, x_w_o=x_w_o, final_norm=final_norm)
    y_prompt = _trunk(x_prompt, mem_prompt, params)
    y_sample = _trunk(x_sample, mem_sample, params)
    return (y_prompt, y_sample)
```

```python
import functools

import numpy as np
import jax
import jax.numpy as jnp
from jax import lax
from jax.experimental import pallas as pl
from jax.experimental.pallas import tpu as pltpu

F32 = jnp.float32
MXU_DTYPE = jnp.bfloat16
EPS = 1e-6
HIGHEST = lax.Precision.HIGHEST

D_MODEL = 1024
DEPTH = 4
A_HEADS, A_DK, A_DV, A_CHUNK = 4, 128, 256, 128
B_HEADS, B_DK, B_DV, B_CHUNK = 4, 128, 256, 64
B_GATE_RANK, B_GATE_TAU = 16, 16.0
B_SUB = 16
C_HEADS, C_Q_LORA, C_KV_LORA, C_NOPE, C_ROPE, C_V = 8, 384, 256, 128, 64, 128
ROPE_BASE = 10000.0
X_HEADS, X_DH, N_MEM = 4, 256, 256

LANES = 128
V7X_VMEM_BYTES = 64 * 1024 * 1024

AB_QK, AB_O, AB_Z, AB_BQ, AB_BK, AB_BZ, AB_AV, AB_BV, AB_SMALL = (
    0, 1024, 2048, 3072, 3584, 4096, 5120, 6144, 7168)
AB_COLS = 7296
AB_TN = 2432
SM_I, SM_F, SM_LOW = 0, 8, 16
C_Z, C_KV, C_KR, C_KSW, C_QL = 0, 1024, 1280, 1408, 1536
C_COLS = 1920
C_TN = 640

TM_PROJ = 1024
TM_CONV = 512
TM_POST = 256
TM_CROSS = 512
TM_UP = 512
TM_OUT = 512
BQ, BK = 512, 512


def _cparams(sem, vmem_mb=48):
    assert vmem_mb * 1024 * 1024 <= V7X_VMEM_BYTES
    return pltpu.CompilerParams(dimension_semantics=sem, vmem_limit_bytes=vmem_mb * 1024 * 1024)


def _sigmoid(x):
    return 1.0 / (1.0 + jnp.exp(-x))


def _silu(x):
    return x * _sigmoid(x)


def _log_sigmoid(x):
    return jnp.minimum(x, 0.0) - jnp.log1p(jnp.exp(-jnp.abs(x)))


def _dot(a, b):
    return jnp.dot(a, b, preferred_element_type=F32)


def _dot_nt(a, b):
    return lax.dot_general(a, b, (((1,), (1,)), ((), ())), preferred_element_type=F32)


def _dot_tn(a, b):
    return lax.dot_general(a, b, (((0,), (0,)), ((), ())), preferred_element_type=F32)


def _rms(x, g):
    ms = jnp.mean(x * x, axis=-1, keepdims=True)
    return x * lax.rsqrt(ms + EPS) * g


def _head_rms(x, g, n_heads):
    w = x.shape[-1] // n_heads
    parts = []
    for h in range(n_heads):
        xs = x[:, h * w:(h + 1) * w]
        ms = jnp.mean(xs * xs, axis=-1, keepdims=True)
        parts.append(xs * lax.rsqrt(ms + EPS))
    return jnp.concatenate(parts, axis=-1) * g


def _norm_matmul_kernel(x_ref, g_ref, w_ref, o_ref, h_ref):
    @pl.when(pl.program_id(1) == 0)
    def _():
        h_ref[...] = _rms(x_ref[...], g_ref[...]).astype(h_ref.dtype)

    o_ref[...] = _dot(h_ref[...], w_ref[...]).astype(o_ref.dtype)


def _norm_matmul(x, g, w, *, tm, tn, out_dtype, x_col_block=0):
    m = x.shape[0]
    k, n = w.shape
    tm = min(tm, m)
    return pl.pallas_call(
        _norm_matmul_kernel,
        grid=(m // tm, n // tn),
        in_specs=[pl.BlockSpec((tm, k), lambda i, j: (i, x_col_block)),
                  pl.BlockSpec((1, k), lambda i, j: (0, 0)),
                  pl.BlockSpec((k, tn), lambda i, j: (0, j))],
        out_specs=pl.BlockSpec((tm, tn), lambda i, j: (i, j)),
        out_shape=jax.ShapeDtypeStruct((m, n), out_dtype),
        scratch_shapes=[pltpu.VMEM((tm, k), MXU_DTYPE)],
        compiler_params=_cparams(("parallel", "arbitrary"), 56),
        name="norm_matmul",
    )(x, g.reshape(1, k), w)


def _matmul_res_kernel(a_ref, w_ref, x_ref, o_ref):
    o_ref[...] = x_ref[...] + _dot(a_ref[...], w_ref[...])


def _matmul_residual(a, w, x, *, tm):
    m, k = a.shape
    n = w.shape[1]
    tm = min(tm, m)
    return pl.pallas_call(
        _matmul_res_kernel,
        grid=(m // tm,),
        in_specs=[pl.BlockSpec((tm, k), lambda i: (i, 0)),
                  pl.BlockSpec((k, n), lambda i: (0, 0)),
                  pl.BlockSpec((tm, n), lambda i: (i, 0))],
        out_specs=pl.BlockSpec((tm, n), lambda i: (i, 0)),
        out_shape=jax.ShapeDtypeStruct((m, n), F32),
        compiler_params=_cparams(("parallel",), 40),
        name="matmul_residual",
    )(a, w, x)


def _conv_kernel(x_ref, xp_ref, xn_ref, w_ref, b_ref, o_ref, *, tiles_per_seq):
    i = pl.program_id(0)
    x = x_ref[...]
    ts = x.shape[0]
    t_in_seq = i % tiles_per_seq
    prev_row = jnp.where(t_in_seq == 0, 0.0, xp_ref[7:8, :])
    next_row = jnp.where(t_in_seq == tiles_per_seq - 1, 0.0, xn_ref[0:1, :])
    rows = lax.broadcasted_iota(jnp.int32, x.shape, 0)
    x_prev = jnp.where(rows == 0, prev_row, pltpu.roll(x, 1, axis=0))
    x_next = jnp.where(rows == ts - 1, next_row, pltpu.roll(x, ts - 1, axis=0))
    y = w_ref[0:1, :] * x_prev + w_ref[1:2, :] * x + w_ref[2:3, :] * x_next + b_ref[...]
    y = _silu(y)
    half = y.shape[1] // 2
    o_ref[:, :half] = (y[:, :half] * (A_DK ** -0.5)).astype(o_ref.dtype)
    o_ref[:, half:] = y[:, half:].astype(o_ref.dtype)


def _conv_silu(proj, conv_w, conv_b, *, seq):
    m = proj.shape[0]
    c = conv_w.shape[1]
    ts = min(TM_CONV, seq)
    nt = m // ts
    sub = ts // 8
    return pl.pallas_call(
        functools.partial(_conv_kernel, tiles_per_seq=seq // ts),
        grid=(nt,),
        in_specs=[pl.BlockSpec((ts, c), lambda i: (i, 0)),
                  pl.BlockSpec((8, c), lambda i: (jnp.maximum(i * sub - 1, 0), 0)),
                  pl.BlockSpec((8, c), lambda i: (jnp.minimum((i + 1) * sub, nt * sub - 1), 0)),
                  pl.BlockSpec((3, c), lambda i: (0, 0)),
                  pl.BlockSpec((1, c), lambda i: (0, 0))],
        out_specs=pl.BlockSpec((ts, c), lambda i: (i, 0)),
        out_shape=jax.ShapeDtypeStruct((m, c), MXU_DTYPE),
        compiler_params=_cparams(("parallel",), 32),
        name="conv_silu",
    )(proj, proj, proj, conv_w, conv_b.reshape(1, c))


def _mlstm_kernel(qk_f, v_f, g_f, qk_b, v_b, g_b, gbias_ref, h_f, h_b, c_scr, n_scr, m_scr):
    L = A_CHUNK

    @pl.when(pl.program_id(1) == 0)
    def _():
        c_scr[...] = jnp.zeros_like(c_scr)
        n_scr[...] = jnp.zeros_like(n_scr)
        m_scr[...] = jnp.zeros_like(m_scr)

    row = lax.broadcasted_iota(jnp.int32, (L, L), 0)
    col = lax.broadcasted_iota(jnp.int32, (L, L), 1)
    for d, (qk_ref, v_ref, g_ref, h_ref) in enumerate(((qk_f, v_f, g_f, h_f), (qk_b, v_b, g_b, h_b))):
        mask = (row >= col) if d == 0 else (row <= col)
        gates = g_ref[...] + gbias_ref[...]
        logf = _log_sigmoid(gates)
        csum = jnp.dot(mask.astype(F32), logf, precision=HIGHEST, preferred_element_type=F32)
        csum_t = csum.T
        gates_t = gates.T
        for h in range(A_HEADS):
            s = d * A_HEADS + h
            ci, cf = SM_I + s, SM_F + s
            bl_c, bl_r = csum[:, cf:cf + 1], csum_t[cf:cf + 1, :]
            i_c, i_r = gates[:, ci:ci + 1], gates_t[ci:ci + 1, :]
            g = bl_c[L - 1:L, :] if d == 0 else bl_c[0:1, :]
            m_prev = m_scr[s][0:1, 0:1]
            n_prev = n_scr[s][0:1, :]
            c_prev = c_scr[s]
            q = qk_ref[:, h * A_DK:(h + 1) * A_DK]
            k = qk_ref[:, (A_HEADS + h) * A_DK:(A_HEADS + h + 1) * A_DK]
            v = v_ref[:, h * A_DV:(h + 1) * A_DV].astype(MXU_DTYPE)
            kf = k.astype(F32)

            log_d = jnp.where(mask, bl_c - bl_r + i_r, -jnp.inf)
            log_inter = bl_c + m_prev
            m_t = jnp.maximum(log_inter, jnp.max(log_d, axis=1, keepdims=True))
            sc = _dot_nt(q, k) * jnp.exp(log_d - m_t)
            w_inter = jnp.exp(log_inter - m_t)
            num = _dot(sc.astype(MXU_DTYPE), v) + w_inter * _dot(q, c_prev.astype(MXU_DTYPE))
            qn = jnp.sum(q.astype(F32) * n_prev, axis=1, keepdims=True)
            den = jnp.sum(sc, axis=1, keepdims=True) + w_inter * qn
            h_ref[:, h * A_DV:(h + 1) * A_DV] = num / jnp.maximum(jnp.abs(den), jnp.exp(-m_t))

            log_w = g - bl_c + i_c
            m_new = jnp.maximum(g + m_prev, jnp.max(log_w, axis=0, keepdims=True))
            w = jnp.exp(log_w - m_new)
            decay = jnp.exp(g + m_prev - m_new)
            kw = w * kf
            c_scr[s] = decay * c_prev + _dot_tn(kw.astype(MXU_DTYPE), v)
            n_scr[s] = jnp.broadcast_to(decay * n_prev + jnp.sum(kw, axis=0, keepdims=True), n_scr.shape[1:])
            m_scr[s] = jnp.broadcast_to(m_new, m_scr.shape[1:])


def _mlstm(qk, proj, gbias, *, batch, seq):
    m = qk.shape[0]
    L = A_CHUNK
    nc = seq // L
    d_a = A_HEADS * A_DV
    v_blk = AB_AV // d_a
    sm_blk = AB_SMALL // LANES
    fwd = lambda b, c: b * nc + c
    bwd = lambda b, c: b * nc + (nc - 1 - c)
    out = jax.ShapeDtypeStruct((m, d_a), F32)
    return pl.pallas_call(
        _mlstm_kernel,
        grid=(batch, nc),
        in_specs=[pl.BlockSpec((L, 2 * A_HEADS * A_DK), lambda b, c: (fwd(b, c), 0)),
                  pl.BlockSpec((L, d_a), lambda b, c: (fwd(b, c), v_blk)),
                  pl.BlockSpec((L, LANES), lambda b, c: (fwd(b, c), sm_blk)),
                  pl.BlockSpec((L, 2 * A_HEADS * A_DK), lambda b, c: (bwd(b, c), 0)),
                  pl.BlockSpec((L, d_a), lambda b, c: (bwd(b, c), v_blk)),
                  pl.BlockSpec((L, LANES), lambda b, c: (bwd(b, c), sm_blk)),
                  pl.BlockSpec((1, LANES), lambda b, c: (0, 0))],
        out_specs=[pl.BlockSpec((L, d_a), lambda b, c: (fwd(b, c), 0)),
                   pl.BlockSpec((L, d_a), lambda b, c: (bwd(b, c), 0))],
        out_shape=[out, out],
        scratch_shapes=[pltpu.VMEM((2 * A_HEADS, A_DK, A_DV), F32),
                        pltpu.VMEM((2 * A_HEADS, 8, A_DK), F32),
                        pltpu.VMEM((2 * A_HEADS, 8, LANES), F32)],
        compiler_params=_cparams(("parallel", "arbitrary"), 32),
        name="mlstm",
    )(qk, proj, proj, qk, proj, proj, gbias)


def _gla_kernel(q_f, k_f, v_f, g_f, q_b, k_b, v_b, g_b, w2_ref, ab_ref, o_f, o_b, st_scr):
    L, SB = B_CHUNK, B_SUB
    nsb = L // SB
    hk = B_HEADS * B_DK

    @pl.when(pl.program_id(1) == 0)
    def _():
        st_scr[...] = jnp.zeros_like(st_scr)

    row = lax.broadcasted_iota(jnp.int32, (L, L), 0)
    col = lax.broadcasted_iota(jnp.int32, (L, L), 1)
    lane_sb = lax.broadcasted_iota(jnp.int32, (SB, L), 1)
    for d, (q_ref, k_ref, v_ref, g_ref, o_ref) in enumerate(((q_f, k_f, v_f, g_f, o_f), (q_b, k_b, v_b, g_b, o_b))):
        mask = (row >= col) if d == 0 else (row <= col)
        pre = _dot(g_ref[...].astype(MXU_DTYPE), w2_ref[:, d * hk:(d + 1) * hk]) + ab_ref[:, d * hk:(d + 1) * hk]
        log_a = _log_sigmoid(pre) / B_GATE_TAU
        bc_all = jnp.dot(mask.astype(F32), log_a, precision=HIGHEST, preferred_element_type=F32)
        for h in range(B_HEADS):
            s = d * B_HEADS + h
            q = q_ref[:, h * B_DK:(h + 1) * B_DK] * (B_DK ** -0.5)
            k = k_ref[:, h * B_DK:(h + 1) * B_DK]
            v = v_ref[:, h * B_DV:(h + 1) * B_DV].astype(MXU_DTYPE)
            b = bc_all[:, h * B_DK:(h + 1) * B_DK]
            g = b[L - 1:L, :] if d == 0 else b[0:1, :]
            st = st_scr[s]

            o = _dot_nt((q * jnp.exp(b)).astype(MXU_DTYPE), st.astype(MXU_DTYPE))

            att_rows = []
            for blk in range(nsb):
                lo, hi = blk * SB, (blk + 1) * SB
                b_i, q_i, k_i = b[lo:hi], q[lo:hi], k[lo:hi]
                if d == 0 and blk > 0:
                    ref = b[lo - 1:lo]
                    kt = k[:lo] * jnp.exp(ref - b[:lo])
                    kt = jnp.concatenate([kt, jnp.zeros((L - lo, B_DK), F32)], axis=0)
                elif d == 1 and blk < nsb - 1:
                    ref = b[hi:hi + 1]
                    kt = k[hi:] * jnp.exp(ref - b[hi:])
                    kt = jnp.concatenate([jnp.zeros((hi, B_DK), F32), kt], axis=0)
                else:
                    ref = None
                if ref is None:
                    att = jnp.zeros((SB, L), F32)
                else:
                    att = _dot_nt((q_i * jnp.exp(b_i - ref)).astype(MXU_DTYPE), kt.astype(MXU_DTYPE))
                for j in range(SB):
                    wgt = jnp.exp(jnp.minimum(b_i - b_i[j:j + 1], 0.0))
                    colv = jnp.sum(q_i * wgt * k_i[j:j + 1], axis=1, keepdims=True)
                    att = jnp.where(lane_sb == lo + j, colv, att)
                att_rows.append(att)
            att = jnp.where(mask, jnp.concatenate(att_rows, axis=0), 0.0)
            o_ref[:, h * B_DV:(h + 1) * B_DV] = o + _dot(att.astype(MXU_DTYPE), v)

            kd = (k * jnp.exp(g - b)).astype(MXU_DTYPE)
            st_scr[s] = jnp.exp(g) * st + _dot_tn(v, kd)


def _gla(proj, w2, ab, *, batch, seq):
    m = proj.shape[0]
    L = B_CHUNK
    nc = seq // L
    hk = B_HEADS * B_DK
    d_b = B_HEADS * B_DV
    q_blk, k_blk, v_blk, sm_blk = AB_BQ // hk, AB_BK // hk, AB_BV // d_b, AB_SMALL // LANES
    fwd = lambda b, c: b * nc + c
    bwd = lambda b, c: b * nc + (nc - 1 - c)

    def specs(idx):
        return [pl.BlockSpec((L, hk), lambda b, c: (idx(b, c), q_blk)),
                pl.BlockSpec((L, hk), lambda b, c: (idx(b, c), k_blk)),
                pl.BlockSpec((L, d_b), lambda b, c: (idx(b, c), v_blk)),
                pl.BlockSpec((L, LANES), lambda b, c: (idx(b, c), sm_blk))]

    out = jax.ShapeDtypeStruct((m, d_b), F32)
    return pl.pallas_call(
        _gla_kernel,
        grid=(batch, nc),
        in_specs=specs(fwd) + specs(bwd) + [pl.BlockSpec((LANES, 2 * hk), lambda b, c: (0, 0)),
                                            pl.BlockSpec((1, 2 * hk), lambda b, c: (0, 0))],
        out_specs=[pl.BlockSpec((L, d_b), lambda b, c: (fwd(b, c), 0)),
                   pl.BlockSpec((L, d_b), lambda b, c: (bwd(b, c), 0))],
        out_shape=[out, out],
        scratch_shapes=[pltpu.VMEM((2 * B_HEADS, B_DV, B_DK), F32)],
        compiler_params=_cparams(("parallel", "arbitrary"), 32),
        name="gla",
    )(proj, proj, proj, proj, proj, proj, proj, proj, w2, ab)


def _ab_post_kernel(hf_ref, hb_ref, ao_ref, az_ref, of_ref, ob_ref, bz_ref, ogb_ref, an_ref, bn_ref,
                    w_ref, x_ref, o_ref):
    d_a = A_HEADS * A_DV
    out_a = _sigmoid(ao_ref[...] + ogb_ref[...]) * (hf_ref[...] + hb_ref[...])
    out_a = _head_rms(out_a, an_ref[...], A_HEADS) * _silu(az_ref[...])
    out_b = _head_rms(of_ref[...] + ob_ref[...], bn_ref[...], B_HEADS) * _silu(bz_ref[...])
    y = _dot(out_a.astype(MXU_DTYPE), w_ref[:d_a, :]) + _dot(out_b.astype(MXU_DTYPE), w_ref[d_a:, :])
    o_ref[...] = x_ref[...] + y


def _ab_post(h_f, h_b, o_f, o_b, proj, ogate_b, a_norm, b_norm, w_out, x):
    m, d = x.shape
    tm = min(TM_POST, m)
    row = lambda i: (i, 0)
    vec = pl.BlockSpec((1, d), lambda i: (0, 0))
    blk = lambda cb: pl.BlockSpec((tm, d), lambda i: (i, cb))
    return pl.pallas_call(
        _ab_post_kernel,
        grid=(m // tm,),
        in_specs=[pl.BlockSpec((tm, d), row), pl.BlockSpec((tm, d), row), blk(AB_O // d), blk(AB_Z // d),
                  pl.BlockSpec((tm, d), row), pl.BlockSpec((tm, d), row), blk(AB_BZ // d),
                  vec, vec, vec, pl.BlockSpec(w_out.shape, lambda i: (0, 0)), pl.BlockSpec((tm, d), row)],
        out_specs=pl.BlockSpec((tm, d), row),
        out_shape=jax.ShapeDtypeStruct((m, d), F32),
        compiler_params=_cparams(("parallel",), 48),
        name="ab_post",
    )(h_f, h_b, proj, proj, o_f, o_b, proj, ogate_b.reshape(1, d), a_norm.reshape(1, d), b_norm.reshape(1, d),
      w_out, x)


def _cross_kernel(x_ref, g_ref, wq_ref, kv_ref, wo_ref, fin_ref, o_ref, *, final):
    x = x_ref[...]
    hn = _rms(x, g_ref[...]).astype(MXU_DTYPE)
    q = _dot(hn, wq_ref[...]).astype(MXU_DTYPE)
    d = X_HEADS * X_DH
    outs = []
    for h in range(X_HEADS):
        k = kv_ref[:, h * X_DH:(h + 1) * X_DH]
        v = kv_ref[:, d + h * X_DH:d + (h + 1) * X_DH]
        sc = _dot_nt(q[:, h * X_DH:(h + 1) * X_DH], k) * (X_DH ** -0.5)
        e = jnp.exp(sc - jnp.max(sc, axis=-1, keepdims=True))
        p = e / jnp.sum(e, axis=-1, keepdims=True)
        outs.append(_dot(p.astype(MXU_DTYPE), v))
    o = jnp.concatenate(outs, axis=-1).astype(MXU_DTYPE)
    y = x + _dot(o, wo_ref[...])
    if final:
        y = _rms(y, fin_ref[...])
    o_ref[...] = y


def _cross_attn(x, kv, g, w_q, w_o, fin, *, seq, final):
    m, d = x.shape
    tm = min(TM_CROSS, seq)
    tiles_per_seq = seq // tm
    full = lambda i: (0, 0)
    return pl.pallas_call(
        functools.partial(_cross_kernel, final=final),
        grid=(m // tm,),
        in_specs=[pl.BlockSpec((tm, d), lambda i: (i, 0)),
                  pl.BlockSpec((1, d), full),
                  pl.BlockSpec(w_q.shape, full),
                  pl.BlockSpec((N_MEM, kv.shape[1]), lambda i: (i // tiles_per_seq, 0)),
                  pl.BlockSpec(w_o.shape, full),
                  pl.BlockSpec((1, d), full)],
        out_specs=pl.BlockSpec((tm, d), lambda i: (i, 0)),
        out_shape=jax.ShapeDtypeStruct((m, d), F32),
        compiler_params=_cparams(("parallel",), 48),
        name="cross_attn",
    )(x, g.reshape(1, d), w_q, kv, w_o, fin.reshape(1, d))


def _q_up_kernel(x_ref, g_ref, w_ref, cos_ref, sin_ref, o_ref, h_ref):
    @pl.when(pl.program_id(1) == 0)
    def _():
        h_ref[...] = _rms(x_ref[...], g_ref[...]).astype(h_ref.dtype)

    r = _dot(h_ref[...], w_ref[...])
    o_ref[0, 0, :, :C_NOPE] = r[:, :C_NOPE].astype(o_ref.dtype)
    rope = r[:, C_NOPE:C_NOPE + LANES] * cos_ref[...] + r[:, C_NOPE + LANES:] * sin_ref[...]
    o_ref[0, 0, :, C_NOPE:] = rope.astype(o_ref.dtype)


def _q_up(cin, q_norm, w, cos, sin, *, batch, seq):
    tm = min(TM_UP, seq)
    tps = seq // tm
    wq = C_NOPE + 2 * LANES
    return pl.pallas_call(
        _q_up_kernel,
        grid=(batch * tps, C_HEADS),
        in_specs=[pl.BlockSpec((tm, C_Q_LORA), lambda i, h: (i, C_QL // C_Q_LORA)),
                  pl.BlockSpec((1, C_Q_LORA), lambda i, h: (0, 0)),
                  pl.BlockSpec((C_Q_LORA, wq), lambda i, h: (0, h)),
                  pl.BlockSpec((tm, LANES), lambda i, h: (i % tps, 0)),
                  pl.BlockSpec((tm, LANES), lambda i, h: (i % tps, 0))],
        out_specs=pl.BlockSpec((1, 1, tm, C_NOPE + LANES), lambda i, h: (i // tps, h, i % tps, 0)),
        out_shape=jax.ShapeDtypeStruct((batch, C_HEADS, seq, C_NOPE + LANES), MXU_DTYPE),
        scratch_shapes=[pltpu.VMEM((tm, C_Q_LORA), MXU_DTYPE)],
        compiler_params=_cparams(("parallel", "arbitrary"), 32),
        name="mla_q_up",
    )(cin, q_norm.reshape(1, C_Q_LORA), w, cos, sin)


def _kv_up_kernel(x_ref, g_ref, w_ref, kr_ref, ksw_ref, cos_ref, sin_ref, k_ref, v_ref, h_ref, rope_ref):
    @pl.when(pl.program_id(1) == 0)
    def _():
        h_ref[...] = _rms(x_ref[...], g_ref[...]).astype(h_ref.dtype)
        rope_ref[...] = (kr_ref[...] * cos_ref[...] + ksw_ref[...] * sin_ref[...]).astype(rope_ref.dtype)

    r = _dot(h_ref[...], w_ref[...])
    k_ref[0, 0, :, :C_NOPE] = r[:, :C_NOPE].astype(k_ref.dtype)
    k_ref[0, 0, :, C_NOPE:] = rope_ref[...]
    v_ref[0, 0] = r[:, C_NOPE:].astype(v_ref.dtype)


def _kv_up(cin, kv_norm, w, cos, sin, *, batch, seq):
    tm = min(TM_UP, seq)
    tps = seq // tm
    return pl.pallas_call(
        _kv_up_kernel,
        grid=(batch * tps, C_HEADS),
        in_specs=[pl.BlockSpec((tm, C_KV_LORA), lambda i, h: (i, C_KV // C_KV_LORA)),
                  pl.BlockSpec((1, C_KV_LORA), lambda i, h: (0, 0)),
                  pl.BlockSpec((C_KV_LORA, C_NOPE + C_V), lambda i, h: (0, h)),
                  pl.BlockSpec((tm, LANES), lambda i, h: (i, C_KR // LANES)),
                  pl.BlockSpec((tm, LANES), lambda i, h: (i, C_KSW // LANES)),
                  pl.BlockSpec((tm, LANES), lambda i, h: (i % tps, 0)),
                  pl.BlockSpec((tm, LANES), lambda i, h: (i % tps, 0))],
        out_specs=[pl.BlockSpec((1, 1, tm, C_NOPE + LANES), lambda i, h: (i // tps, h, i % tps, 0)),
                   pl.BlockSpec((1, 1, tm, C_V), lambda i, h: (i // tps, h, i % tps, 0))],
        out_shape=[jax.ShapeDtypeStruct((batch, C_HEADS, seq, C_NOPE + LANES), MXU_DTYPE),
                   jax.ShapeDtypeStruct((batch, C_HEADS, seq, C_V), MXU_DTYPE)],
        scratch_shapes=[pltpu.VMEM((tm, C_KV_LORA), MXU_DTYPE), pltpu.VMEM((tm, LANES), MXU_DTYPE)],
        compiler_params=_cparams(("parallel", "arbitrary"), 32),
        name="mla_kv_up",
    )(cin, kv_norm.reshape(1, C_KV_LORA), w, cin, cin, cos, sin)


def _flash_kernel(q_ref, k_ref, v_ref, z_ref, o_ref, m_scr, l_scr, acc_scr):
    j = pl.program_id(2)
    scale = (C_NOPE + C_ROPE) ** -0.5

    @pl.when(j == 0)
    def _():
        m_scr[...] = jnp.full_like(m_scr, -jnp.inf)
        l_scr[...] = jnp.zeros_like(l_scr)
        acc_scr[...] = jnp.zeros_like(acc_scr)

    def head(h, carry):
        s = _dot_nt(q_ref[0, h], k_ref[0, h])
        m_prev = m_scr[h]
        m_new = jnp.maximum(m_prev, jnp.max(s, axis=1, keepdims=True))
        p = jnp.exp((s - m_new) * scale)
        alpha = jnp.exp((m_prev - m_new) * scale)
        l_scr[h] = alpha * l_scr[h] + jnp.sum(p, axis=1, keepdims=True)
        acc_scr[h] = alpha * acc_scr[h] + _dot(p.astype(MXU_DTYPE), v_ref[0, h])
        m_scr[h] = m_new
        return carry

    lax.fori_loop(0, C_HEADS, head, 0)

    @pl.when(j == pl.num_programs(2) - 1)
    def _():
        for h in range(C_HEADS):
            o = acc_scr[h] / l_scr[h]
            zh = z_ref[:, h * C_V:(h + 1) * C_V]
            o_ref[:, h * C_V:(h + 1) * C_V] = (o * _silu(zh)).astype(o_ref.dtype)


def _flash(q, k, v, cin, *, batch, seq):
    bq, bk = min(BQ, seq), min(BK, seq)
    nq, nk = seq // bq, seq // bk
    d = C_HEADS * C_V
    dq = q.shape[-1]
    return pl.pallas_call(
        _flash_kernel,
        grid=(batch, nq, nk),
        in_specs=[pl.BlockSpec((1, C_HEADS, bq, dq), lambda b, i, j: (b, 0, i, 0)),
                  pl.BlockSpec((1, C_HEADS, bk, dq), lambda b, i, j: (b, 0, j, 0)),
                  pl.BlockSpec((1, C_HEADS, bk, C_V), lambda b, i, j: (b, 0, j, 0)),
                  pl.BlockSpec((bq, d), lambda b, i, j: (b * nq + i, C_Z // d))],
        out_specs=pl.BlockSpec((bq, d), lambda b, i, j: (b * nq + i, 0)),
        out_shape=jax.ShapeDtypeStruct((batch * seq, d), MXU_DTYPE),
        scratch_shapes=[pltpu.VMEM((C_HEADS, bq, 1), F32), pltpu.VMEM((C_HEADS, bq, 1), F32),
                        pltpu.VMEM((C_HEADS, bq, C_V), F32)],
        compiler_params=_cparams(("parallel", "parallel", "arbitrary"), 48),
        name="mla_flash",
    )(q, k, v, cin)


def _split_cols(w, sizes):
    idx = np.cumsum(np.array(sizes))[:-1].tolist()
    return jnp.split(w, idx, axis=-1)


def _rope_swap(w):
    half = w.shape[-1] // 2
    return jnp.concatenate([-w[..., half:], w[..., :half]], axis=-1)


def _pad_cols(w, n):
    return jnp.pad(w, [(0, 0)] * (w.ndim - 1) + [(0, n - w.shape[-1])])


def _prep_ab(w_in, alpha_w2, alpha_b, igate_b, fgate_b):
    sizes = (2 * A_HEADS * A_DK, A_HEADS * A_DV, 2 * A_HEADS, 2 * A_HEADS, A_HEADS * A_DV, A_HEADS * A_DV,
             B_HEADS * B_DK, B_HEADS * B_DK, B_HEADS * B_DV, 2 * B_GATE_RANK, B_HEADS * B_DV)
    a_qk, a_v, a_i, a_f, a_o, a_z, b_q, b_k, b_v, b_low, b_z = _split_cols(w_in, sizes)
    small = _pad_cols(jnp.concatenate([a_i, a_f, b_low], axis=-1), LANES)
    w = jnp.concatenate([a_qk, a_o, a_z, b_q, b_k, b_z, a_v, b_v, small], axis=-1).astype(MXU_DTYPE)
    hk = B_HEADS * B_DK
    w2 = jnp.zeros((LANES, 2 * hk), F32)
    w2 = w2.at[SM_LOW:SM_LOW + B_GATE_RANK, :hk].set(alpha_w2[0])
    w2 = w2.at[SM_LOW + B_GATE_RANK:SM_LOW + 2 * B_GATE_RANK, hk:].set(alpha_w2[1])
    ab = alpha_b.reshape(1, 2 * hk)
    gbias = _pad_cols(jnp.concatenate([igate_b.reshape(1, -1), fgate_b.reshape(1, -1)], axis=-1), LANES)
    return w, w2.astype(MXU_DTYPE), ab, gbias


def _prep_c(w_in, w_q_up, w_kv_up):
    q_lat, kv_lat, k_rope, z = _split_cols(w_in, (C_Q_LORA, C_KV_LORA, C_ROPE, C_HEADS * C_V))
    w_c = jnp.concatenate([z, kv_lat, _pad_cols(k_rope, LANES), _pad_cols(_rope_swap(k_rope), LANES), q_lat],
                          axis=-1).astype(MXU_DTYPE)
    wq = w_q_up.reshape(C_Q_LORA, C_HEADS, C_NOPE + C_ROPE)
    nope, rope = wq[..., :C_NOPE], wq[..., C_NOPE:]
    wq = jnp.concatenate([nope, _pad_cols(rope, LANES), _pad_cols(_rope_swap(rope), LANES)], axis=-1)
    return w_c, wq.reshape(C_Q_LORA, -1).astype(MXU_DTYPE), w_kv_up.astype(MXU_DTYPE)


def _rope_tables(seq):
    inv = ROPE_BASE ** (-jnp.arange(0, C_ROPE, 2, dtype=F32) / C_ROPE)
    ang = jnp.arange(seq, dtype=F32)[:, None] * inv[None, :]
    cos, sin = jnp.cos(ang), jnp.sin(ang)
    return (_pad_cols(jnp.concatenate([cos, cos], axis=-1), LANES),
            _pad_cols(jnp.concatenate([sin, sin], axis=-1), LANES))


def _trunk(x, mem, p, prep):
    batch, seq, d = x.shape
    x = x.reshape(batch * seq, d)
    mem = mem.reshape(batch * N_MEM, d)
    cos, sin = _rope_tables(seq)
    for l in range(DEPTH):
        j = l // 2
        if l % 2 == 0:
            w_in, w2, ab, gbias = prep['ab'][j]
            proj = _norm_matmul(x, p['norm_mix'][l], w_in, tm=TM_PROJ, tn=AB_TN, out_dtype=F32)
            qk = _conv_silu(proj, p['ab_conv_w'][j], p['ab_conv_b'][j], seq=seq)
            h_f, h_b = _mlstm(qk, proj, gbias, batch=batch, seq=seq)
            o_f, o_b = _gla(proj, w2, ab, batch=batch, seq=seq)
            x = _ab_post(h_f, h_b, o_f, o_b, proj, p['a_ogate_b'][j], p['a_head_norm'][j], p['b_head_norm'][j],
                         prep['ab_w_out'][j], x)
        else:
            w_c, w_q, w_kv = prep['c'][j]
            cin = _norm_matmul(x, p['norm_mix'][l], w_c, tm=TM_PROJ, tn=C_TN, out_dtype=F32)
            q = _q_up(cin, p['c_q_norm'][j], w_q, cos, sin, batch=batch, seq=seq)
            k, v = _kv_up(cin, p['c_kv_norm'][j], w_kv, cos, sin, batch=batch, seq=seq)
            og = _flash(q, k, v, cin, batch=batch, seq=seq)
            x = _matmul_residual(og, prep['c_w_out'][j], x, tm=TM_OUT)
        kv = _norm_matmul(mem, p['norm_mem'][l], prep['x_w_kv'][l], tm=N_MEM, tn=1024, out_dtype=MXU_DTYPE)
        x = _cross_attn(x, kv, p['norm_cross'][l], prep['x_w_q'][l], prep['x_w_o'][l], p['final_norm'],
                        seq=seq, final=(l == DEPTH - 1))
    return x.reshape(batch, seq, d)


def kernel(x_prompt, x_sample, mem_prompt, mem_sample, norm_mix, norm_cross, norm_mem, ab_w_in, ab_conv_w, ab_conv_b, a_igate_b, a_fgate_b, a_ogate_b, a_head_norm, b_alpha_w2, b_alpha_b, b_head_norm, ab_w_out, c_w_in, c_q_norm, c_kv_norm, c_w_q_up, c_w_kv_up, c_w_out, x_w_q, x_w_kv, x_w_o, final_norm):
    p = dict(norm_mix=norm_mix, norm_cross=norm_cross, norm_mem=norm_mem, ab_conv_w=ab_conv_w,
             ab_conv_b=ab_conv_b, a_ogate_b=a_ogate_b, a_head_norm=a_head_norm, b_head_norm=b_head_norm,
             c_q_norm=c_q_norm, c_kv_norm=c_kv_norm, final_norm=final_norm)
    n_even, n_odd = ab_w_in.shape[0], c_w_in.shape[0]
    prep = dict(
        ab=[_prep_ab(ab_w_in[j], b_alpha_w2[j], b_alpha_b[j], a_igate_b[j], a_fgate_b[j]) for j in range(n_even)],
        ab_w_out=[ab_w_out[j].astype(MXU_DTYPE) for j in range(n_even)],
        c=[_prep_c(c_w_in[j], c_w_q_up[j], c_w_kv_up[j]) for j in range(n_odd)],
        c_w_out=[c_w_out[j].astype(MXU_DTYPE) for j in range(n_odd)],
        x_w_q=[x_w_q[l].astype(MXU_DTYPE) for l in range(DEPTH)],
        x_w_kv=[x_w_kv[l].astype(MXU_DTYPE) for l in range(DEPTH)],
        x_w_o=[x_w_o[l].astype(MXU_DTYPE) for l in range(DEPTH)],
    )
    return (_trunk(x_prompt, mem_prompt, p, prep), _trunk(x_sample, mem_sample, p, prep))
```

```python
import functools

import numpy as np
import jax
import jax.numpy as jnp
from jax import lax
from jax.experimental import pallas as pl
from jax.experimental.pallas import tpu as pltpu

F32 = jnp.float32
MXU_DTYPE = jnp.bfloat16
EPS = 1e-6
HIGHEST = lax.Precision.HIGHEST

D_MODEL = 1024
DEPTH = 4
A_HEADS, A_DK, A_DV, A_CHUNK = 4, 128, 256, 128
B_HEADS, B_DK, B_DV, B_CHUNK = 4, 128, 256, 64
B_GATE_RANK, B_GATE_TAU = 16, 16.0
B_SUB = 16
C_HEADS, C_Q_LORA, C_KV_LORA, C_NOPE, C_ROPE, C_V = 8, 384, 256, 128, 64, 128
ROPE_BASE = 10000.0
X_HEADS, X_DH, N_MEM = 4, 256, 256

LANES = 128
V7X_VMEM_BYTES = 64 * 1024 * 1024

AB_QK, AB_O, AB_Z, AB_BQ, AB_BK, AB_BZ, AB_AV, AB_BV, AB_SMALL = (
    0, 1024, 2048, 3072, 3584, 4096, 5120, 6144, 7168)
AB_COLS = 7296
AB_TN = 2432
SM_I, SM_F, SM_LOW = 0, 8, 16
C_Z, C_KV, C_KR, C_KSW, C_QL = 0, 1024, 1280, 1408, 1536
C_COLS = 1920
C_TN = 640

TM_PROJ = 1024
TM_CONV = 512
TM_POST = 256
TM_CROSS = 512
TM_UP = 512
TM_OUT = 512
BQ, BK = 512, 512


def _cparams(sem, vmem_mb=48):
    assert vmem_mb * 1024 * 1024 <= V7X_VMEM_BYTES
    return pltpu.CompilerParams(dimension_semantics=sem, vmem_limit_bytes=vmem_mb * 1024 * 1024)


def _sigmoid(x):
    return 1.0 / (1.0 + jnp.exp(-x))


def _silu(x):
    return x * _sigmoid(x)


def _log_sigmoid(x):
    return jnp.minimum(x, 0.0) - jnp.log1p(jnp.exp(-jnp.abs(x)))


def _dot(a, b):
    return jnp.dot(a, b, preferred_element_type=F32)


def _dot_nt(a, b):
    return lax.dot_general(a, b, (((1,), (1,)), ((), ())), preferred_element_type=F32)


def _dot_tn(a, b):
    return lax.dot_general(a, b, (((0,), (0,)), ((), ())), preferred_element_type=F32)


def _rms(x, g):
    ms = jnp.mean(x * x, axis=-1, keepdims=True)
    return x * lax.rsqrt(ms + EPS) * g


def _head_rms(x, g, n_heads):
    w = x.shape[-1] // n_heads
    parts = []
    for h in range(n_heads):
        xs = x[:, h * w:(h + 1) * w]
        ms = jnp.mean(xs * xs, axis=-1, keepdims=True)
        parts.append(xs * lax.rsqrt(ms + EPS))
    return jnp.concatenate(parts, axis=-1) * g


def _norm_matmul_kernel(x_ref, g_ref, w_ref, o_ref, h_ref):
    @pl.when(pl.program_id(1) == 0)
    def _():
        h_ref[...] = _rms(x_ref[...], g_ref[...]).astype(h_ref.dtype)

    o_ref[...] = _dot(h_ref[...], w_ref[...]).astype(o_ref.dtype)


def _norm_matmul(x, g, w, *, tm, tn, out_dtype, x_col_block=0):
    m = x.shape[0]
    k, n = w.shape
    tm = min(tm, m)
    return pl.pallas_call(
        _norm_matmul_kernel,
        grid=(m // tm, n // tn),
        in_specs=[pl.BlockSpec((tm, k), lambda i, j: (i, x_col_block)),
                  pl.BlockSpec((1, k), lambda i, j: (0, 0)),
                  pl.BlockSpec((k, tn), lambda i, j: (0, j))],
        out_specs=pl.BlockSpec((tm, tn), lambda i, j: (i, j)),
        out_shape=jax.ShapeDtypeStruct((m, n), out_dtype),
        scratch_shapes=[pltpu.VMEM((tm, k), MXU_DTYPE)],
        compiler_params=_cparams(("parallel", "arbitrary"), 56),
        name="norm_matmul",
    )(x, g.reshape(1, k), w)


def _matmul_res_kernel(a_ref, w_ref, x_ref, o_ref):
    o_ref[...] = x_ref[...] + _dot(a_ref[...], w_ref[...])


def _matmul_residual(a, w, x, *, tm):
    m, k = a.shape
    n = w.shape[1]
    tm = min(tm, m)
    return pl.pallas_call(
        _matmul_res_kernel,
        grid=(m // tm,),
        in_specs=[pl.BlockSpec((tm, k), lambda i: (i, 0)),
                  pl.BlockSpec((k, n), lambda i: (0, 0)),
                  pl.BlockSpec((tm, n), lambda i: (i, 0))],
        out_specs=pl.BlockSpec((tm, n), lambda i: (i, 0)),
        out_shape=jax.ShapeDtypeStruct((m, n), F32),
        compiler_params=_cparams(("parallel",), 40),
        name="matmul_residual",
    )(a, w, x)


def _conv_kernel(x_ref, xp_ref, xn_ref, w_ref, b_ref, o_ref, *, tiles_per_seq):
    i = pl.program_id(0)
    x = x_ref[...]
    ts = x.shape[0]
    t_in_seq = i % tiles_per_seq
    prev_row = jnp.where(t_in_seq == 0, 0.0, xp_ref[7:8, :])
    next_row = jnp.where(t_in_seq == tiles_per_seq - 1, 0.0, xn_ref[0:1, :])
    rows = lax.broadcasted_iota(jnp.int32, x.shape, 0)
    x_prev = jnp.where(rows == 0, prev_row, pltpu.roll(x, 1, axis=0))
    x_next = jnp.where(rows == ts - 1, next_row, pltpu.roll(x, ts - 1, axis=0))
    y = w_ref[0:1, :] * x_prev + w_ref[1:2, :] * x + w_ref[2:3, :] * x_next + b_ref[...]
    y = _silu(y)
    half = y.shape[1] // 2
    o_ref[:, :half] = (y[:, :half] * (A_DK ** -0.5)).astype(o_ref.dtype)
    o_ref[:, half:] = y[:, half:].astype(o_ref.dtype)


def _conv_silu(proj, conv_w, conv_b, *, seq):
    m = proj.shape[0]
    c = conv_w.shape[1]
    ts = min(TM_CONV, seq)
    nt = m // ts
    sub = ts // 8
    return pl.pallas_call(
        functools.partial(_conv_kernel, tiles_per_seq=seq // ts),
        grid=(nt,),
        in_specs=[pl.BlockSpec((ts, c), lambda i: (i, 0)),
                  pl.BlockSpec((8, c), lambda i: (jnp.maximum(i * sub - 1, 0), 0)),
                  pl.BlockSpec((8, c), lambda i: (jnp.minimum((i + 1) * sub, nt * sub - 1), 0)),
                  pl.BlockSpec((3, c), lambda i: (0, 0)),
                  pl.BlockSpec((1, c), lambda i: (0, 0))],
        out_specs=pl.BlockSpec((ts, c), lambda i: (i, 0)),
        out_shape=jax.ShapeDtypeStruct((m, c), MXU_DTYPE),
        compiler_params=_cparams(("parallel",), 32),
        name="conv_silu",
    )(proj, proj, proj, conv_w, conv_b.reshape(1, c))


def _mlstm_kernel(qk_f, v_f, g_f, qk_b, v_b, g_b, gbias_ref, h_f, h_b, c_scr, n_scr, m_scr):
    L = A_CHUNK

    @pl.when(pl.program_id(1) == 0)
    def _():
        c_scr[...] = jnp.zeros_like(c_scr)
        n_scr[...] = jnp.zeros_like(n_scr)
        m_scr[...] = jnp.zeros_like(m_scr)

    row = lax.broadcasted_iota(jnp.int32, (L, L), 0)
    col = lax.broadcasted_iota(jnp.int32, (L, L), 1)
    for d, (qk_ref, v_ref, g_ref, h_ref) in enumerate(((qk_f, v_f, g_f, h_f), (qk_b, v_b, g_b, h_b))):
        mask = (row >= col) if d == 0 else (row <= col)
        gates = g_ref[...] + gbias_ref[...]
        logf = _log_sigmoid(gates)
        csum = jnp.dot(mask.astype(F32), logf, precision=HIGHEST, preferred_element_type=F32)
        csum_t = csum.T
        gates_t = gates.T
        for h in range(A_HEADS):
            s = d * A_HEADS + h
            ci, cf = SM_I + s, SM_F + s
            bl_c, bl_r = csum[:, cf:cf + 1], csum_t[cf:cf + 1, :]
            i_c, i_r = gates[:, ci:ci + 1], gates_t[ci:ci + 1, :]
            g = bl_c[L - 1:L, :] if d == 0 else bl_c[0:1, :]
            m_prev = m_scr[s][0:1, 0:1]
            n_prev = n_scr[s][0:1, :]
            c_prev = c_scr[s]
            q = qk_ref[:, h * A_DK:(h + 1) * A_DK]
            k = qk_ref[:, (A_HEADS + h) * A_DK:(A_HEADS + h + 1) * A_DK]
            v = v_ref[:, h * A_DV:(h + 1) * A_DV].astype(MXU_DTYPE)
            kf = k.astype(F32)

            log_d = jnp.where(mask, bl_c - bl_r + i_r, -jnp.inf)
            log_inter = bl_c + m_prev
            m_t = jnp.maximum(log_inter, jnp.max(log_d, axis=1, keepdims=True))
            sc = _dot_nt(q, k) * jnp.exp(log_d - m_t)
            w_inter = jnp.exp(log_inter - m_t)
            num = _dot(sc.astype(MXU_DTYPE), v) + w_inter * _dot(q, c_prev.astype(MXU_DTYPE))
            qn = jnp.sum(q.astype(F32) * n_prev, axis=1, keepdims=True)
            den = jnp.sum(sc, axis=1, keepdims=True) + w_inter * qn
            h_ref[:, h * A_DV:(h + 1) * A_DV] = num / jnp.maximum(jnp.abs(den), jnp.exp(-m_t))

            log_w = g - bl_c + i_c
            m_new = jnp.maximum(g + m_prev, jnp.max(log_w, axis=0, keepdims=True))
            w = jnp.exp(log_w - m_new)
            decay = jnp.exp(g + m_prev - m_new)
            kw = w * kf
            c_scr[s] = decay * c_prev + _dot_tn(kw.astype(MXU_DTYPE), v)
            n_scr[s] = jnp.broadcast_to(decay * n_prev + jnp.sum(kw, axis=0, keepdims=True), n_scr.shape[1:])
            m_scr[s] = jnp.broadcast_to(m_new, m_scr.shape[1:])


def _mlstm(qk, proj, gbias, *, batch, seq):
    m = qk.shape[0]
    L = A_CHUNK
    nc = seq // L
    d_a = A_HEADS * A_DV
    v_blk = AB_AV // d_a
    sm_blk = AB_SMALL // LANES
    fwd = lambda b, c: b * nc + c
    bwd = lambda b, c: b * nc + (nc - 1 - c)
    out = jax.ShapeDtypeStruct((m, d_a), F32)
    return pl.pallas_call(
        _mlstm_kernel,
        grid=(batch, nc),
        in_specs=[pl.BlockSpec((L, 2 * A_HEADS * A_DK), lambda b, c: (fwd(b, c), 0)),
                  pl.BlockSpec((L, d_a), lambda b, c: (fwd(b, c), v_blk)),
                  pl.BlockSpec((L, LANES), lambda b, c: (fwd(b, c), sm_blk)),
                  pl.BlockSpec((L, 2 * A_HEADS * A_DK), lambda b, c: (bwd(b, c), 0)),
                  pl.BlockSpec((L, d_a), lambda b, c: (bwd(b, c), v_blk)),
                  pl.BlockSpec((L, LANES), lambda b, c: (bwd(b, c), sm_blk)),
                  pl.BlockSpec((1, LANES), lambda b, c: (0, 0))],
        out_specs=[pl.BlockSpec((L, d_a), lambda b, c: (fwd(b, c), 0)),
                   pl.BlockSpec((L, d_a), lambda b, c: (bwd(b, c), 0))],
        out_shape=[out, out],
        scratch_shapes=[pltpu.VMEM((2 * A_HEADS, A_DK, A_DV), F32),
                        pltpu.VMEM((2 * A_HEADS, 8, A_DK), F32),
                        pltpu.VMEM((2 * A_HEADS, 8, LANES), F32)],
        compiler_params=_cparams(("parallel", "arbitrary"), 32),
        name="mlstm",
    )(qk, proj, proj, qk, proj, proj, gbias)


def _gla_kernel(q_f, k_f, v_f, g_f, q_b, k_b, v_b, g_b, w2_ref, ab_ref, o_f, o_b, st_scr):
    L, SB = B_CHUNK, B_SUB
    nsb = L // SB
    hk = B_HEADS * B_DK

    @pl.when(pl.program_id(1) == 0)
    def _():
        st_scr[...] = jnp.zeros_like(st_scr)

    row = lax.broadcasted_iota(jnp.int32, (L, L), 0)
    col = lax.broadcasted_iota(jnp.int32, (L, L), 1)
    lane8 = lax.broadcasted_iota(jnp.int32, (8, L), 1)
    log2e = float(np.log2(np.e))
    dirs = ((q_f, k_f, v_f, g_f, o_f), (q_b, k_b, v_b, g_b, o_b))
    masks = (row >= col, row <= col)

    bc_all = []
    for d, (q_ref, k_ref, v_ref, g_ref, o_ref) in enumerate(dirs):
        pre = _dot(g_ref[...].astype(MXU_DTYPE), w2_ref[:, d * hk:(d + 1) * hk]) + ab_ref[:, d * hk:(d + 1) * hk]
        log_a = _log_sigmoid(pre) / B_GATE_TAU
        bc_all.append(jnp.dot(masks[d].astype(F32), log_a, precision=HIGHEST, preferred_element_type=F32))

    work = []
    for d, (q_ref, k_ref, v_ref, g_ref, o_ref) in enumerate(dirs):
        for h in range(B_HEADS):
            s = d * B_HEADS + h
            q = q_ref[:, h * B_DK:(h + 1) * B_DK] * (B_DK ** -0.5)
            k = k_ref[:, h * B_DK:(h + 1) * B_DK]
            b = bc_all[d][:, h * B_DK:(h + 1) * B_DK]
            st = st_scr[s]
            inter = _dot_nt((q * jnp.exp(b)).astype(MXU_DTYPE), st.astype(MXU_DTYPE))

            b2 = b * log2e
            b2_rows = [jnp.broadcast_to(b2[j:j + 1], (8, B_DK)) for j in range(L)]
            pieces, where_to = [], []
            for blk in range(nsb):
                lo = blk * SB
                for oc in range(SB // 8):
                    r0 = lo + 8 * oc
                    js = range(0, 8 * oc + 8) if d == 0 else range(8 * oc, SB)
                    for j in js:
                        pieces.append(q[r0:r0 + 8] * jnp.exp2(b2[r0:r0 + 8] - b2_rows[lo + j]))
                        where_to.append((r0, lo + j))
            diag = _dot_nt(jnp.concatenate(pieces, axis=0).astype(MXU_DTYPE), k.astype(MXU_DTYPE))

            offs = []
            for blk in range(nsb):
                lo, hi = blk * SB, (blk + 1) * SB
                if d == 0 and blk > 0:
                    ref = b[lo - 1:lo]
                    kt = k[:lo] * jnp.exp(ref - b[:lo])
                    kt = jnp.concatenate([kt, jnp.zeros((L - lo, B_DK), F32)], axis=0)
                elif d == 1 and blk < nsb - 1:
                    ref = b[hi:hi + 1]
                    kt = k[hi:] * jnp.exp(ref - b[hi:])
                    kt = jnp.concatenate([jnp.zeros((hi, B_DK), F32), kt], axis=0)
                else:
                    offs.append(jnp.zeros((SB, L), F32))
                    continue
                offs.append(_dot_nt((q[lo:hi] * jnp.exp(b[lo:hi] - ref)).astype(MXU_DTYPE), kt.astype(MXU_DTYPE)))
            work.append((d, h, s, k, b, st, inter, diag, where_to, offs))

    for d, h, s, k, b, st, inter, diag, where_to, offs in work:
        v = dirs[d][2][:, h * B_DV:(h + 1) * B_DV].astype(MXU_DTYPE)
        att_rows = []
        for blk in range(nsb):
            for oc in range(SB // 8):
                r0 = blk * SB + 8 * oc
                a8 = offs[blk][8 * oc:8 * oc + 8]
                for n, (rr, cc) in enumerate(where_to):
                    if rr == r0:
                        a8 = jnp.where(lane8 == cc, diag[8 * n:8 * n + 8], a8)
                att_rows.append(a8)
        att = jnp.where(masks[d], jnp.concatenate(att_rows, axis=0), 0.0)
        dirs[d][4][:, h * B_DV:(h + 1) * B_DV] = inter + _dot(att.astype(MXU_DTYPE), v)

    for d, h, s, k, b, st, inter, diag, where_to, offs in work:
        v = dirs[d][2][:, h * B_DV:(h + 1) * B_DV].astype(MXU_DTYPE)
        g = b[L - 1:L, :] if d == 0 else b[0:1, :]
        kd = (k * jnp.exp(g - b)).astype(MXU_DTYPE)
        st_scr[s] = jnp.exp(g) * st + _dot_tn(v, kd)


def _gla(proj, w2, ab, *, batch, seq):
    m = proj.shape[0]
    L = B_CHUNK
    nc = seq // L
    hk = B_HEADS * B_DK
    d_b = B_HEADS * B_DV
    q_blk, k_blk, v_blk, sm_blk = AB_BQ // hk, AB_BK // hk, AB_BV // d_b, AB_SMALL // LANES
    fwd = lambda b, c: b * nc + c
    bwd = lambda b, c: b * nc + (nc - 1 - c)

    def specs(idx):
        return [pl.BlockSpec((L, hk), lambda b, c: (idx(b, c), q_blk)),
                pl.BlockSpec((L, hk), lambda b, c: (idx(b, c), k_blk)),
                pl.BlockSpec((L, d_b), lambda b, c: (idx(b, c), v_blk)),
                pl.BlockSpec((L, LANES), lambda b, c: (idx(b, c), sm_blk))]

    out = jax.ShapeDtypeStruct((m, d_b), F32)
    return pl.pallas_call(
        _gla_kernel,
        grid=(batch, nc),
        in_specs=specs(fwd) + specs(bwd) + [pl.BlockSpec((LANES, 2 * hk), lambda b, c: (0, 0)),
                                            pl.BlockSpec((1, 2 * hk), lambda b, c: (0, 0))],
        out_specs=[pl.BlockSpec((L, d_b), lambda b, c: (fwd(b, c), 0)),
                   pl.BlockSpec((L, d_b), lambda b, c: (bwd(b, c), 0))],
        out_shape=[out, out],
        scratch_shapes=[pltpu.VMEM((2 * B_HEADS, B_DV, B_DK), F32)],
        compiler_params=_cparams(("parallel", "arbitrary"), 32),
        name="gla",
    )(proj, proj, proj, proj, proj, proj, proj, proj, w2, ab)


def _ab_post_kernel(hf_ref, hb_ref, ao_ref, az_ref, of_ref, ob_ref, bz_ref, ogb_ref, an_ref, bn_ref,
                    w_ref, x_ref, o_ref):
    d_a = A_HEADS * A_DV
    out_a = _sigmoid(ao_ref[...] + ogb_ref[...]) * (hf_ref[...] + hb_ref[...])
    out_a = _head_rms(out_a, an_ref[...], A_HEADS) * _silu(az_ref[...])
    out_b = _head_rms(of_ref[...] + ob_ref[...], bn_ref[...], B_HEADS) * _silu(bz_ref[...])
    y = _dot(out_a.astype(MXU_DTYPE), w_ref[:d_a, :]) + _dot(out_b.astype(MXU_DTYPE), w_ref[d_a:, :])
    o_ref[...] = x_ref[...] + y


def _ab_post(h_f, h_b, o_f, o_b, proj, ogate_b, a_norm, b_norm, w_out, x):
    m, d = x.shape
    tm = min(TM_POST, m)
    row = lambda i: (i, 0)
    vec = pl.BlockSpec((1, d), lambda i: (0, 0))
    blk = lambda cb: pl.BlockSpec((tm, d), lambda i: (i, cb))
    return pl.pallas_call(
        _ab_post_kernel,
        grid=(m // tm,),
        in_specs=[pl.BlockSpec((tm, d), row), pl.BlockSpec((tm, d), row), blk(AB_O // d), blk(AB_Z // d),
                  pl.BlockSpec((tm, d), row), pl.BlockSpec((tm, d), row), blk(AB_BZ // d),
                  vec, vec, vec, pl.BlockSpec(w_out.shape, lambda i: (0, 0)), pl.BlockSpec((tm, d), row)],
        out_specs=pl.BlockSpec((tm, d), row),
        out_shape=jax.ShapeDtypeStruct((m, d), F32),
        compiler_params=_cparams(("parallel",), 48),
        name="ab_post",
    )(h_f, h_b, proj, proj, o_f, o_b, proj, ogate_b.reshape(1, d), a_norm.reshape(1, d), b_norm.reshape(1, d),
      w_out, x)


def _cross_kernel(x_ref, g_ref, wq_ref, kv_ref, wo_ref, fin_ref, o_ref, *, final):
    x = x_ref[...]
    hn = _rms(x, g_ref[...]).astype(MXU_DTYPE)
    q = _dot(hn, wq_ref[...]).astype(MXU_DTYPE)
    d = X_HEADS * X_DH
    outs = []
    for h in range(X_HEADS):
        k = kv_ref[:, h * X_DH:(h + 1) * X_DH]
        v = kv_ref[:, d + h * X_DH:d + (h + 1) * X_DH]
        sc = _dot_nt(q[:, h * X_DH:(h + 1) * X_DH], k) * (X_DH ** -0.5)
        e = jnp.exp(sc - jnp.max(sc, axis=-1, keepdims=True))
        p = e / jnp.sum(e, axis=-1, keepdims=True)
        outs.append(_dot(p.astype(MXU_DTYPE), v))
    o = jnp.concatenate(outs, axis=-1).astype(MXU_DTYPE)
    y = x + _dot(o, wo_ref[...])
    if final:
        y = _rms(y, fin_ref[...])
    o_ref[...] = y


def _cross_attn(x, kv, g, w_q, w_o, fin, *, seq, final):
    m, d = x.shape
    tm = min(TM_CROSS, seq)
    tiles_per_seq = seq // tm
    full = lambda i: (0, 0)
    return pl.pallas_call(
        functools.partial(_cross_kernel, final=final),
        grid=(m // tm,),
        in_specs=[pl.BlockSpec((tm, d), lambda i: (i, 0)),
                  pl.BlockSpec((1, d), full),
                  pl.BlockSpec(w_q.shape, full),
                  pl.BlockSpec((N_MEM, kv.shape[1]), lambda i: (i // tiles_per_seq, 0)),
                  pl.BlockSpec(w_o.shape, full),
                  pl.BlockSpec((1, d), full)],
        out_specs=pl.BlockSpec((tm, d), lambda i: (i, 0)),
        out_shape=jax.ShapeDtypeStruct((m, d), F32),
        compiler_params=_cparams(("parallel",), 48),
        name="cross_attn",
    )(x, g.reshape(1, d), w_q, kv, w_o, fin.reshape(1, d))


C_QK = C_NOPE + LANES


C_EXP2 = (C_NOPE + C_ROPE) ** -0.5 * float(np.log2(np.e))
C_VT = C_V + 8


def _q_up_kernel(x_ref, g_ref, w_ref, cos_ref, sin_ref, o_ref):
    hn = _rms(x_ref[...], g_ref[...]).astype(MXU_DTYPE)
    pad = jnp.zeros((C_QK - C_NOPE - C_ROPE, hn.shape[0]), o_ref.dtype)
    for h in range(C_HEADS):
        rt = _dot_nt(w_ref[h], hn) * C_EXP2
        o_ref[0, h, :C_NOPE, :] = rt[:C_NOPE].astype(o_ref.dtype)
        rope = rt[C_NOPE:C_NOPE + C_ROPE] * cos_ref[...] + rt[C_NOPE + C_ROPE:] * sin_ref[...]
        o_ref[0, h, C_NOPE:C_NOPE + C_ROPE, :] = rope.astype(o_ref.dtype)
        o_ref[0, h, C_NOPE + C_ROPE:, :] = pad


def _q_up(cin, q_norm, w, cos_t, sin_t, *, batch, seq):
    tm = min(TM_UP, seq)
    tps = seq // tm
    return pl.pallas_call(
        _q_up_kernel,
        grid=(batch * tps,),
        in_specs=[pl.BlockSpec((tm, C_Q_LORA), lambda i: (i, C_QL // C_Q_LORA)),
                  pl.BlockSpec((1, C_Q_LORA), lambda i: (0, 0)),
                  pl.BlockSpec(w.shape, lambda i: (0, 0, 0)),
                  pl.BlockSpec((C_ROPE, tm), lambda i: (0, i % tps)),
                  pl.BlockSpec((C_ROPE, tm), lambda i: (0, i % tps))],
        out_specs=pl.BlockSpec((1, C_HEADS, C_QK, tm), lambda i: (i // tps, 0, 0, i % tps)),
        out_shape=jax.ShapeDtypeStruct((batch, C_HEADS, C_QK, seq), MXU_DTYPE),
        compiler_params=_cparams(("parallel",), 32),
        name="mla_q_up",
    )(cin, q_norm.reshape(1, C_Q_LORA), w, cos_t, sin_t)


def _kv_up_kernel(x_ref, g_ref, wk_ref, wvt_ref, kr_ref, ksw_ref, cos_ref, sin_ref, k_ref, vt_ref):
    hn = _rms(x_ref[...], g_ref[...]).astype(MXU_DTYPE)
    rope = (kr_ref[...] * cos_ref[...] + ksw_ref[...] * sin_ref[...]).astype(k_ref.dtype)
    tm = hn.shape[0]
    ones_rows = (lax.broadcasted_iota(jnp.int32, (C_VT - C_V, tm), 0) == 0).astype(vt_ref.dtype)
    for h in range(C_HEADS):
        k_ref[0, h, :, :C_NOPE] = _dot(hn, wk_ref[h]).astype(k_ref.dtype)
        k_ref[0, h, :, C_NOPE:] = rope
        vt_ref[0, h, :C_V, :] = _dot_nt(wvt_ref[h], hn).astype(vt_ref.dtype)
        vt_ref[0, h, C_V:, :] = ones_rows


def _kv_up(cin, kv_norm, wk, wvt, cos, sin, *, batch, seq):
    tm = min(TM_UP, seq)
    tps = seq // tm
    return pl.pallas_call(
        _kv_up_kernel,
        grid=(batch * tps,),
        in_specs=[pl.BlockSpec((tm, C_KV_LORA), lambda i: (i, C_KV // C_KV_LORA)),
                  pl.BlockSpec((1, C_KV_LORA), lambda i: (0, 0)),
                  pl.BlockSpec(wk.shape, lambda i: (0, 0, 0)),
                  pl.BlockSpec(wvt.shape, lambda i: (0, 0, 0)),
                  pl.BlockSpec((tm, LANES), lambda i: (i, C_KR // LANES)),
                  pl.BlockSpec((tm, LANES), lambda i: (i, C_KSW // LANES)),
                  pl.BlockSpec((tm, LANES), lambda i: (i % tps, 0)),
                  pl.BlockSpec((tm, LANES), lambda i: (i % tps, 0))],
        out_specs=[pl.BlockSpec((1, C_HEADS, tm, C_QK), lambda i: (i // tps, 0, i % tps, 0)),
                   pl.BlockSpec((1, C_HEADS, C_VT, tm), lambda i: (i // tps, 0, 0, i % tps))],
        out_shape=[jax.ShapeDtypeStruct((batch, C_HEADS, seq, C_QK), MXU_DTYPE),
                   jax.ShapeDtypeStruct((batch, C_HEADS, C_VT, seq), MXU_DTYPE)],
        compiler_params=_cparams(("parallel",), 32),
        name="mla_kv_up",
    )(cin, kv_norm.reshape(1, C_KV_LORA), wk, wvt, cin, cin, cos, sin)


def _flash_kernel(qt_ref, k_ref, vt_ref, z_ref, o_ref, m_scr, acc_scr):
    j = pl.program_id(2)
    bk, bq = k_ref.shape[2], qt_ref.shape[3]

    @pl.when(j == 0)
    def _():
        m_scr[...] = jnp.full_like(m_scr, -jnp.inf)
        acc_scr[...] = jnp.zeros_like(acc_scr)

    def scores(h):
        return _dot(k_ref[0, h], qt_ref[0, h]).reshape(bk // 8, 8, bq)

    st = scores(0)
    for h in range(C_HEADS):
        st_next = scores(h + 1) if h + 1 < C_HEADS else None
        m_prev = m_scr[h]
        m_new = jnp.maximum(m_prev, jnp.max(jnp.max(st, axis=0), axis=0, keepdims=True))
        p = jnp.exp2(st - m_new[None]).reshape(bk, bq).astype(MXU_DTYPE)
        alpha = jnp.exp2(m_prev - m_new)
        pv = _dot(vt_ref[0, h], p)
        acc = acc_scr[h].reshape(C_VT // 8, 8, bq) * alpha[None]
        acc_scr[h] = acc.reshape(C_VT, bq) + pv
        m_scr[h] = m_new
        st = st_next

    @pl.when(j == pl.num_programs(2) - 1)
    def _():
        for h in range(C_HEADS):
            acc = acc_scr[h]
            o = (acc[:C_V] / acc[C_V:C_V + 1]).T
            zh = z_ref[:, h * C_V:(h + 1) * C_V]
            o_ref[:, h * C_V:(h + 1) * C_V] = (o * _silu(zh)).astype(o_ref.dtype)


def _flash(qt, k, vt, cin, *, batch, seq):
    bq, bk = min(BQ, seq), min(BK, seq)
    nq, nk = seq // bq, seq // bk
    d = C_HEADS * C_V
    return pl.pallas_call(
        _flash_kernel,
        grid=(batch, nq, nk),
        in_specs=[pl.BlockSpec((1, C_HEADS, C_QK, bq), lambda b, i, j: (b, 0, 0, i)),
                  pl.BlockSpec((1, C_HEADS, bk, C_QK), lambda b, i, j: (b, 0, j, 0)),
                  pl.BlockSpec((1, C_HEADS, C_VT, bk), lambda b, i, j: (b, 0, 0, j)),
                  pl.BlockSpec((bq, d), lambda b, i, j: (b * nq + i, C_Z // d))],
        out_specs=pl.BlockSpec((bq, d), lambda b, i, j: (b * nq + i, 0)),
        out_shape=jax.ShapeDtypeStruct((batch * seq, d), MXU_DTYPE),
        scratch_shapes=[pltpu.VMEM((C_HEADS, 8, bq), F32), pltpu.VMEM((C_HEADS, C_VT, bq), F32)],
        compiler_params=_cparams(("parallel", "parallel", "arbitrary"), 48),
        name="mla_flash",
    )(qt, k, vt, cin)


def _split_cols(w, sizes):
    idx = np.cumsum(np.array(sizes))[:-1].tolist()
    return jnp.split(w, idx, axis=-1)


def _rope_swap(w):
    half = w.shape[-1] // 2
    return jnp.concatenate([-w[..., half:], w[..., :half]], axis=-1)


def _pad_cols(w, n):
    return jnp.pad(w, [(0, 0)] * (w.ndim - 1) + [(0, n - w.shape[-1])])


def _prep_ab(w_in, alpha_w2, alpha_b, igate_b, fgate_b):
    sizes = (2 * A_HEADS * A_DK, A_HEADS * A_DV, 2 * A_HEADS, 2 * A_HEADS, A_HEADS * A_DV, A_HEADS * A_DV,
             B_HEADS * B_DK, B_HEADS * B_DK, B_HEADS * B_DV, 2 * B_GATE_RANK, B_HEADS * B_DV)
    a_qk, a_v, a_i, a_f, a_o, a_z, b_q, b_k, b_v, b_low, b_z = _split_cols(w_in, sizes)
    small = _pad_cols(jnp.concatenate([a_i, a_f, b_low], axis=-1), LANES)
    w = jnp.concatenate([a_qk, a_o, a_z, b_q, b_k, b_z, a_v, b_v, small], axis=-1).astype(MXU_DTYPE)
    hk = B_HEADS * B_DK
    w2 = jnp.zeros((LANES, 2 * hk), F32)
    w2 = w2.at[SM_LOW:SM_LOW + B_GATE_RANK, :hk].set(alpha_w2[0])
    w2 = w2.at[SM_LOW + B_GATE_RANK:SM_LOW + 2 * B_GATE_RANK, hk:].set(alpha_w2[1])
    ab = alpha_b.reshape(1, 2 * hk)
    gbias = _pad_cols(jnp.concatenate([igate_b.reshape(1, -1), fgate_b.reshape(1, -1)], axis=-1), LANES)
    return w, w2.astype(MXU_DTYPE), ab, gbias


def _prep_c(w_in, w_q_up, w_kv_up):
    q_lat, kv_lat, k_rope, z = _split_cols(w_in, (C_Q_LORA, C_KV_LORA, C_ROPE, C_HEADS * C_V))
    w_c = jnp.concatenate([z, kv_lat, _pad_cols(k_rope, LANES), _pad_cols(_rope_swap(k_rope), LANES), q_lat],
                          axis=-1).astype(MXU_DTYPE)
    wq = w_q_up.reshape(C_Q_LORA, C_HEADS, C_NOPE + C_ROPE)
    nope, rope = wq[..., :C_NOPE], wq[..., C_NOPE:]
    wq_t = jnp.transpose(jnp.concatenate([nope, rope, _rope_swap(rope)], axis=-1), (1, 2, 0))
    wkv = w_kv_up.reshape(C_KV_LORA, C_HEADS, C_NOPE + C_V)
    wk = jnp.transpose(wkv[..., :C_NOPE], (1, 0, 2))
    wv_t = jnp.transpose(wkv[..., C_NOPE:], (1, 2, 0))
    return w_c, wq_t.astype(MXU_DTYPE), wk.astype(MXU_DTYPE), wv_t.astype(MXU_DTYPE)


def _rope_tables(seq):
    inv = ROPE_BASE ** (-jnp.arange(0, C_ROPE, 2, dtype=F32) / C_ROPE)
    ang = jnp.arange(seq, dtype=F32)[:, None] * inv[None, :]
    cos, sin = jnp.cos(ang), jnp.sin(ang)
    cos2, sin2 = jnp.concatenate([cos, cos], axis=-1), jnp.concatenate([sin, sin], axis=-1)
    return _pad_cols(cos2, LANES), _pad_cols(sin2, LANES), cos2.T, sin2.T


def _trunk(x, mem, p, prep):
    batch, seq, d = x.shape
    x = x.reshape(batch * seq, d)
    mem = mem.reshape(batch * N_MEM, d)
    cos, sin, cos_t, sin_t = _rope_tables(seq)
    for l in range(DEPTH):
        j = l // 2
        if l % 2 == 0:
            w_in, w2, ab, gbias = prep['ab'][j]
            proj = _norm_matmul(x, p['norm_mix'][l], w_in, tm=TM_PROJ, tn=AB_TN, out_dtype=F32)
            qk = _conv_silu(proj, p['ab_conv_w'][j], p['ab_conv_b'][j], seq=seq)
            h_f, h_b = _mlstm(qk, proj, gbias, batch=batch, seq=seq)
            o_f, o_b = _gla(proj, w2, ab, batch=batch, seq=seq)
            x = _ab_post(h_f, h_b, o_f, o_b, proj, p['a_ogate_b'][j], p['a_head_norm'][j], p['b_head_norm'][j],
                         prep['ab_w_out'][j], x)
        else:
            w_c, wq_t, wk, wv_t = prep['c'][j]
            cin = _norm_matmul(x, p['norm_mix'][l], w_c, tm=TM_PROJ, tn=C_TN, out_dtype=F32)
            qt = _q_up(cin, p['c_q_norm'][j], wq_t, cos_t, sin_t, batch=batch, seq=seq)
            k, vt = _kv_up(cin, p['c_kv_norm'][j], wk, wv_t, cos, sin, batch=batch, seq=seq)
            og = _flash(qt, k, vt, cin, batch=batch, seq=seq)
            x = _matmul_residual(og, prep['c_w_out'][j], x, tm=TM_OUT)
        kv = _norm_matmul(mem, p['norm_mem'][l], prep['x_w_kv'][l], tm=N_MEM, tn=1024, out_dtype=MXU_DTYPE)
        x = _cross_attn(x, kv, p['norm_cross'][l], prep['x_w_q'][l], prep['x_w_o'][l], p['final_norm'],
                        seq=seq, final=(l == DEPTH - 1))
    return x.reshape(batch, seq, d)


def kernel(x_prompt, x_sample, mem_prompt, mem_sample, norm_mix, norm_cross, norm_mem, ab_w_in, ab_conv_w, ab_conv_b, a_igate_b, a_fgate_b, a_ogate_b, a_head_norm, b_alpha_w2, b_alpha_b, b_head_norm, ab_w_out, c_w_in, c_q_norm, c_kv_norm, c_w_q_up, c_w_kv_up, c_w_out, x_w_q, x_w_kv, x_w_o, final_norm):
    p = dict(norm_mix=norm_mix, norm_cross=norm_cross, norm_mem=norm_mem, ab_conv_w=ab_conv_w,
             ab_conv_b=ab_conv_b, a_ogate_b=a_ogate_b, a_head_norm=a_head_norm, b_head_norm=b_head_norm,
             c_q_norm=c_q_norm, c_kv_norm=c_kv_norm, final_norm=final_norm)
    n_even, n_odd = ab_w_in.shape[0], c_w_in.shape[0]
    prep = dict(
        ab=[_prep_ab(ab_w_in[j], b_alpha_w2[j], b_alpha_b[j], a_igate_b[j], a_fgate_b[j]) for j in range(n_even)],
        ab_w_out=[ab_w_out[j].astype(MXU_DTYPE) for j in range(n_even)],
        c=[_prep_c(c_w_in[j], c_w_q_up[j], c_w_kv_up[j]) for j in range(n_odd)],
        c_w_out=[c_w_out[j].astype(MXU_DTYPE) for j in range(n_odd)],
        x_w_q=[x_w_q[l].astype(MXU_DTYPE) for l in range(DEPTH)],
        x_w_kv=[x_w_kv[l].astype(MXU_DTYPE) for l in range(DEPTH)],
        x_w_o=[x_w_o[l].astype(MXU_DTYPE) for l in range(DEPTH)],
    )
    return (_trunk(x_prompt, mem_prompt, p, prep), _trunk(x_sample, mem_sample, p, prep))
```

```python
import functools

import numpy as np
import jax
import jax.numpy as jnp
from jax import lax
from jax.experimental import pallas as pl
from jax.experimental.pallas import tpu as pltpu

F32 = jnp.float32
MXU_DTYPE = jnp.bfloat16
EPS = 1e-6
HIGHEST = lax.Precision.HIGHEST

D_MODEL = 1024
DEPTH = 4
A_HEADS, A_DK, A_DV, A_CHUNK = 4, 128, 256, 128
B_HEADS, B_DK, B_DV, B_CHUNK = 4, 128, 256, 64
B_GATE_RANK, B_GATE_TAU = 16, 16.0
B_SUB = 16
C_HEADS, C_Q_LORA, C_KV_LORA, C_NOPE, C_ROPE, C_V = 8, 384, 256, 128, 64, 128
ROPE_BASE = 10000.0
X_HEADS, X_DH, N_MEM = 4, 256, 256

LANES = 128
V7X_VMEM_BYTES = 64 * 1024 * 1024

AB_QK, AB_O, AB_Z, AB_BQ, AB_BK, AB_BZ, AB_AV, AB_BV, AB_SMALL = (
    0, 1024, 2048, 3072, 3584, 4096, 5120, 6144, 7168)
AB_COLS = 7296
AB_TN = 2432
SM_I, SM_F, SM_LOW = 0, 8, 16
C_Z, C_KV, C_KR, C_KSW, C_QL = 0, 1024, 1280, 1408, 1536
C_COLS = 1920
C_TN = 640

TM_PROJ = 1024
TM_CONV = 512
TM_POST = 256
TM_CROSS = 512
TM_UP = 512
TM_OUT = 512
BQ, BK, QSUB = 1024, 512, 512


def _cparams(sem, vmem_mb=48):
    assert vmem_mb * 1024 * 1024 <= V7X_VMEM_BYTES
    return pltpu.CompilerParams(dimension_semantics=sem, vmem_limit_bytes=vmem_mb * 1024 * 1024)


def _sigmoid(x):
    return 1.0 / (1.0 + jnp.exp(-x))


def _silu(x):
    return x * _sigmoid(x)


def _log_sigmoid(x):
    return jnp.minimum(x, 0.0) - jnp.log1p(jnp.exp(-jnp.abs(x)))


def _dot(a, b):
    return jnp.dot(a, b, preferred_element_type=F32)


def _dot_nt(a, b):
    return lax.dot_general(a, b, (((1,), (1,)), ((), ())), preferred_element_type=F32)


def _dot_tn(a, b):
    return lax.dot_general(a, b, (((0,), (0,)), ((), ())), preferred_element_type=F32)


def _rms(x, g):
    ms = jnp.mean(x * x, axis=-1, keepdims=True)
    return x * lax.rsqrt(ms + EPS) * g


def _head_rms(x, g, n_heads):
    w = x.shape[-1] // n_heads
    parts = []
    for h in range(n_heads):
        xs = x[:, h * w:(h + 1) * w]
        ms = jnp.mean(xs * xs, axis=-1, keepdims=True)
        parts.append(xs * lax.rsqrt(ms + EPS))
    return jnp.concatenate(parts, axis=-1) * g


def _norm_matmul_kernel(x_ref, g_ref, w_ref, o_ref, h_ref):
    @pl.when(pl.program_id(1) == 0)
    def _():
        h_ref[...] = _rms(x_ref[...], g_ref[...]).astype(h_ref.dtype)

    o_ref[...] = _dot(h_ref[...], w_ref[...]).astype(o_ref.dtype)


def _norm_matmul(x, g, w, *, tm, tn, out_dtype, x_col_block=0):
    m = x.shape[0]
    k, n = w.shape
    tm = min(tm, m)
    return pl.pallas_call(
        _norm_matmul_kernel,
        grid=(m // tm, n // tn),
        in_specs=[pl.BlockSpec((tm, k), lambda i, j: (i, x_col_block)),
                  pl.BlockSpec((1, k), lambda i, j: (0, 0)),
                  pl.BlockSpec((k, tn), lambda i, j: (0, j))],
        out_specs=pl.BlockSpec((tm, tn), lambda i, j: (i, j)),
        out_shape=jax.ShapeDtypeStruct((m, n), out_dtype),
        scratch_shapes=[pltpu.VMEM((tm, k), MXU_DTYPE)],
        compiler_params=_cparams(("parallel", "arbitrary"), 56),
        name="norm_matmul",
    )(x, g.reshape(1, k), w)


def _matmul_res_kernel(a_ref, w_ref, x_ref, o_ref):
    o_ref[...] = x_ref[...] + _dot(a_ref[...], w_ref[...])


def _matmul_residual(a, w, x, *, tm):
    m, k = a.shape
    n = w.shape[1]
    tm = min(tm, m)
    return pl.pallas_call(
        _matmul_res_kernel,
        grid=(m // tm,),
        in_specs=[pl.BlockSpec((tm, k), lambda i: (i, 0)),
                  pl.BlockSpec((k, n), lambda i: (0, 0)),
                  pl.BlockSpec((tm, n), lambda i: (i, 0))],
        out_specs=pl.BlockSpec((tm, n), lambda i: (i, 0)),
        out_shape=jax.ShapeDtypeStruct((m, n), F32),
        compiler_params=_cparams(("parallel",), 40),
        name="matmul_residual",
    )(a, w, x)


def _conv_kernel(x_ref, xp_ref, xn_ref, w_ref, b_ref, o_ref, kt_ref, *, tiles_per_seq):
    i = pl.program_id(0)
    x = x_ref[...]
    ts = x.shape[0]
    t_in_seq = i % tiles_per_seq
    prev_row = jnp.where(t_in_seq == 0, 0.0, xp_ref[7:8, :])
    next_row = jnp.where(t_in_seq == tiles_per_seq - 1, 0.0, xn_ref[0:1, :])
    rows = lax.broadcasted_iota(jnp.int32, x.shape, 0)
    x_prev = jnp.where(rows == 0, prev_row, pltpu.roll(x, 1, axis=0))
    x_next = jnp.where(rows == ts - 1, next_row, pltpu.roll(x, ts - 1, axis=0))
    y = w_ref[0:1, :] * x_prev + w_ref[1:2, :] * x + w_ref[2:3, :] * x_next + b_ref[...]
    y = _silu(y)
    half = y.shape[1] // 2
    o_ref[:, :half] = (y[:, :half] * (A_DK ** -0.5)).astype(o_ref.dtype)
    o_ref[:, half:] = y[:, half:].astype(o_ref.dtype)
    kt_ref[...] = y[:, half:].T.astype(kt_ref.dtype)


def _conv_silu(proj, conv_w, conv_b, *, seq):
    m = proj.shape[0]
    c = conv_w.shape[1]
    ts = min(TM_CONV, seq)
    nt = m // ts
    sub = ts // 8
    return pl.pallas_call(
        functools.partial(_conv_kernel, tiles_per_seq=seq // ts),
        grid=(nt,),
        in_specs=[pl.BlockSpec((ts, c), lambda i: (i, 0)),
                  pl.BlockSpec((8, c), lambda i: (jnp.maximum(i * sub - 1, 0), 0)),
                  pl.BlockSpec((8, c), lambda i: (jnp.minimum((i + 1) * sub, nt * sub - 1), 0)),
                  pl.BlockSpec((3, c), lambda i: (0, 0)),
                  pl.BlockSpec((1, c), lambda i: (0, 0))],
        out_specs=[pl.BlockSpec((ts, c), lambda i: (i, 0)),
                   pl.BlockSpec((c // 2, ts), lambda i: (0, i))],
        out_shape=[jax.ShapeDtypeStruct((m, c), MXU_DTYPE),
                   jax.ShapeDtypeStruct((c // 2, m), MXU_DTYPE)],
        compiler_params=_cparams(("parallel",), 32),
        name="conv_silu",
    )(proj, proj, proj, conv_w, conv_b.reshape(1, c))


def _mlstm_kernel(qk_f, kt_f, v_f, g_f, qk_b, kt_b, v_b, g_b, gbias_ref, h_f, h_b, c_scr, n_scr, m_scr):
    L = A_CHUNK

    @pl.when(pl.program_id(1) == 0)
    def _():
        c_scr[...] = jnp.zeros_like(c_scr)
        n_scr[...] = jnp.zeros_like(n_scr)
        m_scr[...] = jnp.zeros_like(m_scr)

    row = lax.broadcasted_iota(jnp.int32, (L, L), 0)
    col = lax.broadcasted_iota(jnp.int32, (L, L), 1)
    sub8 = lax.broadcasted_iota(jnp.int32, (8, L), 0)
    lane_row = lax.broadcasted_iota(jnp.int32, (1, LANES), 1)
    neg_inf = float("-inf")
    m_all = m_scr[0:1, :]
    m_next = m_all
    dirs = ((qk_f, kt_f, v_f, g_f, h_f), (qk_b, kt_b, v_b, g_b, h_b))
    pro = []
    for d, (qk_ref, kt_ref, v_ref, g_ref, h_ref) in enumerate(dirs):
        mask = (row >= col) if d == 0 else (row <= col)
        gates = g_ref[...] + gbias_ref[...]
        csum = jnp.dot(mask.astype(F32), _log_sigmoid(gates), precision=HIGHEST, preferred_element_type=F32)
        a_c = pltpu.roll(csum, LANES - (SM_F - SM_I), axis=1)
        e_c = gates - a_c
        run = e_c
        sh = 1
        while sh < L:
            pad = jnp.full((sh, LANES), neg_inf, F32)
            moved = (jnp.concatenate([pad, run[:L - sh]], axis=0) if d == 0
                     else jnp.concatenate([run[sh:], pad], axis=0))
            run = jnp.maximum(run, moved)
            sh *= 2
        m_c = jnp.maximum(run, m_all)
        w_inter_c = jnp.exp(m_all - m_c)
        inv_floor_c = jnp.exp(-(a_c + m_c))
        pro.append((mask, a_c, m_c, w_inter_c, inv_floor_c, e_c.T, m_c.T))

    for d, (qk_ref, kt_ref, v_ref, g_ref, h_ref) in enumerate(dirs):
        mask, a_c, m_c, w_inter_c, inv_floor_c, e_t, m_t_rows = pro[d]
        end = L - 1 if d == 0 else 0
        for h in range(A_HEADS):
            s = d * A_HEADS + h
            n_prev = n_scr[s]
            c_prev = c_scr[s]
            q = qk_ref[:, h * A_DK:(h + 1) * A_DK]
            k = qk_ref[:, (A_HEADS + h) * A_DK:(A_HEADS + h + 1) * A_DK]
            kt = kt_ref[h * A_DK:(h + 1) * A_DK, :]
            v = v_ref[:, h * A_DV:(h + 1) * A_DV].astype(MXU_DTYPE)
            e_r, m_r = e_t[s:s + 1, :], m_t_rows[s:s + 1, :]

            lhs = jnp.where(sub8 == 0, -m_r, jnp.where(sub8 == 1, 1.0, 0.0))
            rhs = jnp.where(sub8 == 0, 1.0, jnp.where(sub8 == 1, e_r, 0.0))
            x = lax.dot_general(lhs, rhs, (((0,), (0,)), ((), ())), precision=HIGHEST,
                                preferred_element_type=F32)
            sc = _dot_nt(q, k) * jnp.exp(jnp.where(mask, x, neg_inf))
            w_inter = w_inter_c[:, s:s + 1]
            num = _dot(sc.astype(MXU_DTYPE), v) + w_inter * _dot(q, c_prev.astype(MXU_DTYPE))
            qn = _dot_nt(q, n_prev.astype(MXU_DTYPE))[:, 0:1]
            den = jnp.sum(sc, axis=1, keepdims=True) + w_inter * qn
            h_ref[:, h * A_DV:(h + 1) * A_DV] = num / jnp.maximum(jnp.abs(den), inv_floor_c[:, s:s + 1])

            g = a_c[end:end + 1, s:s + 1]
            m_end = m_c[end:end + 1, s:s + 1]
            w = jnp.exp(e_r - m_end)
            decay = w_inter_c[end:end + 1, s:s + 1]
            kwt = (kt.astype(F32) * w).astype(MXU_DTYPE)
            c_scr[s] = decay * c_prev + _dot(kwt, v)
            n_scr[s] = decay * n_prev + _dot(jnp.broadcast_to(w, (8, L)).astype(MXU_DTYPE), k)
            m_next = jnp.where(lane_row == s, g + m_end, m_next)
    m_scr[...] = jnp.broadcast_to(m_next, m_scr.shape)


def _mlstm(qk, kt, proj, gbias, *, batch, seq):
    m = qk.shape[0]
    L = A_CHUNK
    nc = seq // L
    d_a = A_HEADS * A_DV
    v_blk = AB_AV // d_a
    sm_blk = AB_SMALL // LANES
    fwd = lambda b, c: b * nc + c
    bwd = lambda b, c: b * nc + (nc - 1 - c)
    out = jax.ShapeDtypeStruct((m, d_a), F32)
    return pl.pallas_call(
        _mlstm_kernel,
        grid=(batch, nc),
        in_specs=[pl.BlockSpec((L, 2 * A_HEADS * A_DK), lambda b, c: (fwd(b, c), 0)),
                  pl.BlockSpec((A_HEADS * A_DK, L), lambda b, c: (0, fwd(b, c))),
                  pl.BlockSpec((L, d_a), lambda b, c: (fwd(b, c), v_blk)),
                  pl.BlockSpec((L, LANES), lambda b, c: (fwd(b, c), sm_blk)),
                  pl.BlockSpec((L, 2 * A_HEADS * A_DK), lambda b, c: (bwd(b, c), 0)),
                  pl.BlockSpec((A_HEADS * A_DK, L), lambda b, c: (0, bwd(b, c))),
                  pl.BlockSpec((L, d_a), lambda b, c: (bwd(b, c), v_blk)),
                  pl.BlockSpec((L, LANES), lambda b, c: (bwd(b, c), sm_blk)),
                  pl.BlockSpec((1, LANES), lambda b, c: (0, 0))],
        out_specs=[pl.BlockSpec((L, d_a), lambda b, c: (fwd(b, c), 0)),
                   pl.BlockSpec((L, d_a), lambda b, c: (bwd(b, c), 0))],
        out_shape=[out, out],
        scratch_shapes=[pltpu.VMEM((2 * A_HEADS, A_DK, A_DV), F32),
                        pltpu.VMEM((2 * A_HEADS, 8, A_DK), F32),
                        pltpu.VMEM((8, LANES), F32)],
        compiler_params=_cparams(("parallel", "arbitrary"), 32),
        name="mlstm",
    )(qk, kt, proj, proj, qk, kt, proj, proj, gbias)


def _gla_kernel(q_f, k_f, v_f, g_f, q_b, k_b, v_b, g_b, w2_ref, ab_ref, o_f, o_b, st_scr):
    L, SB = B_CHUNK, B_SUB
    nsb = L // SB
    hk = B_HEADS * B_DK

    @pl.when(pl.program_id(1) == 0)
    def _():
        st_scr[...] = jnp.zeros_like(st_scr)

    row = lax.broadcasted_iota(jnp.int32, (L, L), 0)
    col = lax.broadcasted_iota(jnp.int32, (L, L), 1)
    lane8 = lax.broadcasted_iota(jnp.int32, (8, L), 1)
    log2e = float(np.log2(np.e))
    dirs = ((q_f, k_f, v_f, g_f, o_f), (q_b, k_b, v_b, g_b, o_b))
    masks = (row >= col, row <= col)

    bc_all = []
    for d, (q_ref, k_ref, v_ref, g_ref, o_ref) in enumerate(dirs):
        pre = _dot(g_ref[...].astype(MXU_DTYPE), w2_ref[:, d * hk:(d + 1) * hk]) + ab_ref[:, d * hk:(d + 1) * hk]
        log_a = _log_sigmoid(pre) / B_GATE_TAU
        bc_all.append(jnp.dot(masks[d].astype(F32), log_a, precision=HIGHEST, preferred_element_type=F32))

    work = []
    for d, (q_ref, k_ref, v_ref, g_ref, o_ref) in enumerate(dirs):
        for h in range(B_HEADS):
            s = d * B_HEADS + h
            q = q_ref[:, h * B_DK:(h + 1) * B_DK] * (B_DK ** -0.5)
            k = k_ref[:, h * B_DK:(h + 1) * B_DK]
            b = bc_all[d][:, h * B_DK:(h + 1) * B_DK]
            st = st_scr[s]
            inter = _dot_nt((q * jnp.exp(b)).astype(MXU_DTYPE), st.astype(MXU_DTYPE))

            b2 = b * log2e
            b2_rows = [jnp.broadcast_to(b2[j:j + 1], (8, B_DK)) for j in range(L)]
            pieces, where_to = [], []
            for blk in range(nsb):
                lo = blk * SB
                for oc in range(SB // 8):
                    r0 = lo + 8 * oc
                    js = range(0, 8 * oc + 8) if d == 0 else range(8 * oc, SB)
                    for j in js:
                        pieces.append(q[r0:r0 + 8] * jnp.exp2(b2[r0:r0 + 8] - b2_rows[lo + j]))
                        where_to.append((r0, lo + j))
            diag = _dot_nt(jnp.concatenate(pieces, axis=0).astype(MXU_DTYPE), k.astype(MXU_DTYPE))

            offs = []
            for blk in range(nsb):
                lo, hi = blk * SB, (blk + 1) * SB
                if d == 0 and blk > 0:
                    ref = b[lo - 1:lo]
                    kt = k[:lo] * jnp.exp(ref - b[:lo])
                    kt = jnp.concatenate([kt, jnp.zeros((L - lo, B_DK), F32)], axis=0)
                elif d == 1 and blk < nsb - 1:
                    ref = b[hi:hi + 1]
                    kt = k[hi:] * jnp.exp(ref - b[hi:])
                    kt = jnp.concatenate([jnp.zeros((hi, B_DK), F32), kt], axis=0)
                else:
                    offs.append(jnp.zeros((SB, L), F32))
                    continue
                offs.append(_dot_nt((q[lo:hi] * jnp.exp(b[lo:hi] - ref)).astype(MXU_DTYPE), kt.astype(MXU_DTYPE)))
            work.append((d, h, s, k, b, st, inter, diag, where_to, offs))

    for d, h, s, k, b, st, inter, diag, where_to, offs in work:
        v = dirs[d][2][:, h * B_DV:(h + 1) * B_DV].astype(MXU_DTYPE)
        att_rows = []
        for blk in range(nsb):
            for oc in range(SB // 8):
                r0 = blk * SB + 8 * oc
                a8 = offs[blk][8 * oc:8 * oc + 8]
                for n, (rr, cc) in enumerate(where_to):
                    if rr == r0:
                        a8 = jnp.where(lane8 == cc, diag[8 * n:8 * n + 8], a8)
                att_rows.append(a8)
        att = jnp.where(masks[d], jnp.concatenate(att_rows, axis=0), 0.0)
        dirs[d][4][:, h * B_DV:(h + 1) * B_DV] = inter + _dot(att.astype(MXU_DTYPE), v)

    for d, h, s, k, b, st, inter, diag, where_to, offs in work:
        v = dirs[d][2][:, h * B_DV:(h + 1) * B_DV].astype(MXU_DTYPE)
        g = b[L - 1:L, :] if d == 0 else b[0:1, :]
        kd = (k * jnp.exp(g - b)).astype(MXU_DTYPE)
        st_scr[s] = jnp.exp(g) * st + _dot_tn(v, kd)


def _gla(proj, w2, ab, *, batch, seq):
    m = proj.shape[0]
    L = B_CHUNK
    nc = seq // L
    hk = B_HEADS * B_DK
    d_b = B_HEADS * B_DV
    q_blk, k_blk, v_blk, sm_blk = AB_BQ // hk, AB_BK // hk, AB_BV // d_b, AB_SMALL // LANES
    fwd = lambda b, c: b * nc + c
    bwd = lambda b, c: b * nc + (nc - 1 - c)

    def specs(idx):
        return [pl.BlockSpec((L, hk), lambda b, c: (idx(b, c), q_blk)),
                pl.BlockSpec((L, hk), lambda b, c: (idx(b, c), k_blk)),
                pl.BlockSpec((L, d_b), lambda b, c: (idx(b, c), v_blk)),
                pl.BlockSpec((L, LANES), lambda b, c: (idx(b, c), sm_blk))]

    out = jax.ShapeDtypeStruct((m, d_b), F32)
    return pl.pallas_call(
        _gla_kernel,
        grid=(batch, nc),
        in_specs=specs(fwd) + specs(bwd) + [pl.BlockSpec((LANES, 2 * hk), lambda b, c: (0, 0)),
                                            pl.BlockSpec((1, 2 * hk), lambda b, c: (0, 0))],
        out_specs=[pl.BlockSpec((L, d_b), lambda b, c: (fwd(b, c), 0)),
                   pl.BlockSpec((L, d_b), lambda b, c: (bwd(b, c), 0))],
        out_shape=[out, out],
        scratch_shapes=[pltpu.VMEM((2 * B_HEADS, B_DV, B_DK), F32)],
        compiler_params=_cparams(("parallel", "arbitrary"), 32),
        name="gla",
    )(proj, proj, proj, proj, proj, proj, proj, proj, w2, ab)


def _ab_post_kernel(hf_ref, hb_ref, ao_ref, az_ref, of_ref, ob_ref, bz_ref, ogb_ref, an_ref, bn_ref,
                    w_ref, x_ref, o_ref):
    d_a = A_HEADS * A_DV
    out_a = _sigmoid(ao_ref[...] + ogb_ref[...]) * (hf_ref[...] + hb_ref[...])
    out_a = _head_rms(out_a, an_ref[...], A_HEADS) * _silu(az_ref[...])
    out_b = _head_rms(of_ref[...] + ob_ref[...], bn_ref[...], B_HEADS) * _silu(bz_ref[...])
    y = _dot(out_a.astype(MXU_DTYPE), w_ref[:d_a, :]) + _dot(out_b.astype(MXU_DTYPE), w_ref[d_a:, :])
    o_ref[...] = x_ref[...] + y


def _ab_post(h_f, h_b, o_f, o_b, proj, ogate_b, a_norm, b_norm, w_out, x):
    m, d = x.shape
    tm = min(TM_POST, m)
    row = lambda i: (i, 0)
    vec = pl.BlockSpec((1, d), lambda i: (0, 0))
    blk = lambda cb: pl.BlockSpec((tm, d), lambda i: (i, cb))
    return pl.pallas_call(
        _ab_post_kernel,
        grid=(m // tm,),
        in_specs=[pl.BlockSpec((tm, d), row), pl.BlockSpec((tm, d), row), blk(AB_O // d), blk(AB_Z // d),
                  pl.BlockSpec((tm, d), row), pl.BlockSpec((tm, d), row), blk(AB_BZ // d),
                  vec, vec, vec, pl.BlockSpec(w_out.shape, lambda i: (0, 0)), pl.BlockSpec((tm, d), row)],
        out_specs=pl.BlockSpec((tm, d), row),
        out_shape=jax.ShapeDtypeStruct((m, d), F32),
        compiler_params=_cparams(("parallel",), 48),
        name="ab_post",
    )(h_f, h_b, proj, proj, o_f, o_b, proj, ogate_b.reshape(1, d), a_norm.reshape(1, d), b_norm.reshape(1, d),
      w_out, x)


def _cross_kernel(x_ref, g_ref, wq_ref, kv_ref, wo_ref, fin_ref, o_ref, *, final):
    x = x_ref[...]
    hn = _rms(x, g_ref[...]).astype(MXU_DTYPE)
    q = _dot(hn, wq_ref[...]).astype(MXU_DTYPE)
    d = X_HEADS * X_DH
    outs = []
    for h in range(X_HEADS):
        k = kv_ref[:, h * X_DH:(h + 1) * X_DH]
        v = kv_ref[:, d + h * X_DH:d + (h + 1) * X_DH]
        sc = _dot_nt(q[:, h * X_DH:(h + 1) * X_DH], k) * (X_DH ** -0.5)
        e = jnp.exp(sc - jnp.max(sc, axis=-1, keepdims=True))
        p = e / jnp.sum(e, axis=-1, keepdims=True)
        outs.append(_dot(p.astype(MXU_DTYPE), v))
    o = jnp.concatenate(outs, axis=-1).astype(MXU_DTYPE)
    y = x + _dot(o, wo_ref[...])
    if final:
        y = _rms(y, fin_ref[...])
    o_ref[...] = y


def _cross_attn(x, kv, g, w_q, w_o, fin, *, seq, final):
    m, d = x.shape
    tm = min(TM_CROSS, seq)
    tiles_per_seq = seq // tm
    full = lambda i: (0, 0)
    return pl.pallas_call(
        functools.partial(_cross_kernel, final=final),
        grid=(m // tm,),
        in_specs=[pl.BlockSpec((tm, d), lambda i: (i, 0)),
                  pl.BlockSpec((1, d), full),
                  pl.BlockSpec(w_q.shape, full),
                  pl.BlockSpec((N_MEM, kv.shape[1]), lambda i: (i // tiles_per_seq, 0)),
                  pl.BlockSpec(w_o.shape, full),
                  pl.BlockSpec((1, d), full)],
        out_specs=pl.BlockSpec((tm, d), lambda i: (i, 0)),
        out_shape=jax.ShapeDtypeStruct((m, d), F32),
        compiler_params=_cparams(("parallel",), 48),
        name="cross_attn",
    )(x, g.reshape(1, d), w_q, kv, w_o, fin.reshape(1, d))


C_QK = C_NOPE + LANES


C_EXP2 = (C_NOPE + C_ROPE) ** -0.5 * float(np.log2(np.e))
C_VT = C_V + 8


def _q_up_kernel(x_ref, g_ref, w_ref, cos_ref, sin_ref, o_ref):
    hn = _rms(x_ref[...], g_ref[...]).astype(MXU_DTYPE)
    pad = jnp.zeros((C_QK - C_NOPE - C_ROPE, hn.shape[0]), o_ref.dtype)
    for h in range(C_HEADS):
        rt = _dot_nt(w_ref[h], hn) * C_EXP2
        o_ref[0, h, :C_NOPE, :] = rt[:C_NOPE].astype(o_ref.dtype)
        rope = rt[C_NOPE:C_NOPE + C_ROPE] * cos_ref[...] + rt[C_NOPE + C_ROPE:] * sin_ref[...]
        o_ref[0, h, C_NOPE:C_NOPE + C_ROPE, :] = rope.astype(o_ref.dtype)
        o_ref[0, h, C_NOPE + C_ROPE:, :] = pad


def _q_up(cin, q_norm, w, cos_t, sin_t, *, batch, seq):
    tm = min(TM_UP, seq)
    tps = seq // tm
    return pl.pallas_call(
        _q_up_kernel,
        grid=(batch * tps,),
        in_specs=[pl.BlockSpec((tm, C_Q_LORA), lambda i: (i, C_QL // C_Q_LORA)),
                  pl.BlockSpec((1, C_Q_LORA), lambda i: (0, 0)),
                  pl.BlockSpec(w.shape, lambda i: (0, 0, 0)),
                  pl.BlockSpec((C_ROPE, tm), lambda i: (0, i % tps)),
                  pl.BlockSpec((C_ROPE, tm), lambda i: (0, i % tps))],
        out_specs=pl.BlockSpec((1, C_HEADS, C_QK, tm), lambda i: (i // tps, 0, 0, i % tps)),
        out_shape=jax.ShapeDtypeStruct((batch, C_HEADS, C_QK, seq), MXU_DTYPE),
        compiler_params=_cparams(("parallel",), 32),
        name="mla_q_up",
    )(cin, q_norm.reshape(1, C_Q_LORA), w, cos_t, sin_t)


def _kv_up_kernel(x_ref, g_ref, wk_ref, wvt_ref, kr_ref, ksw_ref, cos_ref, sin_ref, k_ref, vt_ref):
    hn = _rms(x_ref[...], g_ref[...]).astype(MXU_DTYPE)
    rope = (kr_ref[...] * cos_ref[...] + ksw_ref[...] * sin_ref[...]).astype(k_ref.dtype)
    tm = hn.shape[0]
    ones_rows = (lax.broadcasted_iota(jnp.int32, (C_VT - C_V, tm), 0) == 0).astype(vt_ref.dtype)
    for h in range(C_HEADS):
        k_ref[0, h, :, :C_NOPE] = _dot(hn, wk_ref[h]).astype(k_ref.dtype)
        k_ref[0, h, :, C_NOPE:] = rope
        vt_ref[0, h, :C_V, :] = _dot_nt(wvt_ref[h], hn).astype(vt_ref.dtype)
        vt_ref[0, h, C_V:, :] = ones_rows


def _kv_up(cin, kv_norm, wk, wvt, cos, sin, *, batch, seq):
    tm = min(TM_UP, seq)
    tps = seq // tm
    return pl.pallas_call(
        _kv_up_kernel,
        grid=(batch * tps,),
        in_specs=[pl.BlockSpec((tm, C_KV_LORA), lambda i: (i, C_KV // C_KV_LORA)),
                  pl.BlockSpec((1, C_KV_LORA), lambda i: (0, 0)),
                  pl.BlockSpec(wk.shape, lambda i: (0, 0, 0)),
                  pl.BlockSpec(wvt.shape, lambda i: (0, 0, 0)),
                  pl.BlockSpec((tm, LANES), lambda i: (i, C_KR // LANES)),
                  pl.BlockSpec((tm, LANES), lambda i: (i, C_KSW // LANES)),
                  pl.BlockSpec((tm, LANES), lambda i: (i % tps, 0)),
                  pl.BlockSpec((tm, LANES), lambda i: (i % tps, 0))],
        out_specs=[pl.BlockSpec((1, C_HEADS, tm, C_QK), lambda i: (i // tps, 0, i % tps, 0)),
                   pl.BlockSpec((1, C_HEADS, C_VT, tm), lambda i: (i // tps, 0, 0, i % tps))],
        out_shape=[jax.ShapeDtypeStruct((batch, C_HEADS, seq, C_QK), MXU_DTYPE),
                   jax.ShapeDtypeStruct((batch, C_HEADS, C_VT, seq), MXU_DTYPE)],
        compiler_params=_cparams(("parallel",), 32),
        name="mla_kv_up",
    )(cin, kv_norm.reshape(1, C_KV_LORA), wk, wvt, cin, cin, cos, sin)


def _flash_kernel(qt_ref, k_ref, vt_ref, z_ref, o_ref, m_scr, acc_scr):
    j = pl.program_id(2)
    bk, bq = k_ref.shape[2], qt_ref.shape[3]

    @pl.when(j == 0)
    def _():
        m_scr[...] = jnp.full_like(m_scr, -jnp.inf)
        acc_scr[...] = jnp.zeros_like(acc_scr)

    qs = min(QSUB, bq)
    units = [(h, a * qs) for h in range(C_HEADS) for a in range(bq // qs)]

    def scores(h, q0):
        return _dot(k_ref[0, h], qt_ref[0, h, :, q0:q0 + qs]).reshape(bk // 8, 8, qs)

    st = scores(*units[0])
    for n, (h, q0) in enumerate(units):
        st_next = scores(*units[n + 1]) if n + 1 < len(units) else None
        m_prev = m_scr[h, :, q0:q0 + qs]
        m_new = jnp.maximum(m_prev, jnp.max(jnp.max(st, axis=0), axis=0, keepdims=True))
        p = jnp.exp2(st - m_new[None]).reshape(bk, qs).astype(MXU_DTYPE)
        alpha = jnp.exp2(m_prev - m_new)
        pv = _dot(vt_ref[0, h], p)
        acc = acc_scr[h, :, q0:q0 + qs].reshape(C_VT // 8, 8, qs) * alpha[None]
        acc_scr[h, :, q0:q0 + qs] = acc.reshape(C_VT, qs) + pv
        m_scr[h, :, q0:q0 + qs] = m_new
        st = st_next

    @pl.when(j == pl.num_programs(2) - 1)
    def _():
        for h in range(C_HEADS):
            acc = acc_scr[h]
            o = (acc[:C_V] / acc[C_V:C_V + 1]).T
            zh = z_ref[:, h * C_V:(h + 1) * C_V]
            o_ref[:, h * C_V:(h + 1) * C_V] = (o * _silu(zh)).astype(o_ref.dtype)


def _flash(qt, k, vt, cin, *, batch, seq):
    bq, bk = min(BQ, seq), min(BK, seq)
    nq, nk = seq // bq, seq // bk
    d = C_HEADS * C_V
    return pl.pallas_call(
        _flash_kernel,
        grid=(batch, nq, nk),
        in_specs=[pl.BlockSpec((1, C_HEADS, C_QK, bq), lambda b, i, j: (b, 0, 0, i)),
                  pl.BlockSpec((1, C_HEADS, bk, C_QK), lambda b, i, j: (b, 0, j, 0)),
                  pl.BlockSpec((1, C_HEADS, C_VT, bk), lambda b, i, j: (b, 0, 0, j)),
                  pl.BlockSpec((bq, d), lambda b, i, j: (b * nq + i, C_Z // d))],
        out_specs=pl.BlockSpec((bq, d), lambda b, i, j: (b * nq + i, 0)),
        out_shape=jax.ShapeDtypeStruct((batch * seq, d), MXU_DTYPE),
        scratch_shapes=[pltpu.VMEM((C_HEADS, 8, bq), F32), pltpu.VMEM((C_HEADS, C_VT, bq), F32)],
        compiler_params=_cparams(("parallel", "parallel", "arbitrary"), 56),
        name="mla_flash",
    )(qt, k, vt, cin)


def _split_cols(w, sizes):
    idx = np.cumsum(np.array(sizes))[:-1].tolist()
    return jnp.split(w, idx, axis=-1)


def _rope_swap(w):
    half = w.shape[-1] // 2
    return jnp.concatenate([-w[..., half:], w[..., :half]], axis=-1)


def _pad_cols(w, n):
    return jnp.pad(w, [(0, 0)] * (w.ndim - 1) + [(0, n - w.shape[-1])])


def _prep_ab(w_in, alpha_w2, alpha_b, igate_b, fgate_b):
    sizes = (2 * A_HEADS * A_DK, A_HEADS * A_DV, 2 * A_HEADS, 2 * A_HEADS, A_HEADS * A_DV, A_HEADS * A_DV,
             B_HEADS * B_DK, B_HEADS * B_DK, B_HEADS * B_DV, 2 * B_GATE_RANK, B_HEADS * B_DV)
    a_qk, a_v, a_i, a_f, a_o, a_z, b_q, b_k, b_v, b_low, b_z = _split_cols(w_in, sizes)
    small = _pad_cols(jnp.concatenate([a_i, a_f, b_low], axis=-1), LANES)
    w = jnp.concatenate([a_qk, a_o, a_z, b_q, b_k, b_z, a_v, b_v, small], axis=-1).astype(MXU_DTYPE)
    hk = B_HEADS * B_DK
    w2 = jnp.zeros((LANES, 2 * hk), F32)
    w2 = w2.at[SM_LOW:SM_LOW + B_GATE_RANK, :hk].set(alpha_w2[0])
    w2 = w2.at[SM_LOW + B_GATE_RANK:SM_LOW + 2 * B_GATE_RANK, hk:].set(alpha_w2[1])
    ab = alpha_b.reshape(1, 2 * hk)
    gbias = _pad_cols(jnp.concatenate([igate_b.reshape(1, -1), fgate_b.reshape(1, -1)], axis=-1), LANES)
    return w, w2.astype(MXU_DTYPE), ab, gbias


def _prep_c(w_in, w_q_up, w_kv_up):
    q_lat, kv_lat, k_rope, z = _split_cols(w_in, (C_Q_LORA, C_KV_LORA, C_ROPE, C_HEADS * C_V))
    w_c = jnp.concatenate([z, kv_lat, _pad_cols(k_rope, LANES), _pad_cols(_rope_swap(k_rope), LANES), q_lat],
                          axis=-1).astype(MXU_DTYPE)
    wq = w_q_up.reshape(C_Q_LORA, C_HEADS, C_NOPE + C_ROPE)
    nope, rope = wq[..., :C_NOPE], wq[..., C_NOPE:]
    wq_t = jnp.transpose(jnp.concatenate([nope, rope, _rope_swap(rope)], axis=-1), (1, 2, 0))
    wkv = w_kv_up.reshape(C_KV_LORA, C_HEADS, C_NOPE + C_V)
    wk = jnp.transpose(wkv[..., :C_NOPE], (1, 0, 2))
    wv_t = jnp.transpose(wkv[..., C_NOPE:], (1, 2, 0))
    return w_c, wq_t.astype(MXU_DTYPE), wk.astype(MXU_DTYPE), wv_t.astype(MXU_DTYPE)


def _rope_tables(seq):
    inv = ROPE_BASE ** (-jnp.arange(0, C_ROPE, 2, dtype=F32) / C_ROPE)
    ang = jnp.arange(seq, dtype=F32)[:, None] * inv[None, :]
    cos, sin = jnp.cos(ang), jnp.sin(ang)
    cos2, sin2 = jnp.concatenate([cos, cos], axis=-1), jnp.concatenate([sin, sin], axis=-1)
    return _pad_cols(cos2, LANES), _pad_cols(sin2, LANES), cos2.T, sin2.T


def _trunk(x, mem, p, prep):
    batch, seq, d = x.shape
    x = x.reshape(batch * seq, d)
    mem = mem.reshape(batch * N_MEM, d)
    cos, sin, cos_t, sin_t = _rope_tables(seq)
    for l in range(DEPTH):
        j = l // 2
        if l % 2 == 0:
            w_in, w2, ab, gbias = prep['ab'][j]
            proj = _norm_matmul(x, p['norm_mix'][l], w_in, tm=TM_PROJ, tn=AB_TN, out_dtype=F32)
            qk, kt = _conv_silu(proj, p['ab_conv_w'][j], p['ab_conv_b'][j], seq=seq)
            h_f, h_b = _mlstm(qk, kt, proj, gbias, batch=batch, seq=seq)
            o_f, o_b = _gla(proj, w2, ab, batch=batch, seq=seq)
            x = _ab_post(h_f, h_b, o_f, o_b, proj, p['a_ogate_b'][j], p['a_head_norm'][j], p['b_head_norm'][j],
                         prep['ab_w_out'][j], x)
        else:
            w_c, wq_t, wk, wv_t = prep['c'][j]
            cin = _norm_matmul(x, p['norm_mix'][l], w_c, tm=TM_PROJ, tn=C_TN, out_dtype=F32)
            qt = _q_up(cin, p['c_q_norm'][j], wq_t, cos_t, sin_t, batch=batch, seq=seq)
            k, vt = _kv_up(cin, p['c_kv_norm'][j], wk, wv_t, cos, sin, batch=batch, seq=seq)
            og = _flash(qt, k, vt, cin, batch=batch, seq=seq)
            x = _matmul_residual(og, prep['c_w_out'][j], x, tm=TM_OUT)
        kv = _norm_matmul(mem, p['norm_mem'][l], prep['x_w_kv'][l], tm=N_MEM, tn=1024, out_dtype=MXU_DTYPE)
        x = _cross_attn(x, kv, p['norm_cross'][l], prep['x_w_q'][l], prep['x_w_o'][l], p['final_norm'],
                        seq=seq, final=(l == DEPTH - 1))
    return x.reshape(batch, seq, d)


def kernel(x_prompt, x_sample, mem_prompt, mem_sample, norm_mix, norm_cross, norm_mem, ab_w_in, ab_conv_w, ab_conv_b, a_igate_b, a_fgate_b, a_ogate_b, a_head_norm, b_alpha_w2, b_alpha_b, b_head_norm, ab_w_out, c_w_in, c_q_norm, c_kv_norm, c_w_q_up, c_w_kv_up, c_w_out, x_w_q, x_w_kv, x_w_o, final_norm):
    p = dict(norm_mix=norm_mix, norm_cross=norm_cross, norm_mem=norm_mem, ab_conv_w=ab_conv_w,
             ab_conv_b=ab_conv_b, a_ogate_b=a_ogate_b, a_head_norm=a_head_norm, b_head_norm=b_head_norm,
             c_q_norm=c_q_norm, c_kv_norm=c_kv_norm, final_norm=final_norm)
    n_even, n_odd = ab_w_in.shape[0], c_w_in.shape[0]
    prep = dict(
        ab=[_prep_ab(ab_w_in[j], b_alpha_w2[j], b_alpha_b[j], a_igate_b[j], a_fgate_b[j]) for j in range(n_even)],
        ab_w_out=[ab_w_out[j].astype(MXU_DTYPE) for j in range(n_even)],
        c=[_prep_c(c_w_in[j], c_w_q_up[j], c_w_kv_up[j]) for j in range(n_odd)],
        c_w_out=[c_w_out[j].astype(MXU_DTYPE) for j in range(n_odd)],
        x_w_q=[x_w_q[l].astype(MXU_DTYPE) for l in range(DEPTH)],
        x_w_kv=[x_w_kv[l].astype(MXU_DTYPE) for l in range(DEPTH)],
        x_w_o=[x_w_o[l].astype(MXU_DTYPE) for l in range(DEPTH)],
    )
    return (_trunk(x_prompt, mem_prompt, p, prep), _trunk(x_sample, mem_sample, p, prep))
```

```python
import functools

import numpy as np
import jax
import jax.numpy as jnp
from jax import lax
from jax.experimental import pallas as pl
from jax.experimental.pallas import tpu as pltpu

F32 = jnp.float32
MXU_DTYPE = jnp.bfloat16
EPS = 1e-6
HIGHEST = lax.Precision.HIGHEST

D_MODEL = 1024
DEPTH = 4
A_HEADS, A_DK, A_DV, A_CHUNK = 4, 128, 256, 128
B_HEADS, B_DK, B_DV, B_CHUNK = 4, 128, 256, 64
B_GATE_RANK, B_GATE_TAU = 16, 16.0
B_SUB = 16
C_HEADS, C_Q_LORA, C_KV_LORA, C_NOPE, C_ROPE, C_V = 8, 384, 256, 128, 64, 128
ROPE_BASE = 10000.0
X_HEADS, X_DH, N_MEM = 4, 256, 256

LANES = 128
V7X_VMEM_BYTES = 64 * 1024 * 1024

AB_QK, AB_O, AB_Z, AB_BQ, AB_BK, AB_BZ, AB_AV, AB_BV, AB_SMALL = (
    0, 1024, 2048, 3072, 3584, 4096, 5120, 6144, 7168)
AB_COLS = 7296
AB_TN = 2432
SM_I, SM_F, SM_LOW = 0, 8, 16
C_Z, C_KV, C_KR, C_KSW, C_QL = 0, 1024, 1280, 1408, 1536
C_COLS = 1920
C_TN = 640

TM_PROJ = 1024
TM_CONV = 512
TM_POST = 256
TM_CROSS = 512
TM_UP = 512
TM_OUT = 512
BQ, BK, QSUB = 1024, 512, 512


def _cparams(sem, vmem_mb=48):
    assert vmem_mb * 1024 * 1024 <= V7X_VMEM_BYTES
    return pltpu.CompilerParams(dimension_semantics=sem, vmem_limit_bytes=vmem_mb * 1024 * 1024)


def _sigmoid(x):
    return 1.0 / (1.0 + jnp.exp(-x))


def _silu(x):
    return x * _sigmoid(x)


def _log_sigmoid(x):
    return jnp.minimum(x, 0.0) - jnp.log1p(jnp.exp(-jnp.abs(x)))


def _dot(a, b):
    return jnp.dot(a, b, preferred_element_type=F32)


def _dot_nt(a, b):
    return lax.dot_general(a, b, (((1,), (1,)), ((), ())), preferred_element_type=F32)


def _dot_tn(a, b):
    return lax.dot_general(a, b, (((0,), (0,)), ((), ())), preferred_element_type=F32)


def _rms(x, g):
    ms = jnp.mean(x * x, axis=-1, keepdims=True)
    return x * lax.rsqrt(ms + EPS) * g


def _head_rms(x, g, n_heads):
    w = x.shape[-1] // n_heads
    parts = []
    for h in range(n_heads):
        xs = x[:, h * w:(h + 1) * w]
        ms = jnp.mean(xs * xs, axis=-1, keepdims=True)
        parts.append(xs * lax.rsqrt(ms + EPS))
    return jnp.concatenate(parts, axis=-1) * g


def _norm_matmul_kernel(x_ref, g_ref, w_ref, o_ref, h_ref):
    @pl.when(pl.program_id(1) == 0)
    def _():
        h_ref[...] = _rms(x_ref[...], g_ref[...]).astype(h_ref.dtype)

    o_ref[...] = _dot(h_ref[...], w_ref[...]).astype(o_ref.dtype)


def _norm_matmul(x, g, w, *, tm, tn, out_dtype, x_col_block=0):
    m = x.shape[0]
    k, n = w.shape
    tm = min(tm, m)
    return pl.pallas_call(
        _norm_matmul_kernel,
        grid=(m // tm, n // tn),
        in_specs=[pl.BlockSpec((tm, k), lambda i, j: (i, x_col_block)),
                  pl.BlockSpec((1, k), lambda i, j: (0, 0)),
                  pl.BlockSpec((k, tn), lambda i, j: (0, j))],
        out_specs=pl.BlockSpec((tm, tn), lambda i, j: (i, j)),
        out_shape=jax.ShapeDtypeStruct((m, n), out_dtype),
        scratch_shapes=[pltpu.VMEM((tm, k), MXU_DTYPE)],
        compiler_params=_cparams(("parallel", "arbitrary"), 56),
        name="norm_matmul",
    )(x, g.reshape(1, k), w)


def _matmul_res_kernel(a_ref, w_ref, x_ref, o_ref):
    o_ref[...] = x_ref[...] + _dot(a_ref[...], w_ref[...])


def _matmul_residual(a, w, x, *, tm):
    m, k = a.shape
    n = w.shape[1]
    tm = min(tm, m)
    return pl.pallas_call(
        _matmul_res_kernel,
        grid=(m // tm,),
        in_specs=[pl.BlockSpec((tm, k), lambda i: (i, 0)),
                  pl.BlockSpec((k, n), lambda i: (0, 0)),
                  pl.BlockSpec((tm, n), lambda i: (i, 0))],
        out_specs=pl.BlockSpec((tm, n), lambda i: (i, 0)),
        out_shape=jax.ShapeDtypeStruct((m, n), F32),
        compiler_params=_cparams(("parallel",), 40),
        name="matmul_residual",
    )(a, w, x)


def _conv_kernel(x_ref, xp_ref, xn_ref, w_ref, b_ref, o_ref, kt_ref, *, tiles_per_seq):
    i = pl.program_id(0)
    x = x_ref[...]
    ts = x.shape[0]
    t_in_seq = i % tiles_per_seq
    prev_row = jnp.where(t_in_seq == 0, 0.0, xp_ref[7:8, :])
    next_row = jnp.where(t_in_seq == tiles_per_seq - 1, 0.0, xn_ref[0:1, :])
    rows = lax.broadcasted_iota(jnp.int32, x.shape, 0)
    x_prev = jnp.where(rows == 0, prev_row, pltpu.roll(x, 1, axis=0))
    x_next = jnp.where(rows == ts - 1, next_row, pltpu.roll(x, ts - 1, axis=0))
    y = w_ref[0:1, :] * x_prev + w_ref[1:2, :] * x + w_ref[2:3, :] * x_next + b_ref[...]
    y = _silu(y)
    half = y.shape[1] // 2
    o_ref[:, :half] = (y[:, :half] * (A_DK ** -0.5)).astype(o_ref.dtype)
    o_ref[:, half:] = y[:, half:].astype(o_ref.dtype)
    kt_ref[...] = y[:, half:].T.astype(kt_ref.dtype)


def _conv_silu(proj, conv_w, conv_b, *, seq):
    m = proj.shape[0]
    c = conv_w.shape[1]
    ts = min(TM_CONV, seq)
    nt = m // ts
    sub = ts // 8
    return pl.pallas_call(
        functools.partial(_conv_kernel, tiles_per_seq=seq // ts),
        grid=(nt,),
        in_specs=[pl.BlockSpec((ts, c), lambda i: (i, 0)),
                  pl.BlockSpec((8, c), lambda i: (jnp.maximum(i * sub - 1, 0), 0)),
                  pl.BlockSpec((8, c), lambda i: (jnp.minimum((i + 1) * sub, nt * sub - 1), 0)),
                  pl.BlockSpec((3, c), lambda i: (0, 0)),
                  pl.BlockSpec((1, c), lambda i: (0, 0))],
        out_specs=[pl.BlockSpec((ts, c), lambda i: (i, 0)),
                   pl.BlockSpec((c // 2, ts), lambda i: (0, i))],
        out_shape=[jax.ShapeDtypeStruct((m, c), MXU_DTYPE),
                   jax.ShapeDtypeStruct((c // 2, m), MXU_DTYPE)],
        compiler_params=_cparams(("parallel",), 32),
        name="conv_silu",
    )(proj, proj, proj, conv_w, conv_b.reshape(1, c))


def _mlstm_kernel(qk_f, kt_f, v_f, g_f, gn_f, qk_b, kt_b, v_b, g_b, gn_b, gbias_ref, h_f, h_b,
                  c_scr, n_scr, m_scr, gate_scr):
    L = A_CHUNK

    row = lax.broadcasted_iota(jnp.int32, (L, L), 0)
    col = lax.broadcasted_iota(jnp.int32, (L, L), 1)
    sub8 = lax.broadcasted_iota(jnp.int32, (8, L), 0)
    lane_row = lax.broadcasted_iota(jnp.int32, (1, LANES), 1)
    neg_inf = float("-inf")
    masks = (row >= col, row <= col)

    def gate_sums(gate_ref, d):
        gates = gate_ref[...] + gbias_ref[...]
        csum = jnp.dot(masks[d].astype(F32), _log_sigmoid(gates), precision=HIGHEST, preferred_element_type=F32)
        a_c = pltpu.roll(csum, LANES - (SM_F - SM_I), axis=1)
        e_c = gates - a_c
        run = e_c
        sh = 1
        while sh < L:
            pad = jnp.full((sh, LANES), neg_inf, F32)
            moved = (jnp.concatenate([pad, run[:L - sh]], axis=0) if d == 0
                     else jnp.concatenate([run[sh:], pad], axis=0))
            run = jnp.maximum(run, moved)
            sh *= 2
        return a_c, run, e_c.T

    def put_gate_sums(d, vals):
        for n, val in enumerate(vals):
            gate_scr[d, n] = val

    @pl.when(pl.program_id(1) == 0)
    def _():
        c_scr[...] = jnp.zeros_like(c_scr)
        n_scr[...] = jnp.zeros_like(n_scr)
        m_scr[...] = jnp.zeros_like(m_scr)
        put_gate_sums(0, gate_sums(g_f, 0))
        put_gate_sums(1, gate_sums(g_b, 1))

    m_all = m_scr[0:1, :]
    m_next = m_all
    dirs = ((qk_f, kt_f, v_f, g_f, h_f), (qk_b, kt_b, v_b, g_b, h_b))
    gate_next = [gate_sums(gn_f, 0), gate_sums(gn_b, 1)]
    pro = []
    for d in range(2):
        a_c, run, e_t = gate_scr[d, 0], gate_scr[d, 1], gate_scr[d, 2]
        m_c = jnp.maximum(run, m_all)
        w_inter_c = jnp.exp(m_all - m_c)
        inv_floor_c = jnp.exp(-(a_c + m_c))
        pro.append((masks[d], a_c, m_c, w_inter_c, inv_floor_c, e_t, m_c.T))

    for d, (qk_ref, kt_ref, v_ref, g_ref, h_ref) in enumerate(dirs):
        mask, a_c, m_c, w_inter_c, inv_floor_c, e_t, m_t_rows = pro[d]
        end = L - 1 if d == 0 else 0
        for h in range(A_HEADS):
            s = d * A_HEADS + h
            n_prev = n_scr[s]
            c_prev = c_scr[s]
            q = qk_ref[:, h * A_DK:(h + 1) * A_DK]
            k = qk_ref[:, (A_HEADS + h) * A_DK:(A_HEADS + h + 1) * A_DK]
            kt = kt_ref[h * A_DK:(h + 1) * A_DK, :]
            v = v_ref[:, h * A_DV:(h + 1) * A_DV].astype(MXU_DTYPE)
            e_r, m_r = e_t[s:s + 1, :], m_t_rows[s:s + 1, :]

            lhs = jnp.where(sub8 == 0, -m_r, jnp.where(sub8 == 1, 1.0, 0.0))
            rhs = jnp.where(sub8 == 0, 1.0, jnp.where(sub8 == 1, e_r, 0.0))
            x = lax.dot_general(lhs, rhs, (((0,), (0,)), ((), ())), precision=HIGHEST,
                                preferred_element_type=F32)
            sc = _dot_nt(q, k) * jnp.exp(jnp.where(mask, x, neg_inf))
            w_inter = w_inter_c[:, s:s + 1]
            num = _dot(sc.astype(MXU_DTYPE), v) + w_inter * _dot(q, c_prev.astype(MXU_DTYPE))
            qn = _dot_nt(q, n_prev.astype(MXU_DTYPE))[:, 0:1]
            den = jnp.sum(sc, axis=1, keepdims=True) + w_inter * qn
            h_ref[:, h * A_DV:(h + 1) * A_DV] = num / jnp.maximum(jnp.abs(den), inv_floor_c[:, s:s + 1])

            g = a_c[end:end + 1, s:s + 1]
            m_end = m_c[end:end + 1, s:s + 1]
            w = jnp.exp(e_r - m_end)
            decay = w_inter_c[end:end + 1, s:s + 1]
            kwt = (kt.astype(F32) * w).astype(MXU_DTYPE)
            c_scr[s] = decay * c_prev + _dot(kwt, v)
            n_scr[s] = decay * n_prev + _dot(jnp.broadcast_to(w, (8, L)).astype(MXU_DTYPE), k)
            m_next = jnp.where(lane_row == s, g + m_end, m_next)
    m_scr[...] = jnp.broadcast_to(m_next, m_scr.shape)
    put_gate_sums(0, gate_next[0])
    put_gate_sums(1, gate_next[1])


def _mlstm(qk, kt, proj, gbias, *, batch, seq):
    m = qk.shape[0]
    L = A_CHUNK
    nc = seq // L
    d_a = A_HEADS * A_DV
    v_blk = AB_AV // d_a
    sm_blk = AB_SMALL // LANES
    assert L == LANES
    fwd = lambda b, c: b * nc + c
    bwd = lambda b, c: b * nc + (nc - 1 - c)
    nxt = lambda c: jnp.minimum(c + 1, nc - 1)
    out = jax.ShapeDtypeStruct((m, d_a), F32)
    return pl.pallas_call(
        _mlstm_kernel,
        grid=(batch, nc),
        in_specs=[pl.BlockSpec((L, 2 * A_HEADS * A_DK), lambda b, c: (fwd(b, c), 0)),
                  pl.BlockSpec((A_HEADS * A_DK, L), lambda b, c: (0, fwd(b, c))),
                  pl.BlockSpec((L, d_a), lambda b, c: (fwd(b, c), v_blk)),
                  pl.BlockSpec((L, LANES), lambda b, c: (fwd(b, c), sm_blk)),
                  pl.BlockSpec((L, LANES), lambda b, c: (fwd(b, nxt(c)), sm_blk)),
                  pl.BlockSpec((L, 2 * A_HEADS * A_DK), lambda b, c: (bwd(b, c), 0)),
                  pl.BlockSpec((A_HEADS * A_DK, L), lambda b, c: (0, bwd(b, c))),
                  pl.BlockSpec((L, d_a), lambda b, c: (bwd(b, c), v_blk)),
                  pl.BlockSpec((L, LANES), lambda b, c: (bwd(b, c), sm_blk)),
                  pl.BlockSpec((L, LANES), lambda b, c: (bwd(b, nxt(c)), sm_blk)),
                  pl.BlockSpec((1, LANES), lambda b, c: (0, 0))],
        out_specs=[pl.BlockSpec((L, d_a), lambda b, c: (fwd(b, c), 0)),
                   pl.BlockSpec((L, d_a), lambda b, c: (bwd(b, c), 0))],
        out_shape=[out, out],
        scratch_shapes=[pltpu.VMEM((2 * A_HEADS, A_DK, A_DV), F32),
                        pltpu.VMEM((2 * A_HEADS, 8, A_DK), F32),
                        pltpu.VMEM((8, LANES), F32),
                        pltpu.VMEM((2, 3, L, LANES), F32)],
        compiler_params=_cparams(("parallel", "arbitrary"), 32),
        name="mlstm",
    )(qk, kt, proj, proj, proj, qk, kt, proj, proj, proj, gbias)


def _gla_kernel(q_f, k_f, v_f, g_f, gn_f, q_b, k_b, v_b, g_b, gn_b, w2_ref, ab_ref, o_f, o_b, st_scr, bc_scr):
    L, SB = B_CHUNK, B_SUB
    nsb = L // SB
    hk = B_HEADS * B_DK

    row = lax.broadcasted_iota(jnp.int32, (L, L), 0)
    col = lax.broadcasted_iota(jnp.int32, (L, L), 1)
    lane8 = lax.broadcasted_iota(jnp.int32, (8, L), 1)
    log2e = float(np.log2(np.e))
    dirs = ((q_f, k_f, v_f, g_f, o_f), (q_b, k_b, v_b, g_b, o_b))
    masks = (row >= col, row <= col)

    def decay_sums(gate_ref, d):
        pre = _dot(gate_ref[...].astype(MXU_DTYPE), w2_ref[:, d * hk:(d + 1) * hk]) + ab_ref[:, d * hk:(d + 1) * hk]
        log_a = _log_sigmoid(pre) / B_GATE_TAU
        return jnp.dot(masks[d].astype(F32), log_a, precision=HIGHEST, preferred_element_type=F32)

    @pl.when(pl.program_id(1) == 0)
    def _():
        st_scr[...] = jnp.zeros_like(st_scr)
        bc_scr[0] = decay_sums(g_f, 0)
        bc_scr[1] = decay_sums(g_b, 1)

    bc_all = [bc_scr[0], bc_scr[1]]
    bc_next = [decay_sums(gn_f, 0), decay_sums(gn_b, 1)]

    work = []
    for d, (q_ref, k_ref, v_ref, g_ref, o_ref) in enumerate(dirs):
        for h in range(B_HEADS):
            s = d * B_HEADS + h
            q = q_ref[:, h * B_DK:(h + 1) * B_DK] * (B_DK ** -0.5)
            k = k_ref[:, h * B_DK:(h + 1) * B_DK]
            b = bc_all[d][:, h * B_DK:(h + 1) * B_DK]
            st = st_scr[s]
            inter = _dot_nt((q * jnp.exp(b)).astype(MXU_DTYPE), st.astype(MXU_DTYPE))

            b2 = b * log2e
            b2_rows = [jnp.broadcast_to(b2[j:j + 1], (8, B_DK)) for j in range(L)]
            pieces, where_to = [], []
            for blk in range(nsb):
                lo = blk * SB
                for oc in range(SB // 8):
                    r0 = lo + 8 * oc
                    js = range(0, 8 * oc + 8) if d == 0 else range(8 * oc, SB)
                    for j in js:
                        pieces.append(q[r0:r0 + 8] * jnp.exp2(b2[r0:r0 + 8] - b2_rows[lo + j]))
                        where_to.append((r0, lo + j))
            diag = _dot_nt(jnp.concatenate(pieces, axis=0).astype(MXU_DTYPE), k.astype(MXU_DTYPE))

            offs = []
            for blk in range(nsb):
                lo, hi = blk * SB, (blk + 1) * SB
                if d == 0 and blk > 0:
                    ref = b[lo - 1:lo]
                    kt = k[:lo] * jnp.exp(ref - b[:lo])
                    kt = jnp.concatenate([kt, jnp.zeros((L - lo, B_DK), F32)], axis=0)
                elif d == 1 and blk < nsb - 1:
                    ref = b[hi:hi + 1]
                    kt = k[hi:] * jnp.exp(ref - b[hi:])
                    kt = jnp.concatenate([jnp.zeros((hi, B_DK), F32), kt], axis=0)
                else:
                    offs.append(jnp.zeros((SB, L), F32))
                    continue
                offs.append(_dot_nt((q[lo:hi] * jnp.exp(b[lo:hi] - ref)).astype(MXU_DTYPE), kt.astype(MXU_DTYPE)))
            work.append((d, h, s, k, b, st, inter, diag, where_to, offs))

    for d, h, s, k, b, st, inter, diag, where_to, offs in work:
        v = dirs[d][2][:, h * B_DV:(h + 1) * B_DV].astype(MXU_DTYPE)
        att_rows = []
        for blk in range(nsb):
            for oc in range(SB // 8):
                r0 = blk * SB + 8 * oc
                a8 = offs[blk][8 * oc:8 * oc + 8]
                for n, (rr, cc) in enumerate(where_to):
                    if rr == r0:
                        a8 = jnp.where(lane8 == cc, diag[8 * n:8 * n + 8], a8)
                att_rows.append(a8)
        att = jnp.where(masks[d], jnp.concatenate(att_rows, axis=0), 0.0)
        dirs[d][4][:, h * B_DV:(h + 1) * B_DV] = inter + _dot(att.astype(MXU_DTYPE), v)

    for d, h, s, k, b, st, inter, diag, where_to, offs in work:
        v = dirs[d][2][:, h * B_DV:(h + 1) * B_DV].astype(MXU_DTYPE)
        g = b[L - 1:L, :] if d == 0 else b[0:1, :]
        kd = (k * jnp.exp(g - b)).astype(MXU_DTYPE)
        st_scr[s] = jnp.exp(g) * st + _dot_tn(v, kd)

    bc_scr[0] = bc_next[0]
    bc_scr[1] = bc_next[1]


def _gla(proj, w2, ab, *, batch, seq):
    m = proj.shape[0]
    L = B_CHUNK
    nc = seq // L
    hk = B_HEADS * B_DK
    d_b = B_HEADS * B_DV
    q_blk, k_blk, v_blk, sm_blk = AB_BQ // hk, AB_BK // hk, AB_BV // d_b, AB_SMALL // LANES
    fwd = lambda b, c: b * nc + c
    bwd = lambda b, c: b * nc + (nc - 1 - c)

    def specs(idx):
        nxt = lambda b, c: idx(b, jnp.minimum(c + 1, nc - 1))
        return [pl.BlockSpec((L, hk), lambda b, c: (idx(b, c), q_blk)),
                pl.BlockSpec((L, hk), lambda b, c: (idx(b, c), k_blk)),
                pl.BlockSpec((L, d_b), lambda b, c: (idx(b, c), v_blk)),
                pl.BlockSpec((L, LANES), lambda b, c: (idx(b, c), sm_blk)),
                pl.BlockSpec((L, LANES), lambda b, c: (nxt(b, c), sm_blk))]

    out = jax.ShapeDtypeStruct((m, d_b), F32)
    return pl.pallas_call(
        _gla_kernel,
        grid=(batch, nc),
        in_specs=specs(fwd) + specs(bwd) + [pl.BlockSpec((LANES, 2 * hk), lambda b, c: (0, 0)),
                                            pl.BlockSpec((1, 2 * hk), lambda b, c: (0, 0))],
        out_specs=[pl.BlockSpec((L, d_b), lambda b, c: (fwd(b, c), 0)),
                   pl.BlockSpec((L, d_b), lambda b, c: (bwd(b, c), 0))],
        out_shape=[out, out],
        scratch_shapes=[pltpu.VMEM((2 * B_HEADS, B_DV, B_DK), F32),
                        pltpu.VMEM((2, L, hk), F32)],
        compiler_params=_cparams(("parallel", "arbitrary"), 32),
        name="gla",
    )(proj, proj, proj, proj, proj, proj, proj, proj, proj, proj, w2, ab)


def _ab_post_kernel(hf_ref, hb_ref, ao_ref, az_ref, of_ref, ob_ref, bz_ref, ogb_ref, an_ref, bn_ref,
                    w_ref, x_ref, o_ref):
    d_a = A_HEADS * A_DV
    out_a = _sigmoid(ao_ref[...] + ogb_ref[...]) * (hf_ref[...] + hb_ref[...])
    out_a = _head_rms(out_a, an_ref[...], A_HEADS) * _silu(az_ref[...])
    out_b = _head_rms(of_ref[...] + ob_ref[...], bn_ref[...], B_HEADS) * _silu(bz_ref[...])
    y = _dot(out_a.astype(MXU_DTYPE), w_ref[:d_a, :]) + _dot(out_b.astype(MXU_DTYPE), w_ref[d_a:, :])
    o_ref[...] = x_ref[...] + y


def _ab_post(h_f, h_b, o_f, o_b, proj, ogate_b, a_norm, b_norm, w_out, x):
    m, d = x.shape
    tm = min(TM_POST, m)
    row = lambda i: (i, 0)
    vec = pl.BlockSpec((1, d), lambda i: (0, 0))
    blk = lambda cb: pl.BlockSpec((tm, d), lambda i: (i, cb))
    return pl.pallas_call(
        _ab_post_kernel,
        grid=(m // tm,),
        in_specs=[pl.BlockSpec((tm, d), row), pl.BlockSpec((tm, d), row), blk(AB_O // d), blk(AB_Z // d),
                  pl.BlockSpec((tm, d), row), pl.BlockSpec((tm, d), row), blk(AB_BZ // d),
                  vec, vec, vec, pl.BlockSpec(w_out.shape, lambda i: (0, 0)), pl.BlockSpec((tm, d), row)],
        out_specs=pl.BlockSpec((tm, d), row),
        out_shape=jax.ShapeDtypeStruct((m, d), F32),
        compiler_params=_cparams(("parallel",), 48),
        name="ab_post",
    )(h_f, h_b, proj, proj, o_f, o_b, proj, ogate_b.reshape(1, d), a_norm.reshape(1, d), b_norm.reshape(1, d),
      w_out, x)


def _cross_kernel(x_ref, g_ref, wq_ref, kv_ref, wo_ref, fin_ref, o_ref, *, final):
    x = x_ref[...]
    hn = _rms(x, g_ref[...]).astype(MXU_DTYPE)
    q = _dot(hn, wq_ref[...]).astype(MXU_DTYPE)
    d = X_HEADS * X_DH
    outs = []
    for h in range(X_HEADS):
        k = kv_ref[:, h * X_DH:(h + 1) * X_DH]
        v = kv_ref[:, d + h * X_DH:d + (h + 1) * X_DH]
        sc = _dot_nt(q[:, h * X_DH:(h + 1) * X_DH], k) * (X_DH ** -0.5)
        e = jnp.exp(sc - jnp.max(sc, axis=-1, keepdims=True))
        p = e / jnp.sum(e, axis=-1, keepdims=True)
        outs.append(_dot(p.astype(MXU_DTYPE), v))
    o = jnp.concatenate(outs, axis=-1).astype(MXU_DTYPE)
    y = x + _dot(o, wo_ref[...])
    if final:
        y = _rms(y, fin_ref[...])
    o_ref[...] = y


def _cross_attn(x, kv, g, w_q, w_o, fin, *, seq, final):
    m, d = x.shape
    tm = min(TM_CROSS, seq)
    tiles_per_seq = seq // tm
    full = lambda i: (0, 0)
    return pl.pallas_call(
        functools.partial(_cross_kernel, final=final),
        grid=(m // tm,),
        in_specs=[pl.BlockSpec((tm, d), lambda i: (i, 0)),
                  pl.BlockSpec((1, d), full),
                  pl.BlockSpec(w_q.shape, full),
                  pl.BlockSpec((N_MEM, kv.shape[1]), lambda i: (i // tiles_per_seq, 0)),
                  pl.BlockSpec(w_o.shape, full),
                  pl.BlockSpec((1, d), full)],
        out_specs=pl.BlockSpec((tm, d), lambda i: (i, 0)),
        out_shape=jax.ShapeDtypeStruct((m, d), F32),
        compiler_params=_cparams(("parallel",), 48),
        name="cross_attn",
    )(x, g.reshape(1, d), w_q, kv, w_o, fin.reshape(1, d))


C_QK = C_NOPE + LANES


C_EXP2 = (C_NOPE + C_ROPE) ** -0.5 * float(np.log2(np.e))
C_VT = C_V + 8


def _q_up_kernel(x_ref, g_ref, w_ref, cos_ref, sin_ref, o_ref):
    hn = _rms(x_ref[...], g_ref[...]).astype(MXU_DTYPE)
    pad = jnp.zeros((C_QK - C_NOPE - C_ROPE, hn.shape[0]), o_ref.dtype)
    for h in range(C_HEADS):
        rt = _dot_nt(w_ref[h], hn) * C_EXP2
        o_ref[0, h, :C_NOPE, :] = rt[:C_NOPE].astype(o_ref.dtype)
        rope = rt[C_NOPE:C_NOPE + C_ROPE] * cos_ref[...] + rt[C_NOPE + C_ROPE:] * sin_ref[...]
        o_ref[0, h, C_NOPE:C_NOPE + C_ROPE, :] = rope.astype(o_ref.dtype)
        o_ref[0, h, C_NOPE + C_ROPE:, :] = pad


def _q_up(cin, q_norm, w, cos_t, sin_t, *, batch, seq):
    tm = min(TM_UP, seq)
    tps = seq // tm
    return pl.pallas_call(
        _q_up_kernel,
        grid=(batch * tps,),
        in_specs=[pl.BlockSpec((tm, C_Q_LORA), lambda i: (i, C_QL // C_Q_LORA)),
                  pl.BlockSpec((1, C_Q_LORA), lambda i: (0, 0)),
                  pl.BlockSpec(w.shape, lambda i: (0, 0, 0)),
                  pl.BlockSpec((C_ROPE, tm), lambda i: (0, i % tps)),
                  pl.BlockSpec((C_ROPE, tm), lambda i: (0, i % tps))],
        out_specs=pl.BlockSpec((1, C_HEADS, C_QK, tm), lambda i: (i // tps, 0, 0, i % tps)),
        out_shape=jax.ShapeDtypeStruct((batch, C_HEADS, C_QK, seq), MXU_DTYPE),
        compiler_params=_cparams(("parallel",), 32),
        name="mla_q_up",
    )(cin, q_norm.reshape(1, C_Q_LORA), w, cos_t, sin_t)


def _kv_up_kernel(x_ref, g_ref, wk_ref, wvt_ref, kr_ref, ksw_ref, cos_ref, sin_ref, k_ref, vt_ref):
    hn = _rms(x_ref[...], g_ref[...]).astype(MXU_DTYPE)
    rope = (kr_ref[...] * cos_ref[...] + ksw_ref[...] * sin_ref[...]).astype(k_ref.dtype)
    tm = hn.shape[0]
    ones_rows = (lax.broadcasted_iota(jnp.int32, (C_VT - C_V, tm), 0) == 0).astype(vt_ref.dtype)
    for h in range(C_HEADS):
        k_ref[0, h, :, :C_NOPE] = _dot(hn, wk_ref[h]).astype(k_ref.dtype)
        k_ref[0, h, :, C_NOPE:] = rope
        vt_ref[0, h, :C_V, :] = _dot_nt(wvt_ref[h], hn).astype(vt_ref.dtype)
        vt_ref[0, h, C_V:, :] = ones_rows


def _kv_up(cin, kv_norm, wk, wvt, cos, sin, *, batch, seq):
    tm = min(TM_UP, seq)
    tps = seq // tm
    return pl.pallas_call(
        _kv_up_kernel,
        grid=(batch * tps,),
        in_specs=[pl.BlockSpec((tm, C_KV_LORA), lambda i: (i, C_KV // C_KV_LORA)),
                  pl.BlockSpec((1, C_KV_LORA), lambda i: (0, 0)),
                  pl.BlockSpec(wk.shape, lambda i: (0, 0, 0)),
                  pl.BlockSpec(wvt.shape, lambda i: (0, 0, 0)),
                  pl.BlockSpec((tm, LANES), lambda i: (i, C_KR // LANES)),
                  pl.BlockSpec((tm, LANES), lambda i: (i, C_KSW // LANES)),
                  pl.BlockSpec((tm, LANES), lambda i: (i % tps, 0)),
                  pl.BlockSpec((tm, LANES), lambda i: (i % tps, 0))],
        out_specs=[pl.BlockSpec((1, C_HEADS, tm, C_QK), lambda i: (i // tps, 0, i % tps, 0)),
                   pl.BlockSpec((1, C_HEADS, C_VT, tm), lambda i: (i // tps, 0, 0, i % tps))],
        out_shape=[jax.ShapeDtypeStruct((batch, C_HEADS, seq, C_QK), MXU_DTYPE),
                   jax.ShapeDtypeStruct((batch, C_HEADS, C_VT, seq), MXU_DTYPE)],
        compiler_params=_cparams(("parallel",), 32),
        name="mla_kv_up",
    )(cin, kv_norm.reshape(1, C_KV_LORA), wk, wvt, cin, cin, cos, sin)


def _flash_kernel(*refs, n_sub):
    qt_refs = refs[:n_sub]
    k_ref, vt_ref, z_ref, o_ref, m_scr, acc_scr = refs[n_sub:]
    j = pl.program_id(2)
    bk, qs = k_ref.shape[2], qt_refs[0].shape[3]

    @pl.when(j == 0)
    def _():
        m_scr[...] = jnp.full_like(m_scr, -jnp.inf)
        acc_scr[...] = jnp.zeros_like(acc_scr)

    units = [(h, a) for h in range(C_HEADS) for a in range(n_sub)]

    def scores(h, a):
        return _dot(k_ref[0, h], qt_refs[a][0, h]).reshape(bk // 8, 8, qs)

    st = scores(*units[0])
    for n, (h, a) in enumerate(units):
        q0 = a * qs
        st_next = scores(*units[n + 1]) if n + 1 < len(units) else None
        m_prev = m_scr[h, :, q0:q0 + qs]
        m_new = jnp.maximum(m_prev, jnp.max(jnp.max(st, axis=0), axis=0, keepdims=True))
        p = jnp.exp2(st - m_new[None]).reshape(bk, qs).astype(MXU_DTYPE)
        alpha = jnp.exp2(m_prev - m_new)
        pv = _dot(vt_ref[0, h], p)
        acc = acc_scr[h, :, q0:q0 + qs].reshape(C_VT // 8, 8, qs) * alpha[None]
        acc_scr[h, :, q0:q0 + qs] = acc.reshape(C_VT, qs) + pv
        m_scr[h, :, q0:q0 + qs] = m_new
        st = st_next

    @pl.when(j == pl.num_programs(2) - 1)
    def _():
        for h in range(C_HEADS):
            acc = acc_scr[h]
            o = (acc[:C_V] / acc[C_V:C_V + 1]).T
            zh = z_ref[:, h * C_V:(h + 1) * C_V]
            o_ref[:, h * C_V:(h + 1) * C_V] = (o * _silu(zh)).astype(o_ref.dtype)


def _flash(qt, k, vt, cin, *, batch, seq):
    bq, bk = min(BQ, seq), min(BK, seq)
    qs = min(QSUB, bq)
    n_sub = bq // qs
    nq, nk = seq // bq, seq // bk
    d = C_HEADS * C_V
    qt_specs = [pl.BlockSpec((1, C_HEADS, C_QK, qs), functools.partial(lambda b, i, j, a: (b, 0, 0, i * n_sub + a), a=a))
                for a in range(n_sub)]
    return pl.pallas_call(
        functools.partial(_flash_kernel, n_sub=n_sub),
        grid=(batch, nq, nk),
        in_specs=qt_specs + [
                  pl.BlockSpec((1, C_HEADS, bk, C_QK), lambda b, i, j: (b, 0, j, 0)),
                  pl.BlockSpec((1, C_HEADS, C_VT, bk), lambda b, i, j: (b, 0, 0, j)),
                  pl.BlockSpec((bq, d), lambda b, i, j: (b * nq + i, C_Z // d))],
        out_specs=pl.BlockSpec((bq, d), lambda b, i, j: (b * nq + i, 0)),
        out_shape=jax.ShapeDtypeStruct((batch * seq, d), MXU_DTYPE),
        scratch_shapes=[pltpu.VMEM((C_HEADS, 8, bq), F32), pltpu.VMEM((C_HEADS, C_VT, bq), F32)],
        compiler_params=_cparams(("parallel", "parallel", "arbitrary"), 56),
        name="mla_flash",
    )(*([qt] * n_sub), k, vt, cin)


def _split_cols(w, sizes):
    idx = np.cumsum(np.array(sizes))[:-1].tolist()
    return jnp.split(w, idx, axis=-1)


def _rope_swap(w):
    half = w.shape[-1] // 2
    return jnp.concatenate([-w[..., half:], w[..., :half]], axis=-1)


def _pad_cols(w, n):
    return jnp.pad(w, [(0, 0)] * (w.ndim - 1) + [(0, n - w.shape[-1])])


def _prep_ab(w_in, alpha_w2, alpha_b, igate_b, fgate_b):
    sizes = (2 * A_HEADS * A_DK, A_HEADS * A_DV, 2 * A_HEADS, 2 * A_HEADS, A_HEADS * A_DV, A_HEADS * A_DV,
             B_HEADS * B_DK, B_HEADS * B_DK, B_HEADS * B_DV, 2 * B_GATE_RANK, B_HEADS * B_DV)
    a_qk, a_v, a_i, a_f, a_o, a_z, b_q, b_k, b_v, b_low, b_z = _split_cols(w_in, sizes)
    small = _pad_cols(jnp.concatenate([a_i, a_f, b_low], axis=-1), LANES)
    w = jnp.concatenate([a_qk, a_o, a_z, b_q, b_k, b_z, a_v, b_v, small], axis=-1).astype(MXU_DTYPE)
    hk = B_HEADS * B_DK
    w2 = jnp.zeros((LANES, 2 * hk), F32)
    w2 = w2.at[SM_LOW:SM_LOW + B_GATE_RANK, :hk].set(alpha_w2[0])
    w2 = w2.at[SM_LOW + B_GATE_RANK:SM_LOW + 2 * B_GATE_RANK, hk:].set(alpha_w2[1])
    ab = alpha_b.reshape(1, 2 * hk)
    gbias = _pad_cols(jnp.concatenate([igate_b.reshape(1, -1), fgate_b.reshape(1, -1)], axis=-1), LANES)
    return w, w2.astype(MXU_DTYPE), ab, gbias


def _prep_c(w_in, w_q_up, w_kv_up):
    q_lat, kv_lat, k_rope, z = _split_cols(w_in, (C_Q_LORA, C_KV_LORA, C_ROPE, C_HEADS * C_V))
    w_c = jnp.concatenate([z, kv_lat, _pad_cols(k_rope, LANES), _pad_cols(_rope_swap(k_rope), LANES), q_lat],
                          axis=-1).astype(MXU_DTYPE)
    wq = w_q_up.reshape(C_Q_LORA, C_HEADS, C_NOPE + C_ROPE)
    nope, rope = wq[..., :C_NOPE], wq[..., C_NOPE:]
    wq_t = jnp.transpose(jnp.concatenate([nope, rope, _rope_swap(rope)], axis=-1), (1, 2, 0))
    wkv = w_kv_up.reshape(C_KV_LORA, C_HEADS, C_NOPE + C_V)
    wk = jnp.transpose(wkv[..., :C_NOPE], (1, 0, 2))
    wv_t = jnp.transpose(wkv[..., C_NOPE:], (1, 2, 0))
    return w_c, wq_t.astype(MXU_DTYPE), wk.astype(MXU_DTYPE), wv_t.astype(MXU_DTYPE)


def _rope_tables(seq):
    inv = ROPE_BASE ** (-jnp.arange(0, C_ROPE, 2, dtype=F32) / C_ROPE)
    ang = jnp.arange(seq, dtype=F32)[:, None] * inv[None, :]
    cos, sin = jnp.cos(ang), jnp.sin(ang)
    cos2, sin2 = jnp.concatenate([cos, cos], axis=-1), jnp.concatenate([sin, sin], axis=-1)
    return _pad_cols(cos2, LANES), _pad_cols(sin2, LANES), cos2.T, sin2.T


def _trunk(x, mem, p, prep):
    batch, seq, d = x.shape
    x = x.reshape(batch * seq, d)
    mem = mem.reshape(batch * N_MEM, d)
    cos, sin, cos_t, sin_t = _rope_tables(seq)
    for l in range(DEPTH):
        j = l // 2
        if l % 2 == 0:
            w_in, w2, ab, gbias = prep['ab'][j]
            proj = _norm_matmul(x, p['norm_mix'][l], w_in, tm=TM_PROJ, tn=AB_TN, out_dtype=F32)
            qk, kt = _conv_silu(proj, p['ab_conv_w'][j], p['ab_conv_b'][j], seq=seq)
            h_f, h_b = _mlstm(qk, kt, proj, gbias, batch=batch, seq=seq)
            o_f, o_b = _gla(proj, w2, ab, batch=batch, seq=seq)
            x = _ab_post(h_f, h_b, o_f, o_b, proj, p['a_ogate_b'][j], p['a_head_norm'][j], p['b_head_norm'][j],
                         prep['ab_w_out'][j], x)
        else:
            w_c, wq_t, wk, wv_t = prep['c'][j]
            cin = _norm_matmul(x, p['norm_mix'][l], w_c, tm=TM_PROJ, tn=C_TN, out_dtype=F32)
            qt = _q_up(cin, p['c_q_norm'][j], wq_t, cos_t, sin_t, batch=batch, seq=seq)
            k, vt = _kv_up(cin, p['c_kv_norm'][j], wk, wv_t, cos, sin, batch=batch, seq=seq)
            og = _flash(qt, k, vt, cin, batch=batch, seq=seq)
            x = _matmul_residual(og, prep['c_w_out'][j], x, tm=TM_OUT)
        kv = _norm_matmul(mem, p['norm_mem'][l], prep['x_w_kv'][l], tm=N_MEM, tn=1024, out_dtype=MXU_DTYPE)
        x = _cross_attn(x, kv, p['norm_cross'][l], prep['x_w_q'][l], prep['x_w_o'][l], p['final_norm'],
                        seq=seq, final=(l == DEPTH - 1))
    return x.reshape(batch, seq, d)


def kernel(x_prompt, x_sample, mem_prompt, mem_sample, norm_mix, norm_cross, norm_mem, ab_w_in, ab_conv_w, ab_conv_b, a_igate_b, a_fgate_b, a_ogate_b, a_head_norm, b_alpha_w2, b_alpha_b, b_head_norm, ab_w_out, c_w_in, c_q_norm, c_kv_norm, c_w_q_up, c_w_kv_up, c_w_out, x_w_q, x_w_kv, x_w_o, final_norm):
    p = dict(norm_mix=norm_mix, norm_cross=norm_cross, norm_mem=norm_mem, ab_conv_w=ab_conv_w,
             ab_conv_b=ab_conv_b, a_ogate_b=a_ogate_b, a_head_norm=a_head_norm, b_head_norm=b_head_norm,
             c_q_norm=c_q_norm, c_kv_norm=c_kv_norm, final_norm=final_norm)
    n_even, n_odd = ab_w_in.shape[0], c_w_in.shape[0]
    prep = dict(
        ab=[_prep_ab(ab_w_in[j], b_alpha_w2[j], b_alpha_b[j], a_igate_b[j], a_fgate_b[j]) for j in range(n_even)],
        ab_w_out=[ab_w_out[j].astype(MXU_DTYPE) for j in range(n_even)],
        c=[_prep_c(c_w_in[j], c_w_q_up[j], c_w_kv_up[j]) for j in range(n_odd)],
        c_w_out=[c_w_out[j].astype(MXU_DTYPE) for j in range(n_odd)],
        x_w_q=[x_w_q[l].astype(MXU_DTYPE) for l in range(DEPTH)],
        x_w_kv=[x_w_kv[l].astype(MXU_DTYPE) for l in range(DEPTH)],
        x_w_o=[x_w_o[l].astype(MXU_DTYPE) for l in range(DEPTH)],
    )
    return (_trunk(x_prompt, mem_prompt, p, prep), _trunk(x_sample, mem_sample, p, prep))
```

```python
import functools

import numpy as np
import jax
import jax.numpy as jnp
from jax import lax
from jax.experimental import pallas as pl
from jax.experimental.pallas import tpu as pltpu

F32 = jnp.float32
MXU_DTYPE = jnp.bfloat16
EPS = 1e-6
HIGHEST = lax.Precision.HIGHEST

D_MODEL = 1024
DEPTH = 4
A_HEADS, A_DK, A_DV, A_CHUNK = 4, 128, 256, 128
B_HEADS, B_DK, B_DV, B_CHUNK = 4, 128, 256, 64
B_GATE_RANK, B_GATE_TAU = 16, 16.0
B_SUB = 16
C_HEADS, C_Q_LORA, C_KV_LORA, C_NOPE, C_ROPE, C_V = 8, 384, 256, 128, 64, 128
ROPE_BASE = 10000.0
X_HEADS, X_DH, N_MEM = 4, 256, 256

LANES = 128
V7X_VMEM_BYTES = 64 * 1024 * 1024

SLAB = 512
AB_QK, AB_O, AB_Z, AB_BQ, AB_BK, AB_BZ, AB_AV, AB_BV = 0, 2, 4, 6, 7, 8, 10, 12
AB_SLABS = 14
AB_SLABS_PER_STEP = 7
SM_I, SM_F, SM_LOW = 0, 8, 16
C_Z, C_KV, C_KR, C_KSW, C_QL = 0, 1024, 1280, 1408, 1536
C_COLS = 1920
C_TN = 640

TM_PROJ = 1024
TM_CONV = 512
TM_POST = 256
TM_CROSS = 512
TM_UP = 512
TM_OUT = 512
BQ, BK, QSUB = 1024, 512, 512


def _cparams(sem, vmem_mb=48):
    assert vmem_mb * 1024 * 1024 <= V7X_VMEM_BYTES
    return pltpu.CompilerParams(dimension_semantics=sem, vmem_limit_bytes=vmem_mb * 1024 * 1024)


def _sigmoid(x):
    return 1.0 / (1.0 + jnp.exp(-x))


def _silu(x):
    return x * _sigmoid(x)


def _log_sigmoid(x):
    return jnp.minimum(x, 0.0) - jnp.log1p(jnp.exp(-jnp.abs(x)))


def _dot(a, b):
    return jnp.dot(a, b, preferred_element_type=F32)


def _dot_nt(a, b):
    return lax.dot_general(a, b, (((1,), (1,)), ((), ())), preferred_element_type=F32)


def _dot_tn(a, b):
    return lax.dot_general(a, b, (((0,), (0,)), ((), ())), preferred_element_type=F32)


def _rms(x, g):
    ms = jnp.mean(x * x, axis=-1, keepdims=True)
    return x * lax.rsqrt(ms + EPS) * g


def _head_rms(x, g, n_heads):
    w = x.shape[-1] // n_heads
    parts = []
    for h in range(n_heads):
        xs = x[:, h * w:(h + 1) * w]
        ms = jnp.mean(xs * xs, axis=-1, keepdims=True)
        parts.append(xs * lax.rsqrt(ms + EPS))
    return jnp.concatenate(parts, axis=-1) * g


def _norm_matmul_kernel(x_ref, g_ref, w_ref, o_ref, h_ref):
    @pl.when(pl.program_id(1) == 0)
    def _():
        h_ref[...] = _rms(x_ref[...], g_ref[...]).astype(h_ref.dtype)

    o_ref[...] = _dot(h_ref[...], w_ref[...]).astype(o_ref.dtype)


def _norm_matmul(x, g, w, *, tm, tn, out_dtype, x_col_block=0):
    m = x.shape[0]
    k, n = w.shape
    tm = min(tm, m)
    return pl.pallas_call(
        _norm_matmul_kernel,
        grid=(m // tm, n // tn),
        in_specs=[pl.BlockSpec((tm, k), lambda i, j: (i, x_col_block)),
                  pl.BlockSpec((1, k), lambda i, j: (0, 0)),
                  pl.BlockSpec((k, tn), lambda i, j: (0, j))],
        out_specs=pl.BlockSpec((tm, tn), lambda i, j: (i, j)),
        out_shape=jax.ShapeDtypeStruct((m, n), out_dtype),
        scratch_shapes=[pltpu.VMEM((tm, k), MXU_DTYPE)],
        compiler_params=_cparams(("parallel", "arbitrary"), 56),
        name="norm_matmul",
    )(x, g.reshape(1, k), w)


def _ab_proj_kernel(x_ref, g_ref, w_ref, ws_ref, o_ref, os_ref, h_ref):
    @pl.when(pl.program_id(1) == 0)
    def _():
        h_ref[...] = _rms(x_ref[...], g_ref[...]).astype(h_ref.dtype)
        os_ref[...] = _dot(h_ref[...], ws_ref[...])

    for t in range(o_ref.shape[0]):
        o_ref[t] = _dot(h_ref[...], w_ref[:, t * SLAB:(t + 1) * SLAB]).astype(o_ref.dtype)


def _ab_proj(x, g, w, w_small):
    m, k = x.shape
    tm = min(TM_PROJ, m)
    ns = AB_SLABS_PER_STEP
    return pl.pallas_call(
        _ab_proj_kernel,
        grid=(m // tm, AB_SLABS // ns),
        in_specs=[pl.BlockSpec((tm, k), lambda i, j: (i, 0)),
                  pl.BlockSpec((1, k), lambda i, j: (0, 0)),
                  pl.BlockSpec((k, ns * SLAB), lambda i, j: (0, j)),
                  pl.BlockSpec((k, LANES), lambda i, j: (0, 0))],
        out_specs=[pl.BlockSpec((ns, tm, SLAB), lambda i, j: (j, i, 0)),
                   pl.BlockSpec((tm, LANES), lambda i, j: (i, 0))],
        out_shape=[jax.ShapeDtypeStruct((AB_SLABS, m, SLAB), MXU_DTYPE),
                   jax.ShapeDtypeStruct((m, LANES), F32)],
        scratch_shapes=[pltpu.VMEM((tm, k), MXU_DTYPE)],
        compiler_params=_cparams(("parallel", "arbitrary"), 56),
        name="ab_proj",
    )(x, g.reshape(1, k), w, w_small)


def _matmul_res_kernel(a_ref, w_ref, x_ref, o_ref):
    o_ref[...] = x_ref[...] + _dot(a_ref[...], w_ref[...])


def _matmul_residual(a, w, x, *, tm):
    m, k = a.shape
    n = w.shape[1]
    tm = min(tm, m)
    return pl.pallas_call(
        _matmul_res_kernel,
        grid=(m // tm,),
        in_specs=[pl.BlockSpec((tm, k), lambda i: (i, 0)),
                  pl.BlockSpec((k, n), lambda i: (0, 0)),
                  pl.BlockSpec((tm, n), lambda i: (i, 0))],
        out_specs=pl.BlockSpec((tm, n), lambda i: (i, 0)),
        out_shape=jax.ShapeDtypeStruct((m, n), F32),
        compiler_params=_cparams(("parallel",), 40),
        name="matmul_residual",
    )(a, w, x)


def _conv_kernel(x_ref, xp_ref, xn_ref, w_ref, b_ref, o_ref, kt_ref, *, tiles_per_seq):
    i = pl.program_id(0)
    ts = x_ref.shape[1]
    t_in_seq = i % tiles_per_seq
    rows = lax.broadcasted_iota(jnp.int32, (ts, SLAB), 0)
    for t in range(2):
        cs = slice(t * SLAB, (t + 1) * SLAB)
        x = x_ref[t].astype(F32)
        last = xp_ref.shape[1] - 1
        prev_row = jnp.where(t_in_seq == 0, 0.0, xp_ref[t, last:last + 1, :].astype(F32))
        next_row = jnp.where(t_in_seq == tiles_per_seq - 1, 0.0, xn_ref[t, 0:1, :].astype(F32))
        x_prev = jnp.where(rows == 0, prev_row, pltpu.roll(x, 1, axis=0))
        x_next = jnp.where(rows == ts - 1, next_row, pltpu.roll(x, ts - 1, axis=0))
        y = _silu(w_ref[0:1, cs] * x_prev + w_ref[1:2, cs] * x + w_ref[2:3, cs] * x_next + b_ref[:, cs])
        if t == 0:
            o_ref[:, cs] = (y * (A_DK ** -0.5)).astype(o_ref.dtype)
        else:
            o_ref[:, cs] = y.astype(o_ref.dtype)
            kt_ref[...] = y.T.astype(kt_ref.dtype)


def _conv_silu(proj, conv_w, conv_b, *, seq):
    m = proj.shape[1]
    c = conv_w.shape[1]
    assert c == 2 * SLAB and AB_QK == 0
    ts = min(TM_CONV, seq)
    nt = m // ts
    halo = 16
    sub = ts // halo
    return pl.pallas_call(
        functools.partial(_conv_kernel, tiles_per_seq=seq // ts),
        grid=(nt,),
        in_specs=[pl.BlockSpec((2, ts, SLAB), lambda i: (0, i, 0)),
                  pl.BlockSpec((2, halo, SLAB), lambda i: (0, jnp.maximum(i * sub - 1, 0), 0)),
                  pl.BlockSpec((2, halo, SLAB), lambda i: (0, jnp.minimum((i + 1) * sub, nt * sub - 1), 0)),
                  pl.BlockSpec((3, c), lambda i: (0, 0)),
                  pl.BlockSpec((1, c), lambda i: (0, 0))],
        out_specs=[pl.BlockSpec((ts, c), lambda i: (i, 0)),
                   pl.BlockSpec((c // 2, ts), lambda i: (0, i))],
        out_shape=[jax.ShapeDtypeStruct((m, c), MXU_DTYPE),
                   jax.ShapeDtypeStruct((c // 2, m), MXU_DTYPE)],
        compiler_params=_cparams(("parallel",), 32),
        name="conv_silu",
    )(proj, proj, proj, conv_w, conv_b.reshape(1, c))


def _mlstm_kernel(qk_f, kt_f, v_f, g_f, gn_f, qk_b, kt_b, v_b, g_b, gn_b, gbias_ref, h_f, h_b,
                  c_scr, n_scr, m_scr, gate_scr):
    L = A_CHUNK

    row = lax.broadcasted_iota(jnp.int32, (L, L), 0)
    col = lax.broadcasted_iota(jnp.int32, (L, L), 1)
    sub8 = lax.broadcasted_iota(jnp.int32, (8, L), 0)
    lane_row = lax.broadcasted_iota(jnp.int32, (1, LANES), 1)
    neg_inf = float("-inf")
    masks = (row >= col, row <= col)

    def gate_sums(gate_ref, d):
        gates = gate_ref[...] + gbias_ref[...]
        csum = jnp.dot(masks[d].astype(F32), _log_sigmoid(gates), precision=HIGHEST, preferred_element_type=F32)
        a_c = pltpu.roll(csum, LANES - (SM_F - SM_I), axis=1)
        e_c = gates - a_c
        run = e_c
        sh = 1
        while sh < L:
            pad = jnp.full((sh, LANES), neg_inf, F32)
            moved = (jnp.concatenate([pad, run[:L - sh]], axis=0) if d == 0
                     else jnp.concatenate([run[sh:], pad], axis=0))
            run = jnp.maximum(run, moved)
            sh *= 2
        return a_c, run, e_c.T

    def put_gate_sums(d, vals):
        for n, val in enumerate(vals):
            gate_scr[d, n] = val

    @pl.when(pl.program_id(1) == 0)
    def _():
        c_scr[...] = jnp.zeros_like(c_scr)
        n_scr[...] = jnp.zeros_like(n_scr)
        m_scr[...] = jnp.zeros_like(m_scr)
        put_gate_sums(0, gate_sums(g_f, 0))
        put_gate_sums(1, gate_sums(g_b, 1))

    m_all = m_scr[0:1, :]
    m_next = m_all
    dirs = ((qk_f, kt_f, v_f, g_f, h_f), (qk_b, kt_b, v_b, g_b, h_b))
    gate_next = [gate_sums(gn_f, 0), gate_sums(gn_b, 1)]
    pro = []
    for d in range(2):
        a_c, run, e_t = gate_scr[d, 0], gate_scr[d, 1], gate_scr[d, 2]
        m_c = jnp.maximum(run, m_all)
        w_inter_c = jnp.exp(m_all - m_c)
        inv_floor_c = jnp.exp(-(a_c + m_c))
        pro.append((masks[d], a_c, m_c, w_inter_c, inv_floor_c, e_t, m_c.T))

    for d, (qk_ref, kt_ref, v_ref, g_ref, h_ref) in enumerate(dirs):
        mask, a_c, m_c, w_inter_c, inv_floor_c, e_t, m_t_rows = pro[d]
        end = L - 1 if d == 0 else 0
        for h in range(A_HEADS):
            s = d * A_HEADS + h
            n_prev = n_scr[s]
            c_prev = c_scr[s]
            q = qk_ref[:, h * A_DK:(h + 1) * A_DK]
            k = qk_ref[:, (A_HEADS + h) * A_DK:(A_HEADS + h + 1) * A_DK]
            kt = kt_ref[h * A_DK:(h + 1) * A_DK, :]
            v0 = h * A_DV
            v = v_ref[v0 // SLAB, :, v0 % SLAB:v0 % SLAB + A_DV]
            e_r, m_r = e_t[s:s + 1, :], m_t_rows[s:s + 1, :]

            lhs = jnp.where(sub8 == 0, -m_r, jnp.where(sub8 == 1, 1.0, 0.0))
            rhs = jnp.where(sub8 == 0, 1.0, jnp.where(sub8 == 1, e_r, 0.0))
            x = lax.dot_general(lhs, rhs, (((0,), (0,)), ((), ())), precision=HIGHEST,
                                preferred_element_type=F32)
            sc = _dot_nt(q, k) * jnp.exp(jnp.where(mask, x, neg_inf))
            w_inter = w_inter_c[:, s:s + 1]
            num = _dot(sc.astype(MXU_DTYPE), v) + w_inter * _dot(q, c_prev.astype(MXU_DTYPE))
            qn = _dot_nt(q, n_prev.astype(MXU_DTYPE))[:, 0:1]
            den = jnp.sum(sc, axis=1, keepdims=True) + w_inter * qn
            hv = num / jnp.maximum(jnp.abs(den), inv_floor_c[:, s:s + 1])
            h_ref[:, h * A_DV:(h + 1) * A_DV] = hv.astype(h_ref.dtype)

            g = a_c[end:end + 1, s:s + 1]
            m_end = m_c[end:end + 1, s:s + 1]
            w = jnp.exp(e_r - m_end)
            decay = w_inter_c[end:end + 1, s:s + 1]
            kwt = (kt.astype(F32) * w).astype(MXU_DTYPE)
            c_scr[s] = decay * c_prev + _dot(kwt, v)
            n_scr[s] = decay * n_prev + _dot(jnp.broadcast_to(w, (8, L)).astype(MXU_DTYPE), k)
            m_next = jnp.where(lane_row == s, g + m_end, m_next)
    m_scr[...] = jnp.broadcast_to(m_next, m_scr.shape)
    put_gate_sums(0, gate_next[0])
    put_gate_sums(1, gate_next[1])


def _mlstm(qk, kt, proj, small, gbias, *, batch, seq):
    m = qk.shape[0]
    L = A_CHUNK
    nc = seq // L
    d_a = A_HEADS * A_DV
    v_slabs = d_a // SLAB
    v_blk = AB_AV // v_slabs
    assert L == LANES
    fwd = lambda b, c: b * nc + c
    bwd = lambda b, c: b * nc + (nc - 1 - c)
    nxt = lambda c: jnp.minimum(c + 1, nc - 1)
    out = jax.ShapeDtypeStruct((m, d_a), MXU_DTYPE)
    return pl.pallas_call(
        _mlstm_kernel,
        grid=(batch, nc),
        in_specs=[pl.BlockSpec((L, 2 * A_HEADS * A_DK), lambda b, c: (fwd(b, c), 0)),
                  pl.BlockSpec((A_HEADS * A_DK, L), lambda b, c: (0, fwd(b, c))),
                  pl.BlockSpec((v_slabs, L, SLAB), lambda b, c: (v_blk, fwd(b, c), 0)),
                  pl.BlockSpec((L, LANES), lambda b, c: (fwd(b, c), 0)),
                  pl.BlockSpec((L, LANES), lambda b, c: (fwd(b, nxt(c)), 0)),
                  pl.BlockSpec((L, 2 * A_HEADS * A_DK), lambda b, c: (bwd(b, c), 0)),
                  pl.BlockSpec((A_HEADS * A_DK, L), lambda b, c: (0, bwd(b, c))),
                  pl.BlockSpec((v_slabs, L, SLAB), lambda b, c: (v_blk, bwd(b, c), 0)),
                  pl.BlockSpec((L, LANES), lambda b, c: (bwd(b, c), 0)),
                  pl.BlockSpec((L, LANES), lambda b, c: (bwd(b, nxt(c)), 0)),
                  pl.BlockSpec((1, LANES), lambda b, c: (0, 0))],
        out_specs=[pl.BlockSpec((L, d_a), lambda b, c: (fwd(b, c), 0)),
                   pl.BlockSpec((L, d_a), lambda b, c: (bwd(b, c), 0))],
        out_shape=[out, out],
        scratch_shapes=[pltpu.VMEM((2 * A_HEADS, A_DK, A_DV), F32),
                        pltpu.VMEM((2 * A_HEADS, 8, A_DK), F32),
                        pltpu.VMEM((8, LANES), F32),
                        pltpu.VMEM((2, 3, L, LANES), F32)],
        compiler_params=_cparams(("parallel", "arbitrary"), 32),
        name="mlstm",
    )(qk, kt, proj, small, small, qk, kt, proj, small, small, gbias)


def _gla_kernel(q_f, k_f, v_f, g_f, gn_f, q_b, k_b, v_b, g_b, gn_b, w2_ref, ab_ref, o_f, o_b, st_scr, bc_scr):
    L, SB = B_CHUNK, B_SUB
    nsb = L // SB
    hk = B_HEADS * B_DK

    row = lax.broadcasted_iota(jnp.int32, (L, L), 0)
    col = lax.broadcasted_iota(jnp.int32, (L, L), 1)
    lane8 = lax.broadcasted_iota(jnp.int32, (8, L), 1)
    log2e = float(np.log2(np.e))
    dirs = ((q_f, k_f, v_f, g_f, o_f), (q_b, k_b, v_b, g_b, o_b))
    masks = (row >= col, row <= col)

    def decay_sums(gate_ref, d):
        pre = _dot(gate_ref[...].astype(MXU_DTYPE), w2_ref[:, d * hk:(d + 1) * hk]) + ab_ref[:, d * hk:(d + 1) * hk]
        log_a = _log_sigmoid(pre) / B_GATE_TAU
        return jnp.dot(masks[d].astype(F32), log_a, precision=HIGHEST, preferred_element_type=F32)

    @pl.when(pl.program_id(1) == 0)
    def _():
        st_scr[...] = jnp.zeros_like(st_scr)
        bc_scr[0] = decay_sums(g_f, 0)
        bc_scr[1] = decay_sums(g_b, 1)

    bc_all = [bc_scr[0], bc_scr[1]]
    bc_next = [decay_sums(gn_f, 0), decay_sums(gn_b, 1)]

    work = []
    for d, (q_ref, k_ref, v_ref, g_ref, o_ref) in enumerate(dirs):
        for h in range(B_HEADS):
            s = d * B_HEADS + h
            q = q_ref[0, :, h * B_DK:(h + 1) * B_DK].astype(F32) * (B_DK ** -0.5)
            k = k_ref[0, :, h * B_DK:(h + 1) * B_DK].astype(F32)
            b = bc_all[d][:, h * B_DK:(h + 1) * B_DK]
            st = st_scr[s]
            inter = _dot_nt((q * jnp.exp(b)).astype(MXU_DTYPE), st.astype(MXU_DTYPE))

            b2 = b * log2e
            b2_rows = [jnp.broadcast_to(b2[j:j + 1], (8, B_DK)) for j in range(L)]
            pieces, where_to = [], []
            for blk in range(nsb):
                lo = blk * SB
                for oc in range(SB // 8):
                    r0 = lo + 8 * oc
                    js = range(0, 8 * oc + 8) if d == 0 else range(8 * oc, SB)
                    for j in js:
                        pieces.append(q[r0:r0 + 8] * jnp.exp2(b2[r0:r0 + 8] - b2_rows[lo + j]))
                        where_to.append((r0, lo + j))
            diag = _dot_nt(jnp.concatenate(pieces, axis=0).astype(MXU_DTYPE), k.astype(MXU_DTYPE))

            offs = []
            for blk in range(nsb):
                lo, hi = blk * SB, (blk + 1) * SB
                if d == 0 and blk > 0:
                    ref = b[lo - 1:lo]
                    kt = k[:lo] * jnp.exp(ref - b[:lo])
                    kt = jnp.concatenate([kt, jnp.zeros((L - lo, B_DK), F32)], axis=0)
                elif d == 1 and blk < nsb - 1:
                    ref = b[hi:hi + 1]
                    kt = k[hi:] * jnp.exp(ref - b[hi:])
                    kt = jnp.concatenate([jnp.zeros((hi, B_DK), F32), kt], axis=0)
                else:
                    offs.append(jnp.zeros((SB, L), F32))
                    continue
                offs.append(_dot_nt((q[lo:hi] * jnp.exp(b[lo:hi] - ref)).astype(MXU_DTYPE), kt.astype(MXU_DTYPE)))
            work.append((d, h, s, k, b, st, inter, diag, where_to, offs))

    for d, h, s, k, b, st, inter, diag, where_to, offs in work:
        v = dirs[d][2][h * B_DV // SLAB, :, h * B_DV % SLAB:h * B_DV % SLAB + B_DV]
        att_rows = []
        for blk in range(nsb):
            for oc in range(SB // 8):
                r0 = blk * SB + 8 * oc
                a8 = offs[blk][8 * oc:8 * oc + 8]
                for n, (rr, cc) in enumerate(where_to):
                    if rr == r0:
                        a8 = jnp.where(lane8 == cc, diag[8 * n:8 * n + 8], a8)
                att_rows.append(a8)
        att = jnp.where(masks[d], jnp.concatenate(att_rows, axis=0), 0.0)
        o = inter + _dot(att.astype(MXU_DTYPE), v)
        dirs[d][4][:, h * B_DV:(h + 1) * B_DV] = o.astype(dirs[d][4].dtype)

    for d, h, s, k, b, st, inter, diag, where_to, offs in work:
        v = dirs[d][2][h * B_DV // SLAB, :, h * B_DV % SLAB:h * B_DV % SLAB + B_DV]
        g = b[L - 1:L, :] if d == 0 else b[0:1, :]
        kd = (k * jnp.exp(g - b)).astype(MXU_DTYPE)
        st_scr[s] = jnp.exp(g) * st + _dot_tn(v, kd)

    bc_scr[0] = bc_next[0]
    bc_scr[1] = bc_next[1]


def _gla(proj, small, w2, ab, *, batch, seq):
    m = proj.shape[1]
    L = B_CHUNK
    nc = seq // L
    hk = B_HEADS * B_DK
    d_b = B_HEADS * B_DV
    assert hk == SLAB
    v_slabs = d_b // SLAB
    v_blk = AB_BV // v_slabs
    fwd = lambda b, c: b * nc + c
    bwd = lambda b, c: b * nc + (nc - 1 - c)

    def specs(idx):
        nxt = lambda b, c: idx(b, jnp.minimum(c + 1, nc - 1))
        return [pl.BlockSpec((1, L, SLAB), lambda b, c: (AB_BQ, idx(b, c), 0)),
                pl.BlockSpec((1, L, SLAB), lambda b, c: (AB_BK, idx(b, c), 0)),
                pl.BlockSpec((v_slabs, L, SLAB), lambda b, c: (v_blk, idx(b, c), 0)),
                pl.BlockSpec((L, LANES), lambda b, c: (idx(b, c), 0)),
                pl.BlockSpec((L, LANES), lambda b, c: (nxt(b, c), 0))]

    out = jax.ShapeDtypeStruct((m, d_b), MXU_DTYPE)
    return pl.pallas_call(
        _gla_kernel,
        grid=(batch, nc),
        in_specs=specs(fwd) + specs(bwd) + [pl.BlockSpec((LANES, 2 * hk), lambda b, c: (0, 0)),
                                            pl.BlockSpec((1, 2 * hk), lambda b, c: (0, 0))],
        out_specs=[pl.BlockSpec((L, d_b), lambda b, c: (fwd(b, c), 0)),
                   pl.BlockSpec((L, d_b), lambda b, c: (bwd(b, c), 0))],
        out_shape=[out, out],
        scratch_shapes=[pltpu.VMEM((2 * B_HEADS, B_DV, B_DK), F32),
                        pltpu.VMEM((2, L, hk), F32)],
        compiler_params=_cparams(("parallel", "arbitrary"), 32),
        name="gla",
    )(proj, proj, proj, small, small, proj, proj, proj, small, small, w2, ab)


def _ab_post_kernel(hf_ref, hb_ref, ao_ref, az_ref, of_ref, ob_ref, bz_ref, ogb_ref, an_ref, bn_ref,
                    w_ref, x_ref, o_ref):
    d_a = A_HEADS * A_DV

    def slabs(ref):
        return jnp.concatenate([ref[t] for t in range(ref.shape[0])], axis=-1).astype(F32)

    h_sum = hf_ref[...].astype(F32) + hb_ref[...].astype(F32)
    out_a = _sigmoid(slabs(ao_ref) + ogb_ref[...]) * h_sum
    out_a = _head_rms(out_a, an_ref[...], A_HEADS) * _silu(slabs(az_ref))
    o_sum = of_ref[...].astype(F32) + ob_ref[...].astype(F32)
    out_b = _head_rms(o_sum, bn_ref[...], B_HEADS) * _silu(slabs(bz_ref))
    y = _dot(out_a.astype(MXU_DTYPE), w_ref[:d_a, :]) + _dot(out_b.astype(MXU_DTYPE), w_ref[d_a:, :])
    o_ref[...] = x_ref[...] + y


def _ab_post(h_f, h_b, o_f, o_b, proj, ogate_b, a_norm, b_norm, w_out, x):
    m, d = x.shape
    tm = min(TM_POST, m)
    row = lambda i: (i, 0)
    vec = pl.BlockSpec((1, d), lambda i: (0, 0))
    ns = d // SLAB
    blk = lambda slab: pl.BlockSpec((ns, tm, SLAB), lambda i: (slab // ns, i, 0))
    return pl.pallas_call(
        _ab_post_kernel,
        grid=(m // tm,),
        in_specs=[pl.BlockSpec((tm, d), row), pl.BlockSpec((tm, d), row), blk(AB_O), blk(AB_Z),
                  pl.BlockSpec((tm, d), row), pl.BlockSpec((tm, d), row), blk(AB_BZ),
                  vec, vec, vec, pl.BlockSpec(w_out.shape, lambda i: (0, 0)), pl.BlockSpec((tm, d), row)],
        out_specs=pl.BlockSpec((tm, d), row),
        out_shape=jax.ShapeDtypeStruct((m, d), F32),
        compiler_params=_cparams(("parallel",), 48),
        name="ab_post",
    )(h_f, h_b, proj, proj, o_f, o_b, proj, ogate_b.reshape(1, d), a_norm.reshape(1, d), b_norm.reshape(1, d),
      w_out, x)


def _cross_kernel(x_ref, g_ref, wq_ref, kv_ref, wo_ref, fin_ref, o_ref, *, final):
    x = x_ref[...]
    hn = _rms(x, g_ref[...]).astype(MXU_DTYPE)
    q = _dot(hn, wq_ref[...]).astype(MXU_DTYPE)
    d = X_HEADS * X_DH
    outs = []
    for h in range(X_HEADS):
        k = kv_ref[:, h * X_DH:(h + 1) * X_DH]
        v = kv_ref[:, d + h * X_DH:d + (h + 1) * X_DH]
        sc = _dot_nt(q[:, h * X_DH:(h + 1) * X_DH], k) * (X_DH ** -0.5)
        e = jnp.exp(sc - jnp.max(sc, axis=-1, keepdims=True))
        p = e / jnp.sum(e, axis=-1, keepdims=True)
        outs.append(_dot(p.astype(MXU_DTYPE), v))
    o = jnp.concatenate(outs, axis=-1).astype(MXU_DTYPE)
    y = x + _dot(o, wo_ref[...])
    if final:
        y = _rms(y, fin_ref[...])
    o_ref[...] = y


def _cross_attn(x, kv, g, w_q, w_o, fin, *, seq, final):
    m, d = x.shape
    tm = min(TM_CROSS, seq)
    tiles_per_seq = seq // tm
    full = lambda i: (0, 0)
    return pl.pallas_call(
        functools.partial(_cross_kernel, final=final),
        grid=(m // tm,),
        in_specs=[pl.BlockSpec((tm, d), lambda i: (i, 0)),
                  pl.BlockSpec((1, d), full),
                  pl.BlockSpec(w_q.shape, full),
                  pl.BlockSpec((N_MEM, kv.shape[1]), lambda i: (i // tiles_per_seq, 0)),
                  pl.BlockSpec(w_o.shape, full),
                  pl.BlockSpec((1, d), full)],
        out_specs=pl.BlockSpec((tm, d), lambda i: (i, 0)),
        out_shape=jax.ShapeDtypeStruct((m, d), F32),
        compiler_params=_cparams(("parallel",), 48),
        name="cross_attn",
    )(x, g.reshape(1, d), w_q, kv, w_o, fin.reshape(1, d))


C_QK = C_NOPE + LANES


C_EXP2 = (C_NOPE + C_ROPE) ** -0.5 * float(np.log2(np.e))
C_VT = C_V + 8


def _q_up_kernel(x_ref, g_ref, w_ref, cos_ref, sin_ref, o_ref):
    hn = _rms(x_ref[...], g_ref[...]).astype(MXU_DTYPE)
    pad = jnp.zeros((C_QK - C_NOPE - C_ROPE, hn.shape[0]), o_ref.dtype)
    for h in range(C_HEADS):
        rt = _dot_nt(w_ref[h], hn) * C_EXP2
        o_ref[0, h, :C_NOPE, :] = rt[:C_NOPE].astype(o_ref.dtype)
        rope = rt[C_NOPE:C_NOPE + C_ROPE] * cos_ref[...] + rt[C_NOPE + C_ROPE:] * sin_ref[...]
        o_ref[0, h, C_NOPE:C_NOPE + C_ROPE, :] = rope.astype(o_ref.dtype)
        o_ref[0, h, C_NOPE + C_ROPE:, :] = pad


def _q_up(cin, q_norm, w, cos_t, sin_t, *, batch, seq):
    tm = min(TM_UP, seq)
    tps = seq // tm
    return pl.pallas_call(
        _q_up_kernel,
        grid=(batch * tps,),
        in_specs=[pl.BlockSpec((tm, C_Q_LORA), lambda i: (i, C_QL // C_Q_LORA)),
                  pl.BlockSpec((1, C_Q_LORA), lambda i: (0, 0)),
                  pl.BlockSpec(w.shape, lambda i: (0, 0, 0)),
                  pl.BlockSpec((C_ROPE, tm), lambda i: (0, i % tps)),
                  pl.BlockSpec((C_ROPE, tm), lambda i: (0, i % tps))],
        out_specs=pl.BlockSpec((1, C_HEADS, C_QK, tm), lambda i: (i // tps, 0, 0, i % tps)),
        out_shape=jax.ShapeDtypeStruct((batch, C_HEADS, C_QK, seq), MXU_DTYPE),
        compiler_params=_cparams(("parallel",), 32),
        name="mla_q_up",
    )(cin, q_norm.reshape(1, C_Q_LORA), w, cos_t, sin_t)


def _kv_up_kernel(x_ref, g_ref, wk_ref, wvt_ref, kr_ref, ksw_ref, cos_ref, sin_ref, k_ref, vt_ref):
    hn = _rms(x_ref[...], g_ref[...]).astype(MXU_DTYPE)
    rope = (kr_ref[...] * cos_ref[...] + ksw_ref[...] * sin_ref[...]).astype(k_ref.dtype)
    tm = hn.shape[0]
    ones_rows = (lax.broadcasted_iota(jnp.int32, (C_VT - C_V, tm), 0) == 0).astype(vt_ref.dtype)
    for h in range(C_HEADS):
        k_ref[0, h, :, :C_NOPE] = _dot(hn, wk_ref[h]).astype(k_ref.dtype)
        k_ref[0, h, :, C_NOPE:] = rope
        vt_ref[0, h, :C_V, :] = _dot_nt(wvt_ref[h], hn).astype(vt_ref.dtype)
        vt_ref[0, h, C_V:, :] = ones_rows


def _kv_up(cin, kv_norm, wk, wvt, cos, sin, *, batch, seq):
    tm = min(TM_UP, seq)
    tps = seq // tm
    return pl.pallas_call(
        _kv_up_kernel,
        grid=(batch * tps,),
        in_specs=[pl.BlockSpec((tm, C_KV_LORA), lambda i: (i, C_KV // C_KV_LORA)),
                  pl.BlockSpec((1, C_KV_LORA), lambda i: (0, 0)),
                  pl.BlockSpec(wk.shape, lambda i: (0, 0, 0)),
                  pl.BlockSpec(wvt.shape, lambda i: (0, 0, 0)),
                  pl.BlockSpec((tm, LANES), lambda i: (i, C_KR // LANES)),
                  pl.BlockSpec((tm, LANES), lambda i: (i, C_KSW // LANES)),
                  pl.BlockSpec((tm, LANES), lambda i: (i % tps, 0)),
                  pl.BlockSpec((tm, LANES), lambda i: (i % tps, 0))],
        out_specs=[pl.BlockSpec((1, C_HEADS, tm, C_QK), lambda i: (i // tps, 0, i % tps, 0)),
                   pl.BlockSpec((1, C_HEADS, C_VT, tm), lambda i: (i // tps, 0, 0, i % tps))],
        out_shape=[jax.ShapeDtypeStruct((batch, C_HEADS, seq, C_QK), MXU_DTYPE),
                   jax.ShapeDtypeStruct((batch, C_HEADS, C_VT, seq), MXU_DTYPE)],
        compiler_params=_cparams(("parallel",), 32),
        name="mla_kv_up",
    )(cin, kv_norm.reshape(1, C_KV_LORA), wk, wvt, cin, cin, cos, sin)


def _flash_kernel(*refs, n_sub):
    qt_refs = refs[:n_sub]
    k_ref, vt_ref, z_ref, o_ref, m_scr, acc_scr = refs[n_sub:]
    j = pl.program_id(2)
    bk, qs = k_ref.shape[2], qt_refs[0].shape[3]

    @pl.when(j == 0)
    def _():
        m_scr[...] = jnp.full_like(m_scr, -jnp.inf)
        acc_scr[...] = jnp.zeros_like(acc_scr)

    units = [(h, a) for h in range(C_HEADS) for a in range(n_sub)]

    def scores(h, a):
        return _dot(k_ref[0, h], qt_refs[a][0, h]).reshape(bk // 8, 8, qs)

    st = scores(*units[0])
    for n, (h, a) in enumerate(units):
        q0 = a * qs
        st_next = scores(*units[n + 1]) if n + 1 < len(units) else None
        m_prev = m_scr[h, :, q0:q0 + qs]
        m_new = jnp.maximum(m_prev, jnp.max(jnp.max(st, axis=0), axis=0, keepdims=True))
        p = jnp.exp2(st - m_new[None]).reshape(bk, qs).astype(MXU_DTYPE)
        alpha = jnp.exp2(m_prev - m_new)
        pv = _dot(vt_ref[0, h], p)
        acc = acc_scr[h, :, q0:q0 + qs].reshape(C_VT // 8, 8, qs) * alpha[None]
        acc_scr[h, :, q0:q0 + qs] = acc.reshape(C_VT, qs) + pv
        m_scr[h, :, q0:q0 + qs] = m_new
        st = st_next

    @pl.when(j == pl.num_programs(2) - 1)
    def _():
        for h in range(C_HEADS):
            acc = acc_scr[h]
            o = (acc[:C_V] / acc[C_V:C_V + 1]).T
            zh = z_ref[:, h * C_V:(h + 1) * C_V]
            o_ref[:, h * C_V:(h + 1) * C_V] = (o * _silu(zh)).astype(o_ref.dtype)


def _flash(qt, k, vt, cin, *, batch, seq):
    bq, bk = min(BQ, seq), min(BK, seq)
    qs = min(QSUB, bq)
    n_sub = bq // qs
    nq, nk = seq // bq, seq // bk
    d = C_HEADS * C_V
    qt_specs = [pl.BlockSpec((1, C_HEADS, C_QK, qs), functools.partial(lambda b, i, j, a: (b, 0, 0, i * n_sub + a), a=a))
                for a in range(n_sub)]
    return pl.pallas_call(
        functools.partial(_flash_kernel, n_sub=n_sub),
        grid=(batch, nq, nk),
        in_specs=qt_specs + [
                  pl.BlockSpec((1, C_HEADS, bk, C_QK), lambda b, i, j: (b, 0, j, 0)),
                  pl.BlockSpec((1, C_HEADS, C_VT, bk), lambda b, i, j: (b, 0, 0, j)),
                  pl.BlockSpec((bq, d), lambda b, i, j: (b * nq + i, C_Z // d))],
        out_specs=pl.BlockSpec((bq, d), lambda b, i, j: (b * nq + i, 0)),
        out_shape=jax.ShapeDtypeStruct((batch * seq, d), MXU_DTYPE),
        scratch_shapes=[pltpu.VMEM((C_HEADS, 8, bq), F32), pltpu.VMEM((C_HEADS, C_VT, bq), F32)],
        compiler_params=_cparams(("parallel", "parallel", "arbitrary"), 56),
        name="mla_flash",
    )(*([qt] * n_sub), k, vt, cin)


def _split_cols(w, sizes):
    idx = np.cumsum(np.array(sizes))[:-1].tolist()
    return jnp.split(w, idx, axis=-1)


def _rope_swap(w):
    half = w.shape[-1] // 2
    return jnp.concatenate([-w[..., half:], w[..., :half]], axis=-1)


def _pad_cols(w, n):
    return jnp.pad(w, [(0, 0)] * (w.ndim - 1) + [(0, n - w.shape[-1])])


def _prep_ab(w_in, alpha_w2, alpha_b, igate_b, fgate_b):
    sizes = (2 * A_HEADS * A_DK, A_HEADS * A_DV, 2 * A_HEADS, 2 * A_HEADS, A_HEADS * A_DV, A_HEADS * A_DV,
             B_HEADS * B_DK, B_HEADS * B_DK, B_HEADS * B_DV, 2 * B_GATE_RANK, B_HEADS * B_DV)
    a_qk, a_v, a_i, a_f, a_o, a_z, b_q, b_k, b_v, b_low, b_z = _split_cols(w_in, sizes)
    w_small = _pad_cols(jnp.concatenate([a_i, a_f, b_low], axis=-1), LANES).astype(MXU_DTYPE)
    w = jnp.concatenate([a_qk, a_o, a_z, b_q, b_k, b_z, a_v, b_v], axis=-1).astype(MXU_DTYPE)
    assert w.shape[-1] == AB_SLABS * SLAB
    hk = B_HEADS * B_DK
    w2 = jnp.zeros((LANES, 2 * hk), F32)
    w2 = w2.at[SM_LOW:SM_LOW + B_GATE_RANK, :hk].set(alpha_w2[0])
    w2 = w2.at[SM_LOW + B_GATE_RANK:SM_LOW + 2 * B_GATE_RANK, hk:].set(alpha_w2[1])
    ab = alpha_b.reshape(1, 2 * hk)
    gbias = _pad_cols(jnp.concatenate([igate_b.reshape(1, -1), fgate_b.reshape(1, -1)], axis=-1), LANES)
    return w, w_small, w2.astype(MXU_DTYPE), ab, gbias


def _prep_c(w_in, w_q_up, w_kv_up):
    q_lat, kv_lat, k_rope, z = _split_cols(w_in, (C_Q_LORA, C_KV_LORA, C_ROPE, C_HEADS * C_V))
    w_c = jnp.concatenate([z, kv_lat, _pad_cols(k_rope, LANES), _pad_cols(_rope_swap(k_rope), LANES), q_lat],
                          axis=-1).astype(MXU_DTYPE)
    wq = w_q_up.reshape(C_Q_LORA, C_HEADS, C_NOPE + C_ROPE)
    nope, rope = wq[..., :C_NOPE], wq[..., C_NOPE:]
    wq_t = jnp.transpose(jnp.concatenate([nope, rope, _rope_swap(rope)], axis=-1), (1, 2, 0))
    wkv = w_kv_up.reshape(C_KV_LORA, C_HEADS, C_NOPE + C_V)
    wk = jnp.transpose(wkv[..., :C_NOPE], (1, 0, 2))
    wv_t = jnp.transpose(wkv[..., C_NOPE:], (1, 2, 0))
    return w_c, wq_t.astype(MXU_DTYPE), wk.astype(MXU_DTYPE), wv_t.astype(MXU_DTYPE)


def _rope_tables(seq):
    inv = ROPE_BASE ** (-jnp.arange(0, C_ROPE, 2, dtype=F32) / C_ROPE)
    ang = jnp.arange(seq, dtype=F32)[:, None] * inv[None, :]
    cos, sin = jnp.cos(ang), jnp.sin(ang)
    cos2, sin2 = jnp.concatenate([cos, cos], axis=-1), jnp.concatenate([sin, sin], axis=-1)
    return _pad_cols(cos2, LANES), _pad_cols(sin2, LANES), cos2.T, sin2.T


def _trunk(x, mem, p, prep):
    batch, seq, d = x.shape
    x = x.reshape(batch * seq, d)
    mem = mem.reshape(batch * N_MEM, d)
    cos, sin, cos_t, sin_t = _rope_tables(seq)
    for l in range(DEPTH):
        j = l // 2
        if l % 2 == 0:
            w_in, w_small, w2, ab, gbias = prep['ab'][j]
            proj, small = _ab_proj(x, p['norm_mix'][l], w_in, w_small)
            qk, kt = _conv_silu(proj, p['ab_conv_w'][j], p['ab_conv_b'][j], seq=seq)
            h_f, h_b = _mlstm(qk, kt, proj, small, gbias, batch=batch, seq=seq)
            o_f, o_b = _gla(proj, small, w2, ab, batch=batch, seq=seq)
            x = _ab_post(h_f, h_b, o_f, o_b, proj, p['a_ogate_b'][j], p['a_head_norm'][j], p['b_head_norm'][j],
                         prep['ab_w_out'][j], x)
        else:
            w_c, wq_t, wk, wv_t = prep['c'][j]
            cin = _norm_matmul(x, p['norm_mix'][l], w_c, tm=TM_PROJ, tn=C_TN, out_dtype=F32)
            qt = _q_up(cin, p['c_q_norm'][j], wq_t, cos_t, sin_t, batch=batch, seq=seq)
            k, vt = _kv_up(cin, p['c_kv_norm'][j], wk, wv_t, cos, sin, batch=batch, seq=seq)
            og = _flash(qt, k, vt, cin, batch=batch, seq=seq)
            x = _matmul_residual(og, prep['c_w_out'][j], x, tm=TM_OUT)
        kv = _norm_matmul(mem, p['norm_mem'][l], prep['x_w_kv'][l], tm=N_MEM, tn=1024, out_dtype=MXU_DTYPE)
        x = _cross_attn(x, kv, p['norm_cross'][l], prep['x_w_q'][l], prep['x_w_o'][l], p['final_norm'],
                        seq=seq, final=(l == DEPTH - 1))
    return x.reshape(batch, seq, d)


def kernel(x_prompt, x_sample, mem_prompt, mem_sample, norm_mix, norm_cross, norm_mem, ab_w_in, ab_conv_w, ab_conv_b, a_igate_b, a_fgate_b, a_ogate_b, a_head_norm, b_alpha_w2, b_alpha_b, b_head_norm, ab_w_out, c_w_in, c_q_norm, c_kv_norm, c_w_q_up, c_w_kv_up, c_w_out, x_w_q, x_w_kv, x_w_o, final_norm):
    p = dict(norm_mix=norm_mix, norm_cross=norm_cross, norm_mem=norm_mem, ab_conv_w=ab_conv_w,
             ab_conv_b=ab_conv_b, a_ogate_b=a_ogate_b, a_head_norm=a_head_norm, b_head_norm=b_head_norm,
             c_q_norm=c_q_norm, c_kv_norm=c_kv_norm, final_norm=final_norm)
    n_even, n_odd = ab_w_in.shape[0], c_w_in.shape[0]
    prep = dict(
        ab=[_prep_ab(ab_w_in[j], b_alpha_w2[j], b_alpha_b[j], a_igate_b[j], a_fgate_b[j]) for j in range(n_even)],
        ab_w_out=[ab_w_out[j].astype(MXU_DTYPE) for j in range(n_even)],
        c=[_prep_c(c_w_in[j], c_w_q_up[j], c_w_kv_up[j]) for j in range(n_odd)],
        c_w_out=[c_w_out[j].astype(MXU_DTYPE) for j in range(n_odd)],
        x_w_q=[x_w_q[l].astype(MXU_DTYPE) for l in range(DEPTH)],
        x_w_kv=[x_w_kv[l].astype(MXU_DTYPE) for l in range(DEPTH)],
        x_w_o=[x_w_o[l].astype(MXU_DTYPE) for l in range(DEPTH)],
    )
    return (_trunk(x_prompt, mem_prompt, p, prep), _trunk(x_sample, mem_sample, p, prep))
```

```python
import functools

import numpy as np
import jax
import jax.numpy as jnp
from jax import lax
from jax.experimental import pallas as pl
from jax.experimental.pallas import tpu as pltpu

F32 = jnp.float32
MXU_DTYPE = jnp.bfloat16
EPS = 1e-6
HIGHEST = lax.Precision.HIGHEST

D_MODEL = 1024
DEPTH = 4
A_HEADS, A_DK, A_DV, A_CHUNK = 4, 128, 256, 128
B_HEADS, B_DK, B_DV, B_CHUNK = 4, 128, 256, 64
B_GATE_RANK, B_GATE_TAU = 16, 16.0
B_SUB = 16
B_CHUNKS_PER_STEP = 2
C_HEADS, C_Q_LORA, C_KV_LORA, C_NOPE, C_ROPE, C_V = 8, 384, 256, 128, 64, 128
ROPE_BASE = 10000.0
X_HEADS, X_DH, N_MEM = 4, 256, 256

LANES = 128
V7X_VMEM_BYTES = 64 * 1024 * 1024

SLAB = 512
AB_QK, AB_O, AB_Z, AB_BQ, AB_BK, AB_BZ, AB_AV, AB_BV = 0, 2, 4, 6, 7, 8, 10, 12
AB_SLABS = 14
AB_SLABS_PER_STEP = 7
SM_I, SM_F, SM_LOW = 0, 8, 16
C_Z, C_KV, C_KR, C_KSW, C_QL = 0, 1024, 1280, 1408, 1536
C_COLS = 1920
C_TN = 640

TM_PROJ = 1024
TM_CONV = 512
TM_POST = 512
TM_CROSS = 512
TM_UP = 512
TM_OUT = 512
BQ, BK, QSUB = 1024, 512, 512


def _cparams(sem, vmem_mb=48):
    assert vmem_mb * 1024 * 1024 <= V7X_VMEM_BYTES
    return pltpu.CompilerParams(dimension_semantics=sem, vmem_limit_bytes=vmem_mb * 1024 * 1024)


def _sigmoid(x):
    return 1.0 / (1.0 + jnp.exp(-x))


def _silu(x):
    return x * _sigmoid(x)


def _log_sigmoid(x):
    return jnp.minimum(x, 0.0) - jnp.log1p(jnp.exp(-jnp.abs(x)))


def _dot(a, b):
    return jnp.dot(a, b, preferred_element_type=F32)


def _dot_nt(a, b):
    return lax.dot_general(a, b, (((1,), (1,)), ((), ())), preferred_element_type=F32)


def _dot_tn(a, b):
    return lax.dot_general(a, b, (((0,), (0,)), ((), ())), preferred_element_type=F32)


def _rms(x, g):
    ms = jnp.mean(x * x, axis=-1, keepdims=True)
    return x * lax.rsqrt(ms + EPS) * g


def _head_rms(x, g, n_heads):
    w = x.shape[-1] // n_heads
    parts = []
    for h in range(n_heads):
        xs = x[:, h * w:(h + 1) * w]
        ms = jnp.mean(xs * xs, axis=-1, keepdims=True)
        parts.append(xs * lax.rsqrt(ms + EPS))
    return jnp.concatenate(parts, axis=-1) * g


def _norm_matmul_kernel(x_ref, g_ref, w_ref, o_ref, h_ref):
    @pl.when(pl.program_id(1) == 0)
    def _():
        h_ref[...] = _rms(x_ref[...], g_ref[...]).astype(h_ref.dtype)

    o_ref[...] = _dot(h_ref[...], w_ref[...]).astype(o_ref.dtype)


def _norm_matmul(x, g, w, *, tm, tn, out_dtype, x_col_block=0):
    m = x.shape[0]
    k, n = w.shape
    tm = min(tm, m)
    return pl.pallas_call(
        _norm_matmul_kernel,
        grid=(m // tm, n // tn),
        in_specs=[pl.BlockSpec((tm, k), lambda i, j: (i, x_col_block)),
                  pl.BlockSpec((1, k), lambda i, j: (0, 0)),
                  pl.BlockSpec((k, tn), lambda i, j: (0, j))],
        out_specs=pl.BlockSpec((tm, tn), lambda i, j: (i, j)),
        out_shape=jax.ShapeDtypeStruct((m, n), out_dtype),
        scratch_shapes=[pltpu.VMEM((tm, k), MXU_DTYPE)],
        compiler_params=_cparams(("parallel", "arbitrary"), 56),
        name="norm_matmul",
    )(x, g.reshape(1, k), w)


def _ab_proj_kernel(x_ref, g_ref, w_ref, ws_ref, o_ref, os_ref, h_ref):
    @pl.when(pl.program_id(1) == 0)
    def _():
        h_ref[...] = _rms(x_ref[...], g_ref[...]).astype(h_ref.dtype)
        os_ref[...] = _dot(h_ref[...], ws_ref[...])

    for t in range(o_ref.shape[0]):
        o_ref[t] = _dot(h_ref[...], w_ref[:, t * SLAB:(t + 1) * SLAB]).astype(o_ref.dtype)


def _ab_proj(x, g, w, w_small):
    m, k = x.shape
    tm = min(TM_PROJ, m)
    ns = AB_SLABS_PER_STEP
    return pl.pallas_call(
        _ab_proj_kernel,
        grid=(m // tm, AB_SLABS // ns),
        in_specs=[pl.BlockSpec((tm, k), lambda i, j: (i, 0)),
                  pl.BlockSpec((1, k), lambda i, j: (0, 0)),
                  pl.BlockSpec((k, ns * SLAB), lambda i, j: (0, j)),
                  pl.BlockSpec((k, LANES), lambda i, j: (0, 0))],
        out_specs=[pl.BlockSpec((ns, tm, SLAB), lambda i, j: (j, i, 0)),
                   pl.BlockSpec((tm, LANES), lambda i, j: (i, 0))],
        out_shape=[jax.ShapeDtypeStruct((AB_SLABS, m, SLAB), MXU_DTYPE),
                   jax.ShapeDtypeStruct((m, LANES), F32)],
        scratch_shapes=[pltpu.VMEM((tm, k), MXU_DTYPE)],
        compiler_params=_cparams(("parallel", "arbitrary"), 56),
        name="ab_proj",
    )(x, g.reshape(1, k), w, w_small)


def _matmul_res_kernel(a_ref, w_ref, x_ref, o_ref):
    o_ref[...] = x_ref[...] + _dot(a_ref[...], w_ref[...])


def _matmul_residual(a, w, x, *, tm):
    m, k = a.shape
    n = w.shape[1]
    tm = min(tm, m)
    return pl.pallas_call(
        _matmul_res_kernel,
        grid=(m // tm,),
        in_specs=[pl.BlockSpec((tm, k), lambda i: (i, 0)),
                  pl.BlockSpec((k, n), lambda i: (0, 0)),
                  pl.BlockSpec((tm, n), lambda i: (i, 0))],
        out_specs=pl.BlockSpec((tm, n), lambda i: (i, 0)),
        out_shape=jax.ShapeDtypeStruct((m, n), F32),
        compiler_params=_cparams(("parallel",), 40),
        name="matmul_residual",
    )(a, w, x)


def _conv_kernel(x_ref, xp_ref, xn_ref, w_ref, b_ref, o_ref, kt_ref, *, tiles_per_seq):
    i = pl.program_id(0)
    ts = x_ref.shape[1]
    t_in_seq = i % tiles_per_seq
    rows = lax.broadcasted_iota(jnp.int32, (ts, SLAB), 0)
    for t in range(2):
        cs = slice(t * SLAB, (t + 1) * SLAB)
        x = x_ref[t].astype(F32)
        last = xp_ref.shape[1] - 1
        prev_row = jnp.where(t_in_seq == 0, 0.0, xp_ref[t, last:last + 1, :].astype(F32))
        next_row = jnp.where(t_in_seq == tiles_per_seq - 1, 0.0, xn_ref[t, 0:1, :].astype(F32))
        x_prev = jnp.where(rows == 0, prev_row, pltpu.roll(x, 1, axis=0))
        x_next = jnp.where(rows == ts - 1, next_row, pltpu.roll(x, ts - 1, axis=0))
        y = _silu(w_ref[0:1, cs] * x_prev + w_ref[1:2, cs] * x + w_ref[2:3, cs] * x_next + b_ref[:, cs])
        if t == 0:
            o_ref[:, cs] = (y * (A_DK ** -0.5)).astype(o_ref.dtype)
        else:
            o_ref[:, cs] = y.astype(o_ref.dtype)
            kt_ref[...] = y.T.astype(kt_ref.dtype)


def _conv_silu(proj, conv_w, conv_b, *, seq):
    m = proj.shape[1]
    c = conv_w.shape[1]
    assert c == 2 * SLAB and AB_QK == 0
    ts = min(TM_CONV, seq)
    nt = m // ts
    halo = 16
    sub = ts // halo
    return pl.pallas_call(
        functools.partial(_conv_kernel, tiles_per_seq=seq // ts),
        grid=(nt,),
        in_specs=[pl.BlockSpec((2, ts, SLAB), lambda i: (0, i, 0)),
                  pl.BlockSpec((2, halo, SLAB), lambda i: (0, jnp.maximum(i * sub - 1, 0), 0)),
                  pl.BlockSpec((2, halo, SLAB), lambda i: (0, jnp.minimum((i + 1) * sub, nt * sub - 1), 0)),
                  pl.BlockSpec((3, c), lambda i: (0, 0)),
                  pl.BlockSpec((1, c), lambda i: (0, 0))],
        out_specs=[pl.BlockSpec((ts, c), lambda i: (i, 0)),
                   pl.BlockSpec((c // 2, ts), lambda i: (0, i))],
        out_shape=[jax.ShapeDtypeStruct((m, c), MXU_DTYPE),
                   jax.ShapeDtypeStruct((c // 2, m), MXU_DTYPE)],
        compiler_params=_cparams(("parallel",), 32),
        name="conv_silu",
    )(proj, proj, proj, conv_w, conv_b.reshape(1, c))


def _mlstm_kernel(qk_f, kt_f, v_f, g_f, gn_f, qk_b, kt_b, v_b, g_b, gn_b, gbias_ref, h_f, h_b,
                  c_scr, n_scr, m_scr, gate_scr):
    L = A_CHUNK

    row = lax.broadcasted_iota(jnp.int32, (L, L), 0)
    col = lax.broadcasted_iota(jnp.int32, (L, L), 1)
    sub8 = lax.broadcasted_iota(jnp.int32, (8, L), 0)
    lane_row = lax.broadcasted_iota(jnp.int32, (1, LANES), 1)
    neg_inf = float("-inf")
    masks = (row >= col, row <= col)

    def gate_sums(gate_ref, d):
        gates = gate_ref[...] + gbias_ref[...]
        csum = jnp.dot(masks[d].astype(F32), _log_sigmoid(gates), precision=HIGHEST, preferred_element_type=F32)
        a_c = pltpu.roll(csum, LANES - (SM_F - SM_I), axis=1)
        e_c = gates - a_c
        run = e_c
        sh = 1
        while sh < L:
            pad = jnp.full((sh, LANES), neg_inf, F32)
            moved = (jnp.concatenate([pad, run[:L - sh]], axis=0) if d == 0
                     else jnp.concatenate([run[sh:], pad], axis=0))
            run = jnp.maximum(run, moved)
            sh *= 2
        return a_c, run, e_c.T

    def put_gate_sums(d, vals):
        for n, val in enumerate(vals):
            gate_scr[d, n] = val

    @pl.when(pl.program_id(1) == 0)
    def _():
        c_scr[...] = jnp.zeros_like(c_scr)
        n_scr[...] = jnp.zeros_like(n_scr)
        m_scr[...] = jnp.zeros_like(m_scr)
        put_gate_sums(0, gate_sums(g_f, 0))
        put_gate_sums(1, gate_sums(g_b, 1))

    m_all = m_scr[0:1, :]
    m_next = m_all
    dirs = ((qk_f, kt_f, v_f, g_f, h_f), (qk_b, kt_b, v_b, g_b, h_b))
    gate_next = [gate_sums(gn_f, 0), gate_sums(gn_b, 1)]
    pro = []
    for d in range(2):
        a_c, run, e_t = gate_scr[d, 0], gate_scr[d, 1], gate_scr[d, 2]
        m_c = jnp.maximum(run, m_all)
        w_inter_c = jnp.exp(m_all - m_c)
        inv_floor_c = jnp.exp(-(a_c + m_c))
        pro.append((masks[d], a_c, m_c, w_inter_c, inv_floor_c, e_t, m_c.T))

    for d, (qk_ref, kt_ref, v_ref, g_ref, h_ref) in enumerate(dirs):
        mask, a_c, m_c, w_inter_c, inv_floor_c, e_t, m_t_rows = pro[d]
        end = L - 1 if d == 0 else 0
        for h in range(A_HEADS):
            s = d * A_HEADS + h
            n_prev = n_scr[s]
            c_prev = c_scr[s]
            q = qk_ref[:, h * A_DK:(h + 1) * A_DK]
            k = qk_ref[:, (A_HEADS + h) * A_DK:(A_HEADS + h + 1) * A_DK]
            kt = kt_ref[h * A_DK:(h + 1) * A_DK, :]
            v0 = h * A_DV
            v = v_ref[v0 // SLAB, :, v0 % SLAB:v0 % SLAB + A_DV]
            e_r, m_r = e_t[s:s + 1, :], m_t_rows[s:s + 1, :]

            lhs = jnp.where(sub8 == 0, -m_r, jnp.where(sub8 == 1, 1.0, 0.0))
            rhs = jnp.where(sub8 == 0, 1.0, jnp.where(sub8 == 1, e_r, 0.0))
            x = lax.dot_general(lhs, rhs, (((0,), (0,)), ((), ())), precision=HIGHEST,
                                preferred_element_type=F32)
            sc = _dot_nt(q, k) * jnp.exp(jnp.where(mask, x, neg_inf))
            w_inter = w_inter_c[:, s:s + 1]
            num = _dot(sc.astype(MXU_DTYPE), v) + w_inter * _dot(q, c_prev.astype(MXU_DTYPE))
            qn = _dot_nt(q, n_prev.astype(MXU_DTYPE))[:, 0:1]
            den = jnp.sum(sc, axis=1, keepdims=True) + w_inter * qn
            hv = num / jnp.maximum(jnp.abs(den), inv_floor_c[:, s:s + 1])
            h_ref[:, h * A_DV:(h + 1) * A_DV] = hv.astype(h_ref.dtype)

            g = a_c[end:end + 1, s:s + 1]
            m_end = m_c[end:end + 1, s:s + 1]
            w = jnp.exp(e_r - m_end)
            decay = w_inter_c[end:end + 1, s:s + 1]
            kwt = (kt.astype(F32) * w).astype(MXU_DTYPE)
            c_scr[s] = decay * c_prev + _dot(kwt, v)
            n_scr[s] = decay * n_prev + _dot(jnp.broadcast_to(w, (8, L)).astype(MXU_DTYPE), k)
            m_next = jnp.where(lane_row == s, g + m_end, m_next)
    m_scr[...] = jnp.broadcast_to(m_next, m_scr.shape)
    put_gate_sums(0, gate_next[0])
    put_gate_sums(1, gate_next[1])


def _mlstm(qk, kt, proj, small, gbias, *, batch, seq):
    m = qk.shape[0]
    L = A_CHUNK
    nc = seq // L
    d_a = A_HEADS * A_DV
    v_slabs = d_a // SLAB
    v_blk = AB_AV // v_slabs
    assert L == LANES
    fwd = lambda b, c: b * nc + c
    bwd = lambda b, c: b * nc + (nc - 1 - c)
    nxt = lambda c: jnp.minimum(c + 1, nc - 1)
    out = jax.ShapeDtypeStruct((m, d_a), MXU_DTYPE)
    return pl.pallas_call(
        _mlstm_kernel,
        grid=(batch, nc),
        in_specs=[pl.BlockSpec((L, 2 * A_HEADS * A_DK), lambda b, c: (fwd(b, c), 0)),
                  pl.BlockSpec((A_HEADS * A_DK, L), lambda b, c: (0, fwd(b, c))),
                  pl.BlockSpec((v_slabs, L, SLAB), lambda b, c: (v_blk, fwd(b, c), 0)),
                  pl.BlockSpec((L, LANES), lambda b, c: (fwd(b, c), 0)),
                  pl.BlockSpec((L, LANES), lambda b, c: (fwd(b, nxt(c)), 0)),
                  pl.BlockSpec((L, 2 * A_HEADS * A_DK), lambda b, c: (bwd(b, c), 0)),
                  pl.BlockSpec((A_HEADS * A_DK, L), lambda b, c: (0, bwd(b, c))),
                  pl.BlockSpec((v_slabs, L, SLAB), lambda b, c: (v_blk, bwd(b, c), 0)),
                  pl.BlockSpec((L, LANES), lambda b, c: (bwd(b, c), 0)),
                  pl.BlockSpec((L, LANES), lambda b, c: (bwd(b, nxt(c)), 0)),
                  pl.BlockSpec((1, LANES), lambda b, c: (0, 0))],
        out_specs=[pl.BlockSpec((L, d_a), lambda b, c: (fwd(b, c), 0)),
                   pl.BlockSpec((L, d_a), lambda b, c: (bwd(b, c), 0))],
        out_shape=[out, out],
        scratch_shapes=[pltpu.VMEM((2 * A_HEADS, A_DK, A_DV), F32),
                        pltpu.VMEM((2 * A_HEADS, 8, A_DK), F32),
                        pltpu.VMEM((8, LANES), F32),
                        pltpu.VMEM((2, 3, L, LANES), F32)],
        compiler_params=_cparams(("parallel", "arbitrary"), 32),
        name="mlstm",
    )(qk, kt, proj, small, small, qk, kt, proj, small, small, gbias)


def _gla_kernel(qk_f, v_f, g_f, gn_f, qk_b, v_b, g_b, gn_b, w2_ref, ab_ref, o_f, o_b, st_scr, bc_scr):
    L, SB = B_CHUNK, B_SUB
    nsb = L // SB
    hk = B_HEADS * B_DK
    rows_per_step = g_f.shape[0]
    n_chunks = rows_per_step // L

    row = lax.broadcasted_iota(jnp.int32, (L, L), 0)
    col = lax.broadcasted_iota(jnp.int32, (L, L), 1)
    lane8 = lax.broadcasted_iota(jnp.int32, (8, L), 1)
    log2e = float(np.log2(np.e))
    dirs = ((qk_f, v_f, o_f), (qk_b, v_b, o_b))
    masks = (row >= col, row <= col)
    row_s = lax.broadcasted_iota(jnp.int32, (rows_per_step, rows_per_step), 0)
    col_s = lax.broadcasted_iota(jnp.int32, (rows_per_step, rows_per_step), 1)
    same_chunk = (row_s // L) == (col_s // L)
    sum_masks = (same_chunk & (row_s >= col_s), same_chunk & (row_s <= col_s))

    def decay_sums(gate_ref, d):
        pre = _dot(gate_ref[...].astype(MXU_DTYPE), w2_ref[:, d * hk:(d + 1) * hk]) + ab_ref[:, d * hk:(d + 1) * hk]
        log_a = _log_sigmoid(pre) / B_GATE_TAU
        return jnp.dot(sum_masks[d].astype(F32), log_a, precision=HIGHEST, preferred_element_type=F32)

    @pl.when(pl.program_id(1) == 0)
    def _():
        st_scr[...] = jnp.zeros_like(st_scr)
        bc_scr[0] = decay_sums(g_f, 0)
        bc_scr[1] = decay_sums(g_b, 1)

    bc_all = [bc_scr[0], bc_scr[1]]
    bc_next = [decay_sums(gn_f, 0), decay_sums(gn_b, 1)]

    def chunk(r0, states):
        work = []
        for d, (qk_ref, v_ref, o_ref) in enumerate(dirs):
            rs = slice(r0[d], r0[d] + L)
            for h in range(B_HEADS):
                s = d * B_HEADS + h
                q = qk_ref[0, rs, h * B_DK:(h + 1) * B_DK].astype(F32) * (B_DK ** -0.5)
                k = qk_ref[1, rs, h * B_DK:(h + 1) * B_DK].astype(F32)
                b = bc_all[d][rs, h * B_DK:(h + 1) * B_DK]
                st = states[s]
                inter = _dot_nt((q * jnp.exp(b)).astype(MXU_DTYPE), st.astype(MXU_DTYPE))

                b2 = b * log2e
                b2_rows = [jnp.broadcast_to(b2[j:j + 1], (8, B_DK)) for j in range(L)]
                pieces, where_to = [], []
                for blk in range(nsb):
                    lo = blk * SB
                    for oc in range(SB // 8):
                        t0 = lo + 8 * oc
                        js = range(0, 8 * oc + 8) if d == 0 else range(8 * oc, SB)
                        for j in js:
                            pieces.append(q[t0:t0 + 8] * jnp.exp2(b2[t0:t0 + 8] - b2_rows[lo + j]))
                            where_to.append((t0, lo + j))
                diag = _dot_nt(jnp.concatenate(pieces, axis=0).astype(MXU_DTYPE), k.astype(MXU_DTYPE))

                offs = []
                for blk in range(nsb):
                    lo, hi = blk * SB, (blk + 1) * SB
                    if d == 0 and blk > 0:
                        ref = b[lo - 1:lo]
                        kt = k[:lo] * jnp.exp(ref - b[:lo])
                        kt = jnp.concatenate([kt, jnp.zeros((L - lo, B_DK), F32)], axis=0)
                    elif d == 1 and blk < nsb - 1:
                        ref = b[hi:hi + 1]
                        kt = k[hi:] * jnp.exp(ref - b[hi:])
                        kt = jnp.concatenate([jnp.zeros((hi, B_DK), F32), kt], axis=0)
                    else:
                        offs.append(jnp.zeros((SB, L), F32))
                        continue
                    offs.append(_dot_nt((q[lo:hi] * jnp.exp(b[lo:hi] - ref)).astype(MXU_DTYPE),
                                        kt.astype(MXU_DTYPE)))
                work.append((d, h, s, rs, k, b, st, inter, diag, where_to, offs))

        for d, h, s, rs, k, b, st, inter, diag, where_to, offs in work:
            v = dirs[d][1][h * B_DV // SLAB, rs, h * B_DV % SLAB:h * B_DV % SLAB + B_DV]
            att_rows = []
            for blk in range(nsb):
                for oc in range(SB // 8):
                    t0 = blk * SB + 8 * oc
                    a8 = offs[blk][8 * oc:8 * oc + 8]
                    for n, (rr, cc) in enumerate(where_to):
                        if rr == t0:
                            a8 = jnp.where(lane8 == cc, diag[8 * n:8 * n + 8], a8)
                    att_rows.append(a8)
            att = jnp.where(masks[d], jnp.concatenate(att_rows, axis=0), 0.0)
            o = inter + _dot(att.astype(MXU_DTYPE), v)
            dirs[d][2][rs, h * B_DV:(h + 1) * B_DV] = o.astype(dirs[d][2].dtype)

        new_states = list(states)
        for d, h, s, rs, k, b, st, inter, diag, where_to, offs in work:
            v = dirs[d][1][h * B_DV // SLAB, rs, h * B_DV % SLAB:h * B_DV % SLAB + B_DV]
            g = b[L - 1:L, :] if d == 0 else b[0:1, :]
            kd = (k * jnp.exp(g - b)).astype(MXU_DTYPE)
            new_states[s] = jnp.exp(g) * st + _dot_tn(v, kd)
        return new_states

    states = [st_scr[s] for s in range(2 * B_HEADS)]
    for u in range(n_chunks):
        states = chunk((u * L, (n_chunks - 1 - u) * L), states)
    for s in range(2 * B_HEADS):
        st_scr[s] = states[s]
    bc_scr[0] = bc_next[0]
    bc_scr[1] = bc_next[1]


def _gla(proj, small, w2, ab, *, batch, seq):
    m = proj.shape[1]
    rows = min(B_CHUNKS_PER_STEP * B_CHUNK, seq)
    ns = seq // rows
    hk = B_HEADS * B_DK
    d_b = B_HEADS * B_DV
    assert hk == SLAB and AB_BK == AB_BQ + 1 and AB_BQ % 2 == 0
    v_slabs = d_b // SLAB
    fwd = lambda b, c: b * ns + c
    bwd = lambda b, c: b * ns + (ns - 1 - c)

    def specs(idx):
        nxt = lambda b, c: idx(b, jnp.minimum(c + 1, ns - 1))
        return [pl.BlockSpec((2, rows, SLAB), lambda b, c: (AB_BQ // 2, idx(b, c), 0)),
                pl.BlockSpec((v_slabs, rows, SLAB), lambda b, c: (AB_BV // v_slabs, idx(b, c), 0)),
                pl.BlockSpec((rows, LANES), lambda b, c: (idx(b, c), 0)),
                pl.BlockSpec((rows, LANES), lambda b, c: (nxt(b, c), 0))]

    out = jax.ShapeDtypeStruct((m, d_b), MXU_DTYPE)
    return pl.pallas_call(
        _gla_kernel,
        grid=(batch, ns),
        in_specs=specs(fwd) + specs(bwd) + [pl.BlockSpec((LANES, 2 * hk), lambda b, c: (0, 0)),
                                            pl.BlockSpec((1, 2 * hk), lambda b, c: (0, 0))],
        out_specs=[pl.BlockSpec((rows, d_b), lambda b, c: (fwd(b, c), 0)),
                   pl.BlockSpec((rows, d_b), lambda b, c: (bwd(b, c), 0))],
        out_shape=[out, out],
        scratch_shapes=[pltpu.VMEM((2 * B_HEADS, B_DV, B_DK), F32),
                        pltpu.VMEM((2, rows, hk), F32)],
        compiler_params=_cparams(("parallel", "arbitrary"), 32),
        name="gla",
    )(proj, proj, small, small, proj, proj, small, small, w2, ab)


def _ab_post_kernel(hf_ref, hb_ref, ao_ref, az_ref, of_ref, ob_ref, bz_ref, ogb_ref, an_ref, bn_ref,
                    w_ref, x_ref, o_ref):
    d_a = A_HEADS * A_DV

    def slabs(ref):
        return jnp.concatenate([ref[t] for t in range(ref.shape[0])], axis=-1).astype(F32)

    h_sum = hf_ref[...].astype(F32) + hb_ref[...].astype(F32)
    out_a = _sigmoid(slabs(ao_ref) + ogb_ref[...]) * h_sum
    out_a = _head_rms(out_a, an_ref[...], A_HEADS) * _silu(slabs(az_ref))
    o_sum = of_ref[...].astype(F32) + ob_ref[...].astype(F32)
    out_b = _head_rms(o_sum, bn_ref[...], B_HEADS) * _silu(slabs(bz_ref))
    y = _dot(out_a.astype(MXU_DTYPE), w_ref[:d_a, :]) + _dot(out_b.astype(MXU_DTYPE), w_ref[d_a:, :])
    o_ref[...] = x_ref[...] + y


def _ab_post(h_f, h_b, o_f, o_b, proj, ogate_b, a_norm, b_norm, w_out, x):
    m, d = x.shape
    tm = min(TM_POST, m)
    row = lambda i: (i, 0)
    vec = pl.BlockSpec((1, d), lambda i: (0, 0))
    ns = d // SLAB
    blk = lambda slab: pl.BlockSpec((ns, tm, SLAB), lambda i: (slab // ns, i, 0))
    return pl.pallas_call(
        _ab_post_kernel,
        grid=(m // tm,),
        in_specs=[pl.BlockSpec((tm, d), row), pl.BlockSpec((tm, d), row), blk(AB_O), blk(AB_Z),
                  pl.BlockSpec((tm, d), row), pl.BlockSpec((tm, d), row), blk(AB_BZ),
                  vec, vec, vec, pl.BlockSpec(w_out.shape, lambda i: (0, 0)), pl.BlockSpec((tm, d), row)],
        out_specs=pl.BlockSpec((tm, d), row),
        out_shape=jax.ShapeDtypeStruct((m, d), F32),
        compiler_params=_cparams(("parallel",), 48),
        name="ab_post",
    )(h_f, h_b, proj, proj, o_f, o_b, proj, ogate_b.reshape(1, d), a_norm.reshape(1, d), b_norm.reshape(1, d),
      w_out, x)


def _cross_kernel(x_ref, g_ref, wq_ref, kv_ref, wo_ref, fin_ref, o_ref, *, final):
    x = x_ref[...]
    hn = _rms(x, g_ref[...]).astype(MXU_DTYPE)
    q = _dot(hn, wq_ref[...]).astype(MXU_DTYPE)
    d = X_HEADS * X_DH
    outs = []
    for h in range(X_HEADS):
        k = kv_ref[:, h * X_DH:(h + 1) * X_DH]
        v = kv_ref[:, d + h * X_DH:d + (h + 1) * X_DH]
        sc = _dot_nt(q[:, h * X_DH:(h + 1) * X_DH], k) * (X_DH ** -0.5)
        e = jnp.exp(sc - jnp.max(sc, axis=-1, keepdims=True))
        p = e / jnp.sum(e, axis=-1, keepdims=True)
        outs.append(_dot(p.astype(MXU_DTYPE), v))
    o = jnp.concatenate(outs, axis=-1).astype(MXU_DTYPE)
    y = x + _dot(o, wo_ref[...])
    if final:
        y = _rms(y, fin_ref[...])
    o_ref[...] = y


def _cross_attn(x, kv, g, w_q, w_o, fin, *, seq, final):
    m, d = x.shape
    tm = min(TM_CROSS, seq)
    tiles_per_seq = seq // tm
    full = lambda i: (0, 0)
    return pl.pallas_call(
        functools.partial(_cross_kernel, final=final),
        grid=(m // tm,),
        in_specs=[pl.BlockSpec((tm, d), lambda i: (i, 0)),
                  pl.BlockSpec((1, d), full),
                  pl.BlockSpec(w_q.shape, full),
                  pl.BlockSpec((N_MEM, kv.shape[1]), lambda i: (i // tiles_per_seq, 0)),
                  pl.BlockSpec(w_o.shape, full),
                  pl.BlockSpec((1, d), full)],
        out_specs=pl.BlockSpec((tm, d), lambda i: (i, 0)),
        out_shape=jax.ShapeDtypeStruct((m, d), F32),
        compiler_params=_cparams(("parallel",), 48),
        name="cross_attn",
    )(x, g.reshape(1, d), w_q, kv, w_o, fin.reshape(1, d))


C_QK = C_NOPE + LANES


C_EXP2 = (C_NOPE + C_ROPE) ** -0.5 * float(np.log2(np.e))
C_VT = C_V + 8


def _q_up_kernel(x_ref, g_ref, w_ref, cos_ref, sin_ref, o_ref):
    hn = _rms(x_ref[...], g_ref[...]).astype(MXU_DTYPE)
    pad = jnp.zeros((C_QK - C_NOPE - C_ROPE, hn.shape[0]), o_ref.dtype)
    for h in range(C_HEADS):
        rt = _dot_nt(w_ref[h], hn) * C_EXP2
        o_ref[0, h, :C_NOPE, :] = rt[:C_NOPE].astype(o_ref.dtype)
        rope = rt[C_NOPE:C_NOPE + C_ROPE] * cos_ref[...] + rt[C_NOPE + C_ROPE:] * sin_ref[...]
        o_ref[0, h, C_NOPE:C_NOPE + C_ROPE, :] = rope.astype(o_ref.dtype)
        o_ref[0, h, C_NOPE + C_ROPE:, :] = pad


def _q_up(cin, q_norm, w, cos_t, sin_t, *, batch, seq):
    tm = min(TM_UP, seq)
    tps = seq // tm
    return pl.pallas_call(
        _q_up_kernel,
        grid=(batch * tps,),
        in_specs=[pl.BlockSpec((tm, C_Q_LORA), lambda i: (i, C_QL // C_Q_LORA)),
                  pl.BlockSpec((1, C_Q_LORA), lambda i: (0, 0)),
                  pl.BlockSpec(w.shape, lambda i: (0, 0, 0)),
                  pl.BlockSpec((C_ROPE, tm), lambda i: (0, i % tps)),
                  pl.BlockSpec((C_ROPE, tm), lambda i: (0, i % tps))],
        out_specs=pl.BlockSpec((1, C_HEADS, C_QK, tm), lambda i: (i // tps, 0, 0, i % tps)),
        out_shape=jax.ShapeDtypeStruct((batch, C_HEADS, C_QK, seq), MXU_DTYPE),
        compiler_params=_cparams(("parallel",), 32),
        name="mla_q_up",
    )(cin, q_norm.reshape(1, C_Q_LORA), w, cos_t, sin_t)


def _kv_up_kernel(x_ref, g_ref, wk_ref, wvt_ref, kr_ref, ksw_ref, cos_ref, sin_ref, k_ref, vt_ref):
    hn = _rms(x_ref[...], g_ref[...]).astype(MXU_DTYPE)
    rope = (kr_ref[...] * cos_ref[...] + ksw_ref[...] * sin_ref[...]).astype(k_ref.dtype)
    tm = hn.shape[0]
    ones_rows = (lax.broadcasted_iota(jnp.int32, (C_VT - C_V, tm), 0) == 0).astype(vt_ref.dtype)
    for h in range(C_HEADS):
        k_ref[0, h, :, :C_NOPE] = _dot(hn, wk_ref[h]).astype(k_ref.dtype)
        k_ref[0, h, :, C_NOPE:] = rope
        vt_ref[0, h, :C_V, :] = _dot_nt(wvt_ref[h], hn).astype(vt_ref.dtype)
        vt_ref[0, h, C_V:, :] = ones_rows


def _kv_up(cin, kv_norm, wk, wvt, cos, sin, *, batch, seq):
    tm = min(TM_UP, seq)
    tps = seq // tm
    return pl.pallas_call(
        _kv_up_kernel,
        grid=(batch * tps,),
        in_specs=[pl.BlockSpec((tm, C_KV_LORA), lambda i: (i, C_KV // C_KV_LORA)),
                  pl.BlockSpec((1, C_KV_LORA), lambda i: (0, 0)),
                  pl.BlockSpec(wk.shape, lambda i: (0, 0, 0)),
                  pl.BlockSpec(wvt.shape, lambda i: (0, 0, 0)),
                  pl.BlockSpec((tm, LANES), lambda i: (i, C_KR // LANES)),
                  pl.BlockSpec((tm, LANES), lambda i: (i, C_KSW // LANES)),
                  pl.BlockSpec((tm, LANES), lambda i: (i % tps, 0)),
                  pl.BlockSpec((tm, LANES), lambda i: (i % tps, 0))],
        out_specs=[pl.BlockSpec((1, C_HEADS, tm, C_QK), lambda i: (i // tps, 0, i % tps, 0)),
                   pl.BlockSpec((1, C_HEADS, C_VT, tm), lambda i: (i // tps, 0, 0, i % tps))],
        out_shape=[jax.ShapeDtypeStruct((batch, C_HEADS, seq, C_QK), MXU_DTYPE),
                   jax.ShapeDtypeStruct((batch, C_HEADS, C_VT, seq), MXU_DTYPE)],
        compiler_params=_cparams(("parallel",), 32),
        name="mla_kv_up",
    )(cin, kv_norm.reshape(1, C_KV_LORA), wk, wvt, cin, cin, cos, sin)


def _flash_kernel(*refs, n_sub):
    qt_refs = refs[:n_sub]
    k_ref, vt_ref, z_ref, o_ref, m_scr, acc_scr = refs[n_sub:]
    j = pl.program_id(2)
    bk, qs = k_ref.shape[2], qt_refs[0].shape[3]

    @pl.when(j == 0)
    def _():
        m_scr[...] = jnp.full_like(m_scr, -jnp.inf)
        acc_scr[...] = jnp.zeros_like(acc_scr)

    units = [(h, a) for h in range(C_HEADS) for a in range(n_sub)]

    def scores(h, a):
        return _dot(k_ref[0, h], qt_refs[a][0, h]).reshape(bk // 8, 8, qs)

    st = scores(*units[0])
    for n, (h, a) in enumerate(units):
        q0 = a * qs
        st_next = scores(*units[n + 1]) if n + 1 < len(units) else None
        m_prev = m_scr[h, :, q0:q0 + qs]
        m_new = jnp.maximum(m_prev, jnp.max(jnp.max(st, axis=0), axis=0, keepdims=True))
        p = jnp.exp2(st - m_new[None]).reshape(bk, qs).astype(MXU_DTYPE)
        alpha = jnp.exp2(m_prev - m_new)
        pv = _dot(vt_ref[0, h], p)
        acc = acc_scr[h, :, q0:q0 + qs].reshape(C_VT // 8, 8, qs) * alpha[None]
        acc_scr[h, :, q0:q0 + qs] = acc.reshape(C_VT, qs) + pv
        m_scr[h, :, q0:q0 + qs] = m_new
        st = st_next

    @pl.when(j == pl.num_programs(2) - 1)
    def _():
        for h in range(C_HEADS):
            acc = acc_scr[h]
            o = (acc[:C_V] / acc[C_V:C_V + 1]).T
            zh = z_ref[:, h * C_V:(h + 1) * C_V]
            o_ref[:, h * C_V:(h + 1) * C_V] = (o * _silu(zh)).astype(o_ref.dtype)


def _flash(qt, k, vt, cin, *, batch, seq):
    bq, bk = min(BQ, seq), min(BK, seq)
    qs = min(QSUB, bq)
    n_sub = bq // qs
    nq, nk = seq // bq, seq // bk
    d = C_HEADS * C_V
    qt_specs = [pl.BlockSpec((1, C_HEADS, C_QK, qs), functools.partial(lambda b, i, j, a: (b, 0, 0, i * n_sub + a), a=a))
                for a in range(n_sub)]
    return pl.pallas_call(
        functools.partial(_flash_kernel, n_sub=n_sub),
        grid=(batch, nq, nk),
        in_specs=qt_specs + [
                  pl.BlockSpec((1, C_HEADS, bk, C_QK), lambda b, i, j: (b, 0, j, 0)),
                  pl.BlockSpec((1, C_HEADS, C_VT, bk), lambda b, i, j: (b, 0, 0, j)),
                  pl.BlockSpec((bq, d), lambda b, i, j: (b * nq + i, C_Z // d))],
        out_specs=pl.BlockSpec((bq, d), lambda b, i, j: (b * nq + i, 0)),
        out_shape=jax.ShapeDtypeStruct((batch * seq, d), MXU_DTYPE),
        scratch_shapes=[pltpu.VMEM((C_HEADS, 8, bq), F32), pltpu.VMEM((C_HEADS, C_VT, bq), F32)],
        compiler_params=_cparams(("parallel", "parallel", "arbitrary"), 56),
        name="mla_flash",
    )(*([qt] * n_sub), k, vt, cin)


def _split_cols(w, sizes):
    idx = np.cumsum(np.array(sizes))[:-1].tolist()
    return jnp.split(w, idx, axis=-1)


def _rope_swap(w):
    half = w.shape[-1] // 2
    return jnp.concatenate([-w[..., half:], w[..., :half]], axis=-1)


def _pad_cols(w, n):
    return jnp.pad(w, [(0, 0)] * (w.ndim - 1) + [(0, n - w.shape[-1])])


def _prep_ab(w_in, alpha_w2, alpha_b, igate_b, fgate_b):
    sizes = (2 * A_HEADS * A_DK, A_HEADS * A_DV, 2 * A_HEADS, 2 * A_HEADS, A_HEADS * A_DV, A_HEADS * A_DV,
             B_HEADS * B_DK, B_HEADS * B_DK, B_HEADS * B_DV, 2 * B_GATE_RANK, B_HEADS * B_DV)
    a_qk, a_v, a_i, a_f, a_o, a_z, b_q, b_k, b_v, b_low, b_z = _split_cols(w_in, sizes)
    w_small = _pad_cols(jnp.concatenate([a_i, a_f, b_low], axis=-1), LANES).astype(MXU_DTYPE)
    w = jnp.concatenate([a_qk, a_o, a_z, b_q, b_k, b_z, a_v, b_v], axis=-1).astype(MXU_DTYPE)
    assert w.shape[-1] == AB_SLABS * SLAB
    hk = B_HEADS * B_DK
    w2 = jnp.zeros((LANES, 2 * hk), F32)
    w2 = w2.at[SM_LOW:SM_LOW + B_GATE_RANK, :hk].set(alpha_w2[0])
    w2 = w2.at[SM_LOW + B_GATE_RANK:SM_LOW + 2 * B_GATE_RANK, hk:].set(alpha_w2[1])
    ab = alpha_b.reshape(1, 2 * hk)
    gbias = _pad_cols(jnp.concatenate([igate_b.reshape(1, -1), fgate_b.reshape(1, -1)], axis=-1), LANES)
    return w, w_small, w2.astype(MXU_DTYPE), ab, gbias


def _prep_c(w_in, w_q_up, w_kv_up):
    q_lat, kv_lat, k_rope, z = _split_cols(w_in, (C_Q_LORA, C_KV_LORA, C_ROPE, C_HEADS * C_V))
    w_c = jnp.concatenate([z, kv_lat, _pad_cols(k_rope, LANES), _pad_cols(_rope_swap(k_rope), LANES), q_lat],
                          axis=-1).astype(MXU_DTYPE)
    wq = w_q_up.reshape(C_Q_LORA, C_HEADS, C_NOPE + C_ROPE)
    nope, rope = wq[..., :C_NOPE], wq[..., C_NOPE:]
    wq_t = jnp.transpose(jnp.concatenate([nope, rope, _rope_swap(rope)], axis=-1), (1, 2, 0))
    wkv = w_kv_up.reshape(C_KV_LORA, C_HEADS, C_NOPE + C_V)
    wk = jnp.transpose(wkv[..., :C_NOPE], (1, 0, 2))
    wv_t = jnp.transpose(wkv[..., C_NOPE:], (1, 2, 0))
    return w_c, wq_t.astype(MXU_DTYPE), wk.astype(MXU_DTYPE), wv_t.astype(MXU_DTYPE)


def _rope_tables(seq):
    inv = ROPE_BASE ** (-jnp.arange(0, C_ROPE, 2, dtype=F32) / C_ROPE)
    ang = jnp.arange(seq, dtype=F32)[:, None] * inv[None, :]
    cos, sin = jnp.cos(ang), jnp.sin(ang)
    cos2, sin2 = jnp.concatenate([cos, cos], axis=-1), jnp.concatenate([sin, sin], axis=-1)
    return _pad_cols(cos2, LANES), _pad_cols(sin2, LANES), cos2.T, sin2.T


def _trunk(x, mem, p, prep):
    batch, seq, d = x.shape
    x = x.reshape(batch * seq, d)
    mem = mem.reshape(batch * N_MEM, d)
    cos, sin, cos_t, sin_t = _rope_tables(seq)
    for l in range(DEPTH):
        j = l // 2
        if l % 2 == 0:
            w_in, w_small, w2, ab, gbias = prep['ab'][j]
            proj, small = _ab_proj(x, p['norm_mix'][l], w_in, w_small)
            qk, kt = _conv_silu(proj, p['ab_conv_w'][j], p['ab_conv_b'][j], seq=seq)
            h_f, h_b = _mlstm(qk, kt, proj, small, gbias, batch=batch, seq=seq)
            o_f, o_b = _gla(proj, small, w2, ab, batch=batch, seq=seq)
            x = _ab_post(h_f, h_b, o_f, o_b, proj, p['a_ogate_b'][j], p['a_head_norm'][j], p['b_head_norm'][j],
                         prep['ab_w_out'][j], x)
        else:
            w_c, wq_t, wk, wv_t = prep['c'][j]
            cin = _norm_matmul(x, p['norm_mix'][l], w_c, tm=TM_PROJ, tn=C_TN, out_dtype=F32)
            qt = _q_up(cin, p['c_q_norm'][j], wq_t, cos_t, sin_t, batch=batch, seq=seq)
            k, vt = _kv_up(cin, p['c_kv_norm'][j], wk, wv_t, cos, sin, batch=batch, seq=seq)
            og = _flash(qt, k, vt, cin, batch=batch, seq=seq)
            x = _matmul_residual(og, prep['c_w_out'][j], x, tm=TM_OUT)
        kv = _norm_matmul(mem, p['norm_mem'][l], prep['x_w_kv'][l], tm=N_MEM, tn=1024, out_dtype=MXU_DTYPE)
        x = _cross_attn(x, kv, p['norm_cross'][l], prep['x_w_q'][l], prep['x_w_o'][l], p['final_norm'],
                        seq=seq, final=(l == DEPTH - 1))
    return x.reshape(batch, seq, d)


def kernel(x_prompt, x_sample, mem_prompt, mem_sample, norm_mix, norm_cross, norm_mem, ab_w_in, ab_conv_w, ab_conv_b, a_igate_b, a_fgate_b, a_ogate_b, a_head_norm, b_alpha_w2, b_alpha_b, b_head_norm, ab_w_out, c_w_in, c_q_norm, c_kv_norm, c_w_q_up, c_w_kv_up, c_w_out, x_w_q, x_w_kv, x_w_o, final_norm):
    p = dict(norm_mix=norm_mix, norm_cross=norm_cross, norm_mem=norm_mem, ab_conv_w=ab_conv_w,
             ab_conv_b=ab_conv_b, a_ogate_b=a_ogate_b, a_head_norm=a_head_norm, b_head_norm=b_head_norm,
             c_q_norm=c_q_norm, c_kv_norm=c_kv_norm, final_norm=final_norm)
    n_even, n_odd = ab_w_in.shape[0], c_w_in.shape[0]
    prep = dict(
        ab=[_prep_ab(ab_w_in[j], b_alpha_w2[j], b_alpha_b[j], a_igate_b[j], a_fgate_b[j]) for j in range(n_even)],
        ab_w_out=[ab_w_out[j].astype(MXU_DTYPE) for j in range(n_even)],
        c=[_prep_c(c_w_in[j], c_w_q_up[j], c_w_kv_up[j]) for j in range(n_odd)],
        c_w_out=[c_w_out[j].astype(MXU_DTYPE) for j in range(n_odd)],
        x_w_q=[x_w_q[l].astype(MXU_DTYPE) for l in range(DEPTH)],
        x_w_kv=[x_w_kv[l].astype(MXU_DTYPE) for l in range(DEPTH)],
        x_w_o=[x_w_o[l].astype(MXU_DTYPE) for l in range(DEPTH)],
    )
    return (_trunk(x_prompt, mem_prompt, p, prep), _trunk(x_sample, mem_sample, p, prep))
```

```python
import functools

import numpy as np
import jax
import jax.numpy as jnp
from jax import lax
from jax.experimental import pallas as pl
from jax.experimental.pallas import tpu as pltpu

F32 = jnp.float32
MXU_DTYPE = jnp.bfloat16
EPS = 1e-6
HIGHEST = lax.Precision.HIGHEST

D_MODEL = 1024
DEPTH = 4
A_HEADS, A_DK, A_DV, A_CHUNK = 4, 128, 256, 128
B_HEADS, B_DK, B_DV, B_CHUNK = 4, 128, 256, 64
B_GATE_RANK, B_GATE_TAU = 16, 16.0
B_SUB = 16
B_CHUNKS_PER_STEP = 4
C_HEADS, C_Q_LORA, C_KV_LORA, C_NOPE, C_ROPE, C_V = 8, 384, 256, 128, 64, 128
ROPE_BASE = 10000.0
X_HEADS, X_DH, N_MEM = 4, 256, 256

LANES = 128
V7X_VMEM_BYTES = 64 * 1024 * 1024

SLAB = 512
AB_QK, AB_O, AB_Z, AB_BQ, AB_BK, AB_BZ, AB_AV, AB_BV = 0, 2, 4, 6, 7, 8, 10, 12
AB_SLABS = 14
AB_SLABS_PER_STEP = 7
SM_I, SM_F, SM_LOW = 0, 8, 16
C_Z, C_KV, C_KR, C_KSW, C_QL = 0, 1024, 1280, 1408, 1536
C_COLS = 1920
C_TN = 640

TM_PROJ = 1024
TM_CONV = 512
TM_POST = 512
TM_CROSS = 512
TM_UP = 512
TM_OUT = 512
BQ, BK, QSUB, KSUB = 1024, 1024, 512, 512


def _cparams(sem, vmem_mb=48):
    assert vmem_mb * 1024 * 1024 <= V7X_VMEM_BYTES
    return pltpu.CompilerParams(dimension_semantics=sem, vmem_limit_bytes=vmem_mb * 1024 * 1024)


def _sigmoid(x):
    return 1.0 / (1.0 + jnp.exp(-x))


def _silu(x):
    return x * _sigmoid(x)


def _log_sigmoid(x):
    return jnp.minimum(x, 0.0) - jnp.log1p(jnp.exp(-jnp.abs(x)))


def _dot(a, b):
    return jnp.dot(a, b, preferred_element_type=F32)


def _dot_nt(a, b):
    return lax.dot_general(a, b, (((1,), (1,)), ((), ())), preferred_element_type=F32)


def _dot_tn(a, b):
    return lax.dot_general(a, b, (((0,), (0,)), ((), ())), preferred_element_type=F32)


def _rms(x, g):
    ms = jnp.mean(x * x, axis=-1, keepdims=True)
    return x * lax.rsqrt(ms + EPS) * g


def _head_rms(x, g, n_heads):
    w = x.shape[-1] // n_heads
    parts = []
    for h in range(n_heads):
        xs = x[:, h * w:(h + 1) * w]
        ms = jnp.mean(xs * xs, axis=-1, keepdims=True)
        parts.append(xs * lax.rsqrt(ms + EPS))
    return jnp.concatenate(parts, axis=-1) * g


def _norm_matmul_kernel(x_ref, g_ref, w_ref, o_ref, h_ref):
    @pl.when(pl.program_id(1) == 0)
    def _():
        h_ref[...] = _rms(x_ref[...], g_ref[...]).astype(h_ref.dtype)

    o_ref[...] = _dot(h_ref[...], w_ref[...]).astype(o_ref.dtype)


def _norm_matmul(x, g, w, *, tm, tn, out_dtype, x_col_block=0):
    m = x.shape[0]
    k, n = w.shape
    tm = min(tm, m)
    return pl.pallas_call(
        _norm_matmul_kernel,
        grid=(m // tm, n // tn),
        in_specs=[pl.BlockSpec((tm, k), lambda i, j: (i, x_col_block)),
                  pl.BlockSpec((1, k), lambda i, j: (0, 0)),
                  pl.BlockSpec((k, tn), lambda i, j: (0, j))],
        out_specs=pl.BlockSpec((tm, tn), lambda i, j: (i, j)),
        out_shape=jax.ShapeDtypeStruct((m, n), out_dtype),
        scratch_shapes=[pltpu.VMEM((tm, k), MXU_DTYPE)],
        compiler_params=_cparams(("parallel", "arbitrary"), 56),
        name="norm_matmul",
    )(x, g.reshape(1, k), w)


def _ab_proj_kernel(x_ref, g_ref, w_ref, ws_ref, o_ref, os_ref, h_ref):
    @pl.when(pl.program_id(1) == 0)
    def _():
        h_ref[...] = _rms(x_ref[...], g_ref[...]).astype(h_ref.dtype)
        os_ref[...] = _dot(h_ref[...], ws_ref[...])

    for t in range(o_ref.shape[0]):
        o_ref[t] = _dot(h_ref[...], w_ref[:, t * SLAB:(t + 1) * SLAB]).astype(o_ref.dtype)


def _ab_proj(x, g, w, w_small):
    m, k = x.shape
    tm = min(TM_PROJ, m)
    ns = AB_SLABS_PER_STEP
    return pl.pallas_call(
        _ab_proj_kernel,
        grid=(m // tm, AB_SLABS // ns),
        in_specs=[pl.BlockSpec((tm, k), lambda i, j: (i, 0)),
                  pl.BlockSpec((1, k), lambda i, j: (0, 0)),
                  pl.BlockSpec((k, ns * SLAB), lambda i, j: (0, j)),
                  pl.BlockSpec((k, LANES), lambda i, j: (0, 0))],
        out_specs=[pl.BlockSpec((ns, tm, SLAB), lambda i, j: (j, i, 0)),
                   pl.BlockSpec((tm, LANES), lambda i, j: (i, 0))],
        out_shape=[jax.ShapeDtypeStruct((AB_SLABS, m, SLAB), MXU_DTYPE),
                   jax.ShapeDtypeStruct((m, LANES), F32)],
        scratch_shapes=[pltpu.VMEM((tm, k), MXU_DTYPE)],
        compiler_params=_cparams(("parallel", "arbitrary"), 56),
        name="ab_proj",
    )(x, g.reshape(1, k), w, w_small)


def _matmul_res_kernel(a_ref, w_ref, x_ref, o_ref):
    o_ref[...] = x_ref[...] + _dot(a_ref[...], w_ref[...])


def _matmul_residual(a, w, x, *, tm):
    m, k = a.shape
    n = w.shape[1]
    tm = min(tm, m)
    return pl.pallas_call(
        _matmul_res_kernel,
        grid=(m // tm,),
        in_specs=[pl.BlockSpec((tm, k), lambda i: (i, 0)),
                  pl.BlockSpec((k, n), lambda i: (0, 0)),
                  pl.BlockSpec((tm, n), lambda i: (i, 0))],
        out_specs=pl.BlockSpec((tm, n), lambda i: (i, 0)),
        out_shape=jax.ShapeDtypeStruct((m, n), F32),
        compiler_params=_cparams(("parallel",), 40),
        name="matmul_residual",
    )(a, w, x)


def _conv_kernel(x_ref, xp_ref, xn_ref, w_ref, b_ref, o_ref, kt_ref, *, tiles_per_seq):
    i = pl.program_id(0)
    ts = x_ref.shape[1]
    t_in_seq = i % tiles_per_seq
    rows = lax.broadcasted_iota(jnp.int32, (ts, SLAB), 0)
    for t in range(2):
        cs = slice(t * SLAB, (t + 1) * SLAB)
        x = x_ref[t].astype(F32)
        last = xp_ref.shape[1] - 1
        prev_row = jnp.where(t_in_seq == 0, 0.0, xp_ref[t, last:last + 1, :].astype(F32))
        next_row = jnp.where(t_in_seq == tiles_per_seq - 1, 0.0, xn_ref[t, 0:1, :].astype(F32))
        x_prev = jnp.where(rows == 0, prev_row, pltpu.roll(x, 1, axis=0))
        x_next = jnp.where(rows == ts - 1, next_row, pltpu.roll(x, ts - 1, axis=0))
        y = _silu(w_ref[0:1, cs] * x_prev + w_ref[1:2, cs] * x + w_ref[2:3, cs] * x_next + b_ref[:, cs])
        if t == 0:
            o_ref[:, cs] = (y * (A_DK ** -0.5)).astype(o_ref.dtype)
        else:
            o_ref[:, cs] = y.astype(o_ref.dtype)
            kt_ref[...] = y.T.astype(kt_ref.dtype)


def _conv_silu(proj, conv_w, conv_b, *, seq):
    m = proj.shape[1]
    c = conv_w.shape[1]
    assert c == 2 * SLAB and AB_QK == 0
    ts = min(TM_CONV, seq)
    nt = m // ts
    halo = 16
    sub = ts // halo
    return pl.pallas_call(
        functools.partial(_conv_kernel, tiles_per_seq=seq // ts),
        grid=(nt,),
        in_specs=[pl.BlockSpec((2, ts, SLAB), lambda i: (0, i, 0)),
                  pl.BlockSpec((2, halo, SLAB), lambda i: (0, jnp.maximum(i * sub - 1, 0), 0)),
                  pl.BlockSpec((2, halo, SLAB), lambda i: (0, jnp.minimum((i + 1) * sub, nt * sub - 1), 0)),
                  pl.BlockSpec((3, c), lambda i: (0, 0)),
                  pl.BlockSpec((1, c), lambda i: (0, 0))],
        out_specs=[pl.BlockSpec((ts, c), lambda i: (i, 0)),
                   pl.BlockSpec((c // 2, ts), lambda i: (0, i))],
        out_shape=[jax.ShapeDtypeStruct((m, c), MXU_DTYPE),
                   jax.ShapeDtypeStruct((c // 2, m), MXU_DTYPE)],
        compiler_params=_cparams(("parallel",), 32),
        name="conv_silu",
    )(proj, proj, proj, conv_w, conv_b.reshape(1, c))


def _mlstm_kernel(qk_f, kt_f, v_f, g_f, gn_f, qk_b, kt_b, v_b, g_b, gn_b, gbias_ref, h_f, h_b,
                  c_scr, n_scr, m_scr, gate_scr):
    L = A_CHUNK

    row = lax.broadcasted_iota(jnp.int32, (L, L), 0)
    col = lax.broadcasted_iota(jnp.int32, (L, L), 1)
    sub8 = lax.broadcasted_iota(jnp.int32, (8, L), 0)
    lane_row = lax.broadcasted_iota(jnp.int32, (1, LANES), 1)
    neg_inf = float("-inf")
    masks = (row >= col, row <= col)

    def gate_sums(gate_ref, d):
        gates = gate_ref[...] + gbias_ref[...]
        csum = jnp.dot(masks[d].astype(F32), _log_sigmoid(gates), precision=HIGHEST, preferred_element_type=F32)
        a_c = pltpu.roll(csum, LANES - (SM_F - SM_I), axis=1)
        e_c = gates - a_c
        run = e_c
        sh = 1
        while sh < L:
            pad = jnp.full((sh, LANES), neg_inf, F32)
            moved = (jnp.concatenate([pad, run[:L - sh]], axis=0) if d == 0
                     else jnp.concatenate([run[sh:], pad], axis=0))
            run = jnp.maximum(run, moved)
            sh *= 2
        return a_c, run, e_c.T

    def put_gate_sums(d, vals):
        for n, val in enumerate(vals):
            gate_scr[d, n] = val

    @pl.when(pl.program_id(1) == 0)
    def _():
        c_scr[...] = jnp.zeros_like(c_scr)
        n_scr[...] = jnp.zeros_like(n_scr)
        m_scr[...] = jnp.zeros_like(m_scr)
        put_gate_sums(0, gate_sums(g_f, 0))
        put_gate_sums(1, gate_sums(g_b, 1))

    m_all = m_scr[0:1, :]
    m_next = m_all
    dirs = ((qk_f, kt_f, v_f, g_f, h_f), (qk_b, kt_b, v_b, g_b, h_b))
    gate_next = [gate_sums(gn_f, 0), gate_sums(gn_b, 1)]
    pro = []
    for d in range(2):
        a_c, run, e_t = gate_scr[d, 0], gate_scr[d, 1], gate_scr[d, 2]
        m_c = jnp.maximum(run, m_all)
        w_inter_c = jnp.exp(m_all - m_c)
        inv_floor_c = jnp.exp(-(a_c + m_c))
        pro.append((masks[d], a_c, m_c, w_inter_c, inv_floor_c, e_t, m_c.T))

    for d, (qk_ref, kt_ref, v_ref, g_ref, h_ref) in enumerate(dirs):
        mask, a_c, m_c, w_inter_c, inv_floor_c, e_t, m_t_rows = pro[d]
        end = L - 1 if d == 0 else 0
        for h in range(A_HEADS):
            s = d * A_HEADS + h
            n_prev = n_scr[s]
            c_prev = c_scr[s]
            q = qk_ref[:, h * A_DK:(h + 1) * A_DK]
            k = qk_ref[:, (A_HEADS + h) * A_DK:(A_HEADS + h + 1) * A_DK]
            kt = kt_ref[h * A_DK:(h + 1) * A_DK, :]
            v0 = h * A_DV
            v = v_ref[v0 // SLAB, :, v0 % SLAB:v0 % SLAB + A_DV]
            e_r, m_r = e_t[s:s + 1, :], m_t_rows[s:s + 1, :]

            lhs = jnp.where(sub8 == 0, -m_r, jnp.where(sub8 == 1, 1.0, 0.0))
            rhs = jnp.where(sub8 == 0, 1.0, jnp.where(sub8 == 1, e_r, 0.0))
            x = lax.dot_general(lhs, rhs, (((0,), (0,)), ((), ())), precision=HIGHEST,
                                preferred_element_type=F32)
            sc = _dot_nt(q, k) * jnp.exp(jnp.where(mask, x, neg_inf))
            w_inter = w_inter_c[:, s:s + 1]
            num = _dot(sc.astype(MXU_DTYPE), v) + w_inter * _dot(q, c_prev.astype(MXU_DTYPE))
            qn = _dot_nt(q, n_prev.astype(MXU_DTYPE))[:, 0:1]
            den = jnp.sum(sc, axis=1, keepdims=True) + w_inter * qn
            hv = num / jnp.maximum(jnp.abs(den), inv_floor_c[:, s:s + 1])
            h_ref[:, h * A_DV:(h + 1) * A_DV] = hv.astype(h_ref.dtype)

            g = a_c[end:end + 1, s:s + 1]
            m_end = m_c[end:end + 1, s:s + 1]
            w = jnp.exp(e_r - m_end)
            decay = w_inter_c[end:end + 1, s:s + 1]
            kwt = (kt.astype(F32) * w).astype(MXU_DTYPE)
            c_scr[s] = decay * c_prev + _dot(kwt, v)
            n_scr[s] = decay * n_prev + _dot(jnp.broadcast_to(w, (8, L)).astype(MXU_DTYPE), k)
            m_next = jnp.where(lane_row == s, g + m_end, m_next)
    m_scr[...] = jnp.broadcast_to(m_next, m_scr.shape)
    put_gate_sums(0, gate_next[0])
    put_gate_sums(1, gate_next[1])


def _mlstm(qk, kt, proj, small, gbias, *, batch, seq):
    m = qk.shape[0]
    L = A_CHUNK
    nc = seq // L
    d_a = A_HEADS * A_DV
    v_slabs = d_a // SLAB
    v_blk = AB_AV // v_slabs
    assert L == LANES
    fwd = lambda b, c: b * nc + c
    bwd = lambda b, c: b * nc + (nc - 1 - c)
    nxt = lambda c: jnp.minimum(c + 1, nc - 1)
    out = jax.ShapeDtypeStruct((m, d_a), MXU_DTYPE)
    return pl.pallas_call(
        _mlstm_kernel,
        grid=(batch, nc),
        in_specs=[pl.BlockSpec((L, 2 * A_HEADS * A_DK), lambda b, c: (fwd(b, c), 0)),
                  pl.BlockSpec((A_HEADS * A_DK, L), lambda b, c: (0, fwd(b, c))),
                  pl.BlockSpec((v_slabs, L, SLAB), lambda b, c: (v_blk, fwd(b, c), 0)),
                  pl.BlockSpec((L, LANES), lambda b, c: (fwd(b, c), 0)),
                  pl.BlockSpec((L, LANES), lambda b, c: (fwd(b, nxt(c)), 0)),
                  pl.BlockSpec((L, 2 * A_HEADS * A_DK), lambda b, c: (bwd(b, c), 0)),
                  pl.BlockSpec((A_HEADS * A_DK, L), lambda b, c: (0, bwd(b, c))),
                  pl.BlockSpec((v_slabs, L, SLAB), lambda b, c: (v_blk, bwd(b, c), 0)),
                  pl.BlockSpec((L, LANES), lambda b, c: (bwd(b, c), 0)),
                  pl.BlockSpec((L, LANES), lambda b, c: (bwd(b, nxt(c)), 0)),
                  pl.BlockSpec((1, LANES), lambda b, c: (0, 0))],
        out_specs=[pl.BlockSpec((L, d_a), lambda b, c: (fwd(b, c), 0)),
                   pl.BlockSpec((L, d_a), lambda b, c: (bwd(b, c), 0))],
        out_shape=[out, out],
        scratch_shapes=[pltpu.VMEM((2 * A_HEADS, A_DK, A_DV), F32),
                        pltpu.VMEM((2 * A_HEADS, 8, A_DK), F32),
                        pltpu.VMEM((8, LANES), F32),
                        pltpu.VMEM((2, 3, L, LANES), F32)],
        compiler_params=_cparams(("parallel", "arbitrary"), 32),
        name="mlstm",
    )(qk, kt, proj, small, small, qk, kt, proj, small, small, gbias)


def _gla_kernel(qk_f, v_f, g_f, gn_f, qk_b, v_b, g_b, gn_b, w2_ref, ab_ref, o_f, o_b, st_scr, bc_scr):
    L, SB = B_CHUNK, B_SUB
    nsb = L // SB
    hk = B_HEADS * B_DK
    rows_per_step = g_f.shape[0]
    n_chunks = rows_per_step // L

    row = lax.broadcasted_iota(jnp.int32, (L, L), 0)
    col = lax.broadcasted_iota(jnp.int32, (L, L), 1)
    lane8 = lax.broadcasted_iota(jnp.int32, (8, L), 1)
    log2e = float(np.log2(np.e))
    dirs = ((qk_f, v_f, o_f), (qk_b, v_b, o_b))
    masks = (row >= col, row <= col)

    def decay_sums(gate_ref, d):
        pre = _dot(gate_ref[...].astype(MXU_DTYPE), w2_ref[:, d * hk:(d + 1) * hk]) + ab_ref[:, d * hk:(d + 1) * hk]
        log_a = _log_sigmoid(pre) / B_GATE_TAU
        tri = masks[d].astype(F32)
        return jnp.concatenate(
            [jnp.dot(tri, log_a[u * L:(u + 1) * L], precision=HIGHEST, preferred_element_type=F32)
             for u in range(n_chunks)], axis=0)

    @pl.when(pl.program_id(1) == 0)
    def _():
        st_scr[...] = jnp.zeros_like(st_scr)
        bc_scr[0] = decay_sums(g_f, 0)
        bc_scr[1] = decay_sums(g_b, 1)

    bc_all = [bc_scr[0], bc_scr[1]]
    bc_next = [decay_sums(gn_f, 0), decay_sums(gn_b, 1)]

    def chunk(r0, states):
        work = []
        for d, (qk_ref, v_ref, o_ref) in enumerate(dirs):
            rs = slice(r0[d], r0[d] + L)
            for h in range(B_HEADS):
                s = d * B_HEADS + h
                q = qk_ref[0, rs, h * B_DK:(h + 1) * B_DK].astype(F32) * (B_DK ** -0.5)
                k = qk_ref[1, rs, h * B_DK:(h + 1) * B_DK].astype(F32)
                b = bc_all[d][rs, h * B_DK:(h + 1) * B_DK]
                st = states[s]
                inter = _dot_nt((q * jnp.exp(b)).astype(MXU_DTYPE), st.astype(MXU_DTYPE))

                b2 = b * log2e
                b2_rows = [jnp.broadcast_to(b2[j:j + 1], (8, B_DK)) for j in range(L)]
                pieces, where_to = [], []
                for blk in range(nsb):
                    lo = blk * SB
                    for oc in range(SB // 8):
                        t0 = lo + 8 * oc
                        js = range(0, 8 * oc + 8) if d == 0 else range(8 * oc, SB)
                        for j in js:
                            pieces.append(q[t0:t0 + 8] * jnp.exp2(b2[t0:t0 + 8] - b2_rows[lo + j]))
                            where_to.append((t0, lo + j))
                diag = _dot_nt(jnp.concatenate(pieces, axis=0).astype(MXU_DTYPE), k.astype(MXU_DTYPE))

                offs = []
                for blk in range(nsb):
                    lo, hi = blk * SB, (blk + 1) * SB
                    if d == 0 and blk > 0:
                        ref = b[lo - 1:lo]
                        kt = k[:lo] * jnp.exp(ref - b[:lo])
                        kt = jnp.concatenate([kt, jnp.zeros((L - lo, B_DK), F32)], axis=0)
                    elif d == 1 and blk < nsb - 1:
                        ref = b[hi:hi + 1]
                        kt = k[hi:] * jnp.exp(ref - b[hi:])
                        kt = jnp.concatenate([jnp.zeros((hi, B_DK), F32), kt], axis=0)
                    else:
                        offs.append(jnp.zeros((SB, L), F32))
                        continue
                    offs.append(_dot_nt((q[lo:hi] * jnp.exp(b[lo:hi] - ref)).astype(MXU_DTYPE),
                                        kt.astype(MXU_DTYPE)))
                work.append((d, h, s, rs, k, b, st, inter, diag, where_to, offs))

        for d, h, s, rs, k, b, st, inter, diag, where_to, offs in work:
            v = dirs[d][1][h * B_DV // SLAB, rs, h * B_DV % SLAB:h * B_DV % SLAB + B_DV]
            att_rows = []
            for blk in range(nsb):
                for oc in range(SB // 8):
                    t0 = blk * SB + 8 * oc
                    a8 = offs[blk][8 * oc:8 * oc + 8]
                    for n, (rr, cc) in enumerate(where_to):
                        if rr == t0:
                            a8 = jnp.where(lane8 == cc, diag[8 * n:8 * n + 8], a8)
                    att_rows.append(a8)
            att = jnp.where(masks[d], jnp.concatenate(att_rows, axis=0), 0.0)
            o = inter + _dot(att.astype(MXU_DTYPE), v)
            dirs[d][2][rs, h * B_DV:(h + 1) * B_DV] = o.astype(dirs[d][2].dtype)

        new_states = list(states)
        for d, h, s, rs, k, b, st, inter, diag, where_to, offs in work:
            v = dirs[d][1][h * B_DV // SLAB, rs, h * B_DV % SLAB:h * B_DV % SLAB + B_DV]
            g = b[L - 1:L, :] if d == 0 else b[0:1, :]
            kd = (k * jnp.exp(g - b)).astype(MXU_DTYPE)
            new_states[s] = jnp.exp(g) * st + _dot_tn(v, kd)
        return new_states

    states = [st_scr[s] for s in range(2 * B_HEADS)]
    for u in range(n_chunks):
        states = chunk((u * L, (n_chunks - 1 - u) * L), states)
    for s in range(2 * B_HEADS):
        st_scr[s] = states[s]
    bc_scr[0] = bc_next[0]
    bc_scr[1] = bc_next[1]


def _gla(proj, small, w2, ab, *, batch, seq):
    m = proj.shape[1]
    rows = min(B_CHUNKS_PER_STEP * B_CHUNK, seq)
    ns = seq // rows
    hk = B_HEADS * B_DK
    d_b = B_HEADS * B_DV
    assert hk == SLAB and AB_BK == AB_BQ + 1 and AB_BQ % 2 == 0
    v_slabs = d_b // SLAB
    fwd = lambda b, c: b * ns + c
    bwd = lambda b, c: b * ns + (ns - 1 - c)

    def specs(idx):
        nxt = lambda b, c: idx(b, jnp.minimum(c + 1, ns - 1))
        return [pl.BlockSpec((2, rows, SLAB), lambda b, c: (AB_BQ // 2, idx(b, c), 0)),
                pl.BlockSpec((v_slabs, rows, SLAB), lambda b, c: (AB_BV // v_slabs, idx(b, c), 0)),
                pl.BlockSpec((rows, LANES), lambda b, c: (idx(b, c), 0)),
                pl.BlockSpec((rows, LANES), lambda b, c: (nxt(b, c), 0))]

    out = jax.ShapeDtypeStruct((m, d_b), MXU_DTYPE)
    return pl.pallas_call(
        _gla_kernel,
        grid=(batch, ns),
        in_specs=specs(fwd) + specs(bwd) + [pl.BlockSpec((LANES, 2 * hk), lambda b, c: (0, 0)),
                                            pl.BlockSpec((1, 2 * hk), lambda b, c: (0, 0))],
        out_specs=[pl.BlockSpec((rows, d_b), lambda b, c: (fwd(b, c), 0)),
                   pl.BlockSpec((rows, d_b), lambda b, c: (bwd(b, c), 0))],
        out_shape=[out, out],
        scratch_shapes=[pltpu.VMEM((2 * B_HEADS, B_DV, B_DK), F32),
                        pltpu.VMEM((2, rows, hk), F32)],
        compiler_params=_cparams(("parallel", "arbitrary"), 32),
        name="gla",
    )(proj, proj, small, small, proj, proj, small, small, w2, ab)


def _ab_post_kernel(hf_ref, hb_ref, ao_ref, az_ref, of_ref, ob_ref, bz_ref, ogb_ref, an_ref, bn_ref,
                    w_ref, x_ref, o_ref):
    d_a = A_HEADS * A_DV

    def slabs(ref):
        return jnp.concatenate([ref[t] for t in range(ref.shape[0])], axis=-1).astype(F32)

    h_sum = hf_ref[...].astype(F32) + hb_ref[...].astype(F32)
    out_a = _sigmoid(slabs(ao_ref) + ogb_ref[...]) * h_sum
    out_a = _head_rms(out_a, an_ref[...], A_HEADS) * _silu(slabs(az_ref))
    o_sum = of_ref[...].astype(F32) + ob_ref[...].astype(F32)
    out_b = _head_rms(o_sum, bn_ref[...], B_HEADS) * _silu(slabs(bz_ref))
    y = _dot(out_a.astype(MXU_DTYPE), w_ref[:d_a, :]) + _dot(out_b.astype(MXU_DTYPE), w_ref[d_a:, :])
    o_ref[...] = x_ref[...] + y


def _ab_post(h_f, h_b, o_f, o_b, proj, ogate_b, a_norm, b_norm, w_out, x):
    m, d = x.shape
    tm = min(TM_POST, m)
    row = lambda i: (i, 0)
    vec = pl.BlockSpec((1, d), lambda i: (0, 0))
    ns = d // SLAB
    blk = lambda slab: pl.BlockSpec((ns, tm, SLAB), lambda i: (slab // ns, i, 0))
    return pl.pallas_call(
        _ab_post_kernel,
        grid=(m // tm,),
        in_specs=[pl.BlockSpec((tm, d), row), pl.BlockSpec((tm, d), row), blk(AB_O), blk(AB_Z),
                  pl.BlockSpec((tm, d), row), pl.BlockSpec((tm, d), row), blk(AB_BZ),
                  vec, vec, vec, pl.BlockSpec(w_out.shape, lambda i: (0, 0)), pl.BlockSpec((tm, d), row)],
        out_specs=pl.BlockSpec((tm, d), row),
        out_shape=jax.ShapeDtypeStruct((m, d), F32),
        compiler_params=_cparams(("parallel",), 48),
        name="ab_post",
    )(h_f, h_b, proj, proj, o_f, o_b, proj, ogate_b.reshape(1, d), a_norm.reshape(1, d), b_norm.reshape(1, d),
      w_out, x)


def _cross_kernel(x_ref, g_ref, wq_ref, kv_ref, wo_ref, fin_ref, o_ref, *, final):
    x = x_ref[...]
    hn = _rms(x, g_ref[...]).astype(MXU_DTYPE)
    q = _dot(hn, wq_ref[...]).astype(MXU_DTYPE)
    d = X_HEADS * X_DH
    outs = []
    for h in range(X_HEADS):
        k = kv_ref[:, h * X_DH:(h + 1) * X_DH]
        v = kv_ref[:, d + h * X_DH:d + (h + 1) * X_DH]
        sc = _dot_nt(q[:, h * X_DH:(h + 1) * X_DH], k) * (X_DH ** -0.5)
        e = jnp.exp(sc - jnp.max(sc, axis=-1, keepdims=True))
        p = e / jnp.sum(e, axis=-1, keepdims=True)
        outs.append(_dot(p.astype(MXU_DTYPE), v))
    o = jnp.concatenate(outs, axis=-1).astype(MXU_DTYPE)
    y = x + _dot(o, wo_ref[...])
    if final:
        y = _rms(y, fin_ref[...])
    o_ref[...] = y


def _cross_attn(x, kv, g, w_q, w_o, fin, *, seq, final):
    m, d = x.shape
    tm = min(TM_CROSS, seq)
    tiles_per_seq = seq // tm
    full = lambda i: (0, 0)
    return pl.pallas_call(
        functools.partial(_cross_kernel, final=final),
        grid=(m // tm,),
        in_specs=[pl.BlockSpec((tm, d), lambda i: (i, 0)),
                  pl.BlockSpec((1, d), full),
                  pl.BlockSpec(w_q.shape, full),
                  pl.BlockSpec((N_MEM, kv.shape[1]), lambda i: (i // tiles_per_seq, 0)),
                  pl.BlockSpec(w_o.shape, full),
                  pl.BlockSpec((1, d), full)],
        out_specs=pl.BlockSpec((tm, d), lambda i: (i, 0)),
        out_shape=jax.ShapeDtypeStruct((m, d), F32),
        compiler_params=_cparams(("parallel",), 48),
        name="cross_attn",
    )(x, g.reshape(1, d), w_q, kv, w_o, fin.reshape(1, d))


C_QK = C_NOPE + LANES


C_EXP2 = (C_NOPE + C_ROPE) ** -0.5 * float(np.log2(np.e))
C_VT = C_V + 8


def _q_up_kernel(x_ref, g_ref, w_ref, cos_ref, sin_ref, o_ref):
    hn = _rms(x_ref[...], g_ref[...]).astype(MXU_DTYPE)
    pad = jnp.zeros((C_QK - C_NOPE - C_ROPE, hn.shape[0]), o_ref.dtype)
    for h in range(C_HEADS):
        rt = _dot_nt(w_ref[h], hn) * C_EXP2
        o_ref[0, h, :C_NOPE, :] = rt[:C_NOPE].astype(o_ref.dtype)
        rope = rt[C_NOPE:C_NOPE + C_ROPE] * cos_ref[...] + rt[C_NOPE + C_ROPE:] * sin_ref[...]
        o_ref[0, h, C_NOPE:C_NOPE + C_ROPE, :] = rope.astype(o_ref.dtype)
        o_ref[0, h, C_NOPE + C_ROPE:, :] = pad


def _q_up(cin, q_norm, w, cos_t, sin_t, *, batch, seq):
    tm = min(TM_UP, seq)
    tps = seq // tm
    return pl.pallas_call(
        _q_up_kernel,
        grid=(batch * tps,),
        in_specs=[pl.BlockSpec((tm, C_Q_LORA), lambda i: (i, C_QL // C_Q_LORA)),
                  pl.BlockSpec((1, C_Q_LORA), lambda i: (0, 0)),
                  pl.BlockSpec(w.shape, lambda i: (0, 0, 0)),
                  pl.BlockSpec((C_ROPE, tm), lambda i: (0, i % tps)),
                  pl.BlockSpec((C_ROPE, tm), lambda i: (0, i % tps))],
        out_specs=pl.BlockSpec((1, C_HEADS, C_QK, tm), lambda i: (i // tps, 0, 0, i % tps)),
        out_shape=jax.ShapeDtypeStruct((batch, C_HEADS, C_QK, seq), MXU_DTYPE),
        compiler_params=_cparams(("parallel",), 32),
        name="mla_q_up",
    )(cin, q_norm.reshape(1, C_Q_LORA), w, cos_t, sin_t)


def _kv_up_kernel(x_ref, g_ref, wk_ref, wvt_ref, kr_ref, ksw_ref, cos_ref, sin_ref, k_ref, vt_ref):
    hn = _rms(x_ref[...], g_ref[...]).astype(MXU_DTYPE)
    rope = (kr_ref[...] * cos_ref[...] + ksw_ref[...] * sin_ref[...]).astype(k_ref.dtype)
    tm = hn.shape[0]
    ones_rows = (lax.broadcasted_iota(jnp.int32, (C_VT - C_V, tm), 0) == 0).astype(vt_ref.dtype)
    for h in range(C_HEADS):
        k_ref[0, h, :, :C_NOPE] = _dot(hn, wk_ref[h]).astype(k_ref.dtype)
        k_ref[0, h, :, C_NOPE:] = rope
        vt_ref[0, h, :C_V, :] = _dot_nt(wvt_ref[h], hn).astype(vt_ref.dtype)
        vt_ref[0, h, C_V:, :] = ones_rows


def _kv_up(cin, kv_norm, wk, wvt, cos, sin, *, batch, seq):
    tm = min(TM_UP, seq)
    tps = seq // tm
    return pl.pallas_call(
        _kv_up_kernel,
        grid=(batch * tps,),
        in_specs=[pl.BlockSpec((tm, C_KV_LORA), lambda i: (i, C_KV // C_KV_LORA)),
                  pl.BlockSpec((1, C_KV_LORA), lambda i: (0, 0)),
                  pl.BlockSpec(wk.shape, lambda i: (0, 0, 0)),
                  pl.BlockSpec(wvt.shape, lambda i: (0, 0, 0)),
                  pl.BlockSpec((tm, LANES), lambda i: (i, C_KR // LANES)),
                  pl.BlockSpec((tm, LANES), lambda i: (i, C_KSW // LANES)),
                  pl.BlockSpec((tm, LANES), lambda i: (i % tps, 0)),
                  pl.BlockSpec((tm, LANES), lambda i: (i % tps, 0))],
        out_specs=[pl.BlockSpec((1, C_HEADS, tm, C_QK), lambda i: (i // tps, 0, i % tps, 0)),
                   pl.BlockSpec((1, C_HEADS, C_VT, tm), lambda i: (i // tps, 0, 0, i % tps))],
        out_shape=[jax.ShapeDtypeStruct((batch, C_HEADS, seq, C_QK), MXU_DTYPE),
                   jax.ShapeDtypeStruct((batch, C_HEADS, C_VT, seq), MXU_DTYPE)],
        compiler_params=_cparams(("parallel",), 32),
        name="mla_kv_up",
    )(cin, kv_norm.reshape(1, C_KV_LORA), wk, wvt, cin, cin, cos, sin)


def _flash_kernel(*refs, n_sub, n_ksub):
    qt_refs = refs[:n_sub]
    vt_refs = refs[n_sub:n_sub + n_ksub]
    k_ref, z_ref, o_ref, m_scr, acc_scr = refs[n_sub + n_ksub:]
    j = pl.program_id(2)
    ks, qs = vt_refs[0].shape[3], qt_refs[0].shape[3]

    @pl.when(j == 0)
    def _():
        m_scr[...] = jnp.full_like(m_scr, -jnp.inf)
        acc_scr[...] = jnp.zeros_like(acc_scr)

    units = [(kb, h, a) for kb in range(n_ksub) for h in range(C_HEADS) for a in range(n_sub)]

    def scores(kb, h, a):
        return _dot(k_ref[0, h, kb * ks:(kb + 1) * ks, :], qt_refs[a][0, h]).reshape(ks // 8, 8, qs)

    st = scores(*units[0])
    for n, (kb, h, a) in enumerate(units):
        q0 = a * qs
        st_next = scores(*units[n + 1]) if n + 1 < len(units) else None
        m_prev = m_scr[h, :, q0:q0 + qs]
        m_new = jnp.maximum(m_prev, jnp.max(jnp.max(st, axis=0), axis=0, keepdims=True))
        p = jnp.exp2(st - m_new[None]).reshape(ks, qs).astype(MXU_DTYPE)
        alpha = jnp.exp2(m_prev - m_new)
        pv = _dot(vt_refs[kb][0, h], p)
        acc = acc_scr[h, :, q0:q0 + qs].reshape(C_VT // 8, 8, qs) * alpha[None]
        acc_scr[h, :, q0:q0 + qs] = acc.reshape(C_VT, qs) + pv
        m_scr[h, :, q0:q0 + qs] = m_new
        st = st_next

    @pl.when(j == pl.num_programs(2) - 1)
    def _():
        for h in range(C_HEADS):
            acc = acc_scr[h]
            o = (acc[:C_V] / acc[C_V:C_V + 1]).T
            zh = z_ref[:, h * C_V:(h + 1) * C_V]
            o_ref[:, h * C_V:(h + 1) * C_V] = (o * _silu(zh)).astype(o_ref.dtype)


def _flash(qt, k, vt, cin, *, batch, seq):
    bq, bk = min(BQ, seq), min(BK, seq)
    qs, ks = min(QSUB, bq), min(KSUB, bk)
    n_sub, n_ksub = bq // qs, bk // ks
    nq, nk = seq // bq, seq // bk
    d = C_HEADS * C_V
    qt_specs = [pl.BlockSpec((1, C_HEADS, C_QK, qs), functools.partial(lambda b, i, j, a: (b, 0, 0, i * n_sub + a), a=a))
                for a in range(n_sub)]
    vt_specs = [pl.BlockSpec((1, C_HEADS, C_VT, ks), functools.partial(lambda b, i, j, a: (b, 0, 0, j * n_ksub + a), a=a))
                for a in range(n_ksub)]
    return pl.pallas_call(
        functools.partial(_flash_kernel, n_sub=n_sub, n_ksub=n_ksub),
        grid=(batch, nq, nk),
        in_specs=qt_specs + vt_specs + [
                  pl.BlockSpec((1, C_HEADS, bk, C_QK), lambda b, i, j: (b, 0, j, 0)),
                  pl.BlockSpec((bq, d), lambda b, i, j: (b * nq + i, C_Z // d))],
        out_specs=pl.BlockSpec((bq, d), lambda b, i, j: (b * nq + i, 0)),
        out_shape=jax.ShapeDtypeStruct((batch * seq, d), MXU_DTYPE),
        scratch_shapes=[pltpu.VMEM((C_HEADS, 8, bq), F32), pltpu.VMEM((C_HEADS, C_VT, bq), F32)],
        compiler_params=_cparams(("parallel", "parallel", "arbitrary"), 56),
        name="mla_flash",
    )(*([qt] * n_sub), *([vt] * n_ksub), k, cin)


def _split_cols(w, sizes):
    idx = np.cumsum(np.array(sizes))[:-1].tolist()
    return jnp.split(w, idx, axis=-1)


def _rope_swap(w):
    half = w.shape[-1] // 2
    return jnp.concatenate([-w[..., half:], w[..., :half]], axis=-1)


def _pad_cols(w, n):
    return jnp.pad(w, [(0, 0)] * (w.ndim - 1) + [(0, n - w.shape[-1])])


def _prep_ab(w_in, alpha_w2, alpha_b, igate_b, fgate_b):
    sizes = (2 * A_HEADS * A_DK, A_HEADS * A_DV, 2 * A_HEADS, 2 * A_HEADS, A_HEADS * A_DV, A_HEADS * A_DV,
             B_HEADS * B_DK, B_HEADS * B_DK, B_HEADS * B_DV, 2 * B_GATE_RANK, B_HEADS * B_DV)
    a_qk, a_v, a_i, a_f, a_o, a_z, b_q, b_k, b_v, b_low, b_z = _split_cols(w_in, sizes)
    w_small = _pad_cols(jnp.concatenate([a_i, a_f, b_low], axis=-1), LANES).astype(MXU_DTYPE)
    w = jnp.concatenate([a_qk, a_o, a_z, b_q, b_k, b_z, a_v, b_v], axis=-1).astype(MXU_DTYPE)
    assert w.shape[-1] == AB_SLABS * SLAB
    hk = B_HEADS * B_DK
    w2 = jnp.zeros((LANES, 2 * hk), F32)
    w2 = w2.at[SM_LOW:SM_LOW + B_GATE_RANK, :hk].set(alpha_w2[0])
    w2 = w2.at[SM_LOW + B_GATE_RANK:SM_LOW + 2 * B_GATE_RANK, hk:].set(alpha_w2[1])
    ab = alpha_b.reshape(1, 2 * hk)
    gbias = _pad_cols(jnp.concatenate([igate_b.reshape(1, -1), fgate_b.reshape(1, -1)], axis=-1), LANES)
    return w, w_small, w2.astype(MXU_DTYPE), ab, gbias


def _prep_c(w_in, w_q_up, w_kv_up):
    q_lat, kv_lat, k_rope, z = _split_cols(w_in, (C_Q_LORA, C_KV_LORA, C_ROPE, C_HEADS * C_V))
    w_c = jnp.concatenate([z, kv_lat, _pad_cols(k_rope, LANES), _pad_cols(_rope_swap(k_rope), LANES), q_lat],
                          axis=-1).astype(MXU_DTYPE)
    wq = w_q_up.reshape(C_Q_LORA, C_HEADS, C_NOPE + C_ROPE)
    nope, rope = wq[..., :C_NOPE], wq[..., C_NOPE:]
    wq_t = jnp.transpose(jnp.concatenate([nope, rope, _rope_swap(rope)], axis=-1), (1, 2, 0))
    wkv = w_kv_up.reshape(C_KV_LORA, C_HEADS, C_NOPE + C_V)
    wk = jnp.transpose(wkv[..., :C_NOPE], (1, 0, 2))
    wv_t = jnp.transpose(wkv[..., C_NOPE:], (1, 2, 0))
    return w_c, wq_t.astype(MXU_DTYPE), wk.astype(MXU_DTYPE), wv_t.astype(MXU_DTYPE)


def _rope_tables(seq):
    inv = ROPE_BASE ** (-jnp.arange(0, C_ROPE, 2, dtype=F32) / C_ROPE)
    ang = jnp.arange(seq, dtype=F32)[:, None] * inv[None, :]
    cos, sin = jnp.cos(ang), jnp.sin(ang)
    cos2, sin2 = jnp.concatenate([cos, cos], axis=-1), jnp.concatenate([sin, sin], axis=-1)
    return _pad_cols(cos2, LANES), _pad_cols(sin2, LANES), cos2.T, sin2.T


def _trunk(x, mem, p, prep):
    batch, seq, d = x.shape
    x = x.reshape(batch * seq, d)
    mem = mem.reshape(batch * N_MEM, d)
    cos, sin, cos_t, sin_t = _rope_tables(seq)
    for l in range(DEPTH):
        j = l // 2
        if l % 2 == 0:
            w_in, w_small, w2, ab, gbias = prep['ab'][j]
            proj, small = _ab_proj(x, p['norm_mix'][l], w_in, w_small)
            qk, kt = _conv_silu(proj, p['ab_conv_w'][j], p['ab_conv_b'][j], seq=seq)
            h_f, h_b = _mlstm(qk, kt, proj, small, gbias, batch=batch, seq=seq)
            o_f, o_b = _gla(proj, small, w2, ab, batch=batch, seq=seq)
            x = _ab_post(h_f, h_b, o_f, o_b, proj, p['a_ogate_b'][j], p['a_head_norm'][j], p['b_head_norm'][j],
                         prep['ab_w_out'][j], x)
        else:
            w_c, wq_t, wk, wv_t = prep['c'][j]
            cin = _norm_matmul(x, p['norm_mix'][l], w_c, tm=TM_PROJ, tn=C_TN, out_dtype=F32)
            qt = _q_up(cin, p['c_q_norm'][j], wq_t, cos_t, sin_t, batch=batch, seq=seq)
            k, vt = _kv_up(cin, p['c_kv_norm'][j], wk, wv_t, cos, sin, batch=batch, seq=seq)
            og = _flash(qt, k, vt, cin, batch=batch, seq=seq)
            x = _matmul_residual(og, prep['c_w_out'][j], x, tm=TM_OUT)
        kv = _norm_matmul(mem, p['norm_mem'][l], prep['x_w_kv'][l], tm=N_MEM, tn=1024, out_dtype=MXU_DTYPE)
        x = _cross_attn(x, kv, p['norm_cross'][l], prep['x_w_q'][l], prep['x_w_o'][l], p['final_norm'],
                        seq=seq, final=(l == DEPTH - 1))
    return x.reshape(batch, seq, d)


def kernel(x_prompt, x_sample, mem_prompt, mem_sample, norm_mix, norm_cross, norm_mem, ab_w_in, ab_conv_w, ab_conv_b, a_igate_b, a_fgate_b, a_ogate_b, a_head_norm, b_alpha_w2, b_alpha_b, b_head_norm, ab_w_out, c_w_in, c_q_norm, c_kv_norm, c_w_q_up, c_w_kv_up, c_w_out, x_w_q, x_w_kv, x_w_o, final_norm):
    p = dict(norm_mix=norm_mix, norm_cross=norm_cross, norm_mem=norm_mem, ab_conv_w=ab_conv_w,
             ab_conv_b=ab_conv_b, a_ogate_b=a_ogate_b, a_head_norm=a_head_norm, b_head_norm=b_head_norm,
             c_q_norm=c_q_norm, c_kv_norm=c_kv_norm, final_norm=final_norm)
    n_even, n_odd = ab_w_in.shape[0], c_w_in.shape[0]
    prep = dict(
        ab=[_prep_ab(ab_w_in[j], b_alpha_w2[j], b_alpha_b[j], a_igate_b[j], a_fgate_b[j]) for j in range(n_even)],
        ab_w_out=[ab_w_out[j].astype(MXU_DTYPE) for j in range(n_even)],
        c=[_prep_c(c_w_in[j], c_w_q_up[j], c_w_kv_up[j]) for j in range(n_odd)],
        c_w_out=[c_w_out[j].astype(MXU_DTYPE) for j in range(n_odd)],
        x_w_q=[x_w_q[l].astype(MXU_DTYPE) for l in range(DEPTH)],
        x_w_kv=[x_w_kv[l].astype(MXU_DTYPE) for l in range(DEPTH)],
        x_w_o=[x_w_o[l].astype(MXU_DTYPE) for l in range(DEPTH)],
    )
    return (_trunk(x_prompt, mem_prompt, p, prep), _trunk(x_sample, mem_sample, p, prep))
```

```python
import functools

import numpy as np
import jax
import jax.numpy as jnp
from jax import lax
from jax.experimental import pallas as pl
from jax.experimental.pallas import tpu as pltpu

F32 = jnp.float32
MXU_DTYPE = jnp.bfloat16
EPS = 1e-6
HIGHEST = lax.Precision.HIGHEST

D_MODEL = 1024
DEPTH = 4
A_HEADS, A_DK, A_DV, A_CHUNK = 4, 128, 256, 128
B_HEADS, B_DK, B_DV, B_CHUNK = 4, 128, 256, 64
B_GATE_RANK, B_GATE_TAU = 16, 16.0
B_SUB = 16
B_CHUNKS_PER_STEP = 4
C_HEADS, C_Q_LORA, C_KV_LORA, C_NOPE, C_ROPE, C_V = 8, 384, 256, 128, 64, 128
ROPE_BASE = 10000.0
X_HEADS, X_DH, N_MEM = 4, 256, 256

LANES = 128
V7X_VMEM_BYTES = 64 * 1024 * 1024

SLAB = 512
AB_QK, AB_O, AB_Z, AB_BQ, AB_BK, AB_BZ, AB_AV, AB_BV = 0, 2, 4, 6, 7, 8, 10, 12
AB_SLABS = 14
AB_SLABS_PER_STEP = 7
SM_I, SM_F, SM_LOW = 0, 8, 16
C_Z, C_KV, C_KR, C_KSW, C_QL = 0, 1024, 1280, 1408, 1536
C_COLS = 1920
C_TN = 640

TM_PROJ = 1024
TM_CONV = 512
TM_POST = 512
TM_CROSS = 1024
TM_UP = 512
TM_OUT = 1024
BQ, BK, QSUB, KSUB = 1024, 1024, 512, 512


def _cparams(sem, vmem_mb=48):
    assert vmem_mb * 1024 * 1024 <= V7X_VMEM_BYTES
    return pltpu.CompilerParams(dimension_semantics=sem, vmem_limit_bytes=vmem_mb * 1024 * 1024)


def _sigmoid(x):
    return 1.0 / (1.0 + jnp.exp(-x))


def _silu(x):
    return x * _sigmoid(x)


def _log_sigmoid(x):
    return jnp.minimum(x, 0.0) - jnp.log1p(jnp.exp(-jnp.abs(x)))


def _dot(a, b):
    return jnp.dot(a, b, preferred_element_type=F32)


def _dot_nt(a, b):
    return lax.dot_general(a, b, (((1,), (1,)), ((), ())), preferred_element_type=F32)


def _dot_tn(a, b):
    return lax.dot_general(a, b, (((0,), (0,)), ((), ())), preferred_element_type=F32)


def _rms(x, g):
    ms = jnp.mean(x * x, axis=-1, keepdims=True)
    return x * lax.rsqrt(ms + EPS) * g


def _head_rms(x, g, n_heads):
    w = x.shape[-1] // n_heads
    parts = []
    for h in range(n_heads):
        xs = x[:, h * w:(h + 1) * w]
        ms = jnp.mean(xs * xs, axis=-1, keepdims=True)
        parts.append(xs * lax.rsqrt(ms + EPS))
    return jnp.concatenate(parts, axis=-1) * g


def _norm_matmul_kernel(x_ref, g_ref, w_ref, o_ref, h_ref):
    @pl.when(pl.program_id(1) == 0)
    def _():
        h_ref[...] = _rms(x_ref[...], g_ref[...]).astype(h_ref.dtype)

    o_ref[...] = _dot(h_ref[...], w_ref[...]).astype(o_ref.dtype)


def _norm_matmul(x, g, w, *, tm, tn, out_dtype, x_col_block=0):
    m = x.shape[0]
    k, n = w.shape
    tm = min(tm, m)
    return pl.pallas_call(
        _norm_matmul_kernel,
        grid=(m // tm, n // tn),
        in_specs=[pl.BlockSpec((tm, k), lambda i, j: (i, x_col_block)),
                  pl.BlockSpec((1, k), lambda i, j: (0, 0)),
                  pl.BlockSpec((k, tn), lambda i, j: (0, j))],
        out_specs=pl.BlockSpec((tm, tn), lambda i, j: (i, j)),
        out_shape=jax.ShapeDtypeStruct((m, n), out_dtype),
        scratch_shapes=[pltpu.VMEM((tm, k), MXU_DTYPE)],
        compiler_params=_cparams(("parallel", "arbitrary"), 56),
        name="norm_matmul",
    )(x, g.reshape(1, k), w)


def _ab_proj_kernel(x_ref, g_ref, w_ref, ws_ref, o_ref, os_ref, h_ref):
    @pl.when(pl.program_id(1) == 0)
    def _():
        h_ref[...] = _rms(x_ref[...], g_ref[...]).astype(h_ref.dtype)
        os_ref[...] = _dot(h_ref[...], ws_ref[...])

    for t in range(o_ref.shape[0]):
        o_ref[t] = _dot(h_ref[...], w_ref[:, t * SLAB:(t + 1) * SLAB]).astype(o_ref.dtype)


def _ab_proj(x, g, w, w_small):
    m, k = x.shape
    tm = min(TM_PROJ, m)
    ns = AB_SLABS_PER_STEP
    return pl.pallas_call(
        _ab_proj_kernel,
        grid=(m // tm, AB_SLABS // ns),
        in_specs=[pl.BlockSpec((tm, k), lambda i, j: (i, 0)),
                  pl.BlockSpec((1, k), lambda i, j: (0, 0)),
                  pl.BlockSpec((k, ns * SLAB), lambda i, j: (0, j)),
                  pl.BlockSpec((k, LANES), lambda i, j: (0, 0))],
        out_specs=[pl.BlockSpec((ns, tm, SLAB), lambda i, j: (j, i, 0)),
                   pl.BlockSpec((tm, LANES), lambda i, j: (i, 0))],
        out_shape=[jax.ShapeDtypeStruct((AB_SLABS, m, SLAB), MXU_DTYPE),
                   jax.ShapeDtypeStruct((m, LANES), F32)],
        scratch_shapes=[pltpu.VMEM((tm, k), MXU_DTYPE)],
        compiler_params=_cparams(("parallel", "arbitrary"), 56),
        name="ab_proj",
    )(x, g.reshape(1, k), w, w_small)


def _matmul_res_kernel(a_ref, w_ref, x_ref, o_ref):
    o_ref[...] = x_ref[...] + _dot(a_ref[...], w_ref[...])


def _matmul_residual(a, w, x, *, tm):
    m, k = a.shape
    n = w.shape[1]
    tm = min(tm, m)
    return pl.pallas_call(
        _matmul_res_kernel,
        grid=(m // tm,),
        in_specs=[pl.BlockSpec((tm, k), lambda i: (i, 0)),
                  pl.BlockSpec((k, n), lambda i: (0, 0)),
                  pl.BlockSpec((tm, n), lambda i: (i, 0))],
        out_specs=pl.BlockSpec((tm, n), lambda i: (i, 0)),
        out_shape=jax.ShapeDtypeStruct((m, n), F32),
        compiler_params=_cparams(("parallel",), 40),
        name="matmul_residual",
    )(a, w, x)


def _conv_kernel(x_ref, xp_ref, xn_ref, w_ref, b_ref, o_ref, kt_ref, *, tiles_per_seq):
    i = pl.program_id(0)
    ts = x_ref.shape[1]
    t_in_seq = i % tiles_per_seq
    rows = lax.broadcasted_iota(jnp.int32, (ts, SLAB), 0)
    for t in range(2):
        cs = slice(t * SLAB, (t + 1) * SLAB)
        x = x_ref[t].astype(F32)
        last = xp_ref.shape[1] - 1
        prev_row = jnp.where(t_in_seq == 0, 0.0, xp_ref[t, last:last + 1, :].astype(F32))
        next_row = jnp.where(t_in_seq == tiles_per_seq - 1, 0.0, xn_ref[t, 0:1, :].astype(F32))
        x_prev = jnp.where(rows == 0, prev_row, pltpu.roll(x, 1, axis=0))
        x_next = jnp.where(rows == ts - 1, next_row, pltpu.roll(x, ts - 1, axis=0))
        y = _silu(w_ref[0:1, cs] * x_prev + w_ref[1:2, cs] * x + w_ref[2:3, cs] * x_next + b_ref[:, cs])
        if t == 0:
            o_ref[:, cs] = (y * (A_DK ** -0.5)).astype(o_ref.dtype)
        else:
            o_ref[:, cs] = y.astype(o_ref.dtype)
            kt_ref[...] = y.T.astype(kt_ref.dtype)


def _conv_silu(proj, conv_w, conv_b, *, seq):
    m = proj.shape[1]
    c = conv_w.shape[1]
    assert c == 2 * SLAB and AB_QK == 0
    ts = min(TM_CONV, seq)
    nt = m // ts
    halo = 16
    sub = ts // halo
    return pl.pallas_call(
        functools.partial(_conv_kernel, tiles_per_seq=seq // ts),
        grid=(nt,),
        in_specs=[pl.BlockSpec((2, ts, SLAB), lambda i: (0, i, 0)),
                  pl.BlockSpec((2, halo, SLAB), lambda i: (0, jnp.maximum(i * sub - 1, 0), 0)),
                  pl.BlockSpec((2, halo, SLAB), lambda i: (0, jnp.minimum((i + 1) * sub, nt * sub - 1), 0)),
                  pl.BlockSpec((3, c), lambda i: (0, 0)),
                  pl.BlockSpec((1, c), lambda i: (0, 0))],
        out_specs=[pl.BlockSpec((ts, c), lambda i: (i, 0)),
                   pl.BlockSpec((c // 2, ts), lambda i: (0, i))],
        out_shape=[jax.ShapeDtypeStruct((m, c), MXU_DTYPE),
                   jax.ShapeDtypeStruct((c // 2, m), MXU_DTYPE)],
        compiler_params=_cparams(("parallel",), 32),
        name="conv_silu",
    )(proj, proj, proj, conv_w, conv_b.reshape(1, c))


def _mlstm_kernel(qk_f, kt_f, v_f, g_f, gn_f, qk_b, kt_b, v_b, g_b, gn_b, gbias_ref, h_f, h_b,
                  c_scr, n_scr, m_scr, gate_scr):
    L = A_CHUNK

    row = lax.broadcasted_iota(jnp.int32, (L, L), 0)
    col = lax.broadcasted_iota(jnp.int32, (L, L), 1)
    sub8 = lax.broadcasted_iota(jnp.int32, (8, L), 0)
    lane_row = lax.broadcasted_iota(jnp.int32, (1, LANES), 1)
    neg_inf = float("-inf")
    masks = (row >= col, row <= col)

    def gate_sums(gate_ref, d):
        gates = gate_ref[...] + gbias_ref[...]
        csum = jnp.dot(masks[d].astype(F32), _log_sigmoid(gates), precision=HIGHEST, preferred_element_type=F32)
        a_c = pltpu.roll(csum, LANES - (SM_F - SM_I), axis=1)
        e_c = gates - a_c
        run = e_c
        sh = 1
        while sh < L:
            pad = jnp.full((sh, LANES), neg_inf, F32)
            moved = (jnp.concatenate([pad, run[:L - sh]], axis=0) if d == 0
                     else jnp.concatenate([run[sh:], pad], axis=0))
            run = jnp.maximum(run, moved)
            sh *= 2
        return a_c, run, e_c.T

    def put_gate_sums(d, vals):
        for n, val in enumerate(vals):
            gate_scr[d, n] = val

    @pl.when(pl.program_id(1) == 0)
    def _():
        c_scr[...] = jnp.zeros_like(c_scr)
        n_scr[...] = jnp.zeros_like(n_scr)
        m_scr[...] = jnp.zeros_like(m_scr)
        put_gate_sums(0, gate_sums(g_f, 0))
        put_gate_sums(1, gate_sums(g_b, 1))

    m_all = m_scr[0:1, :]
    m_next = m_all
    dirs = ((qk_f, kt_f, v_f, g_f, h_f), (qk_b, kt_b, v_b, g_b, h_b))
    gate_next = [gate_sums(gn_f, 0), gate_sums(gn_b, 1)]
    pro = []
    for d in range(2):
        a_c, run, e_t = gate_scr[d, 0], gate_scr[d, 1], gate_scr[d, 2]
        m_c = jnp.maximum(run, m_all)
        w_inter_c = jnp.exp(m_all - m_c)
        inv_floor_c = jnp.exp(-(a_c + m_c))
        pro.append((masks[d], a_c, m_c, w_inter_c, inv_floor_c, e_t, m_c.T))

    for d, (qk_ref, kt_ref, v_ref, g_ref, h_ref) in enumerate(dirs):
        mask, a_c, m_c, w_inter_c, inv_floor_c, e_t, m_t_rows = pro[d]
        end = L - 1 if d == 0 else 0
        for h in range(A_HEADS):
            s = d * A_HEADS + h
            n_prev = n_scr[s]
            c_prev = c_scr[s]
            q = qk_ref[:, h * A_DK:(h + 1) * A_DK]
            k = qk_ref[:, (A_HEADS + h) * A_DK:(A_HEADS + h + 1) * A_DK]
            kt = kt_ref[h * A_DK:(h + 1) * A_DK, :]
            v0 = h * A_DV
            v = v_ref[v0 // SLAB, :, v0 % SLAB:v0 % SLAB + A_DV]
            e_r, m_r = e_t[s:s + 1, :], m_t_rows[s:s + 1, :]

            lhs = jnp.where(sub8 == 0, -m_r, jnp.where(sub8 == 1, 1.0, 0.0))
            rhs = jnp.where(sub8 == 0, 1.0, jnp.where(sub8 == 1, e_r, 0.0))
            x = lax.dot_general(lhs, rhs, (((0,), (0,)), ((), ())), precision=HIGHEST,
                                preferred_element_type=F32)
            sc = _dot_nt(q, k) * jnp.exp(jnp.where(mask, x, neg_inf))
            w_inter = w_inter_c[:, s:s + 1]
            num = _dot(sc.astype(MXU_DTYPE), v) + w_inter * _dot(q, c_prev.astype(MXU_DTYPE))
            qn = _dot_nt(q, n_prev.astype(MXU_DTYPE))[:, 0:1]
            den = jnp.sum(sc, axis=1, keepdims=True) + w_inter * qn
            hv = num / jnp.maximum(jnp.abs(den), inv_floor_c[:, s:s + 1])
            h_ref[:, h * A_DV:(h + 1) * A_DV] = hv.astype(h_ref.dtype)

            g = a_c[end:end + 1, s:s + 1]
            m_end = m_c[end:end + 1, s:s + 1]
            w = jnp.exp(e_r - m_end)
            decay = w_inter_c[end:end + 1, s:s + 1]
            kwt = (kt.astype(F32) * w).astype(MXU_DTYPE)
            c_scr[s] = decay * c_prev + _dot(kwt, v)
            n_scr[s] = decay * n_prev + _dot(jnp.broadcast_to(w, (8, L)).astype(MXU_DTYPE), k)
            m_next = jnp.where(lane_row == s, g + m_end, m_next)
    m_scr[...] = jnp.broadcast_to(m_next, m_scr.shape)
    put_gate_sums(0, gate_next[0])
    put_gate_sums(1, gate_next[1])


def _mlstm(qk, kt, proj, small, gbias, *, batch, seq):
    m = qk.shape[0]
    L = A_CHUNK
    nc = seq // L
    d_a = A_HEADS * A_DV
    v_slabs = d_a // SLAB
    v_blk = AB_AV // v_slabs
    assert L == LANES
    fwd = lambda b, c: b * nc + c
    bwd = lambda b, c: b * nc + (nc - 1 - c)
    nxt = lambda c: jnp.minimum(c + 1, nc - 1)
    out = jax.ShapeDtypeStruct((m, d_a), MXU_DTYPE)
    return pl.pallas_call(
        _mlstm_kernel,
        grid=(batch, nc),
        in_specs=[pl.BlockSpec((L, 2 * A_HEADS * A_DK), lambda b, c: (fwd(b, c), 0)),
                  pl.BlockSpec((A_HEADS * A_DK, L), lambda b, c: (0, fwd(b, c))),
                  pl.BlockSpec((v_slabs, L, SLAB), lambda b, c: (v_blk, fwd(b, c), 0)),
                  pl.BlockSpec((L, LANES), lambda b, c: (fwd(b, c), 0)),
                  pl.BlockSpec((L, LANES), lambda b, c: (fwd(b, nxt(c)), 0)),
                  pl.BlockSpec((L, 2 * A_HEADS * A_DK), lambda b, c: (bwd(b, c), 0)),
                  pl.BlockSpec((A_HEADS * A_DK, L), lambda b, c: (0, bwd(b, c))),
                  pl.BlockSpec((v_slabs, L, SLAB), lambda b, c: (v_blk, bwd(b, c), 0)),
                  pl.BlockSpec((L, LANES), lambda b, c: (bwd(b, c), 0)),
                  pl.BlockSpec((L, LANES), lambda b, c: (bwd(b, nxt(c)), 0)),
                  pl.BlockSpec((1, LANES), lambda b, c: (0, 0))],
        out_specs=[pl.BlockSpec((L, d_a), lambda b, c: (fwd(b, c), 0)),
                   pl.BlockSpec((L, d_a), lambda b, c: (bwd(b, c), 0))],
        out_shape=[out, out],
        scratch_shapes=[pltpu.VMEM((2 * A_HEADS, A_DK, A_DV), F32),
                        pltpu.VMEM((2 * A_HEADS, 8, A_DK), F32),
                        pltpu.VMEM((8, LANES), F32),
                        pltpu.VMEM((2, 3, L, LANES), F32)],
        compiler_params=_cparams(("parallel", "arbitrary"), 32),
        name="mlstm",
    )(qk, kt, proj, small, small, qk, kt, proj, small, small, gbias)


def _gla_kernel(qk_f, v_f, g_f, gn_f, qk_b, v_b, g_b, gn_b, w2_ref, ab_ref, o_f, o_b, st_scr, bc_scr):
    L, SB = B_CHUNK, B_SUB
    nsb = L // SB
    hk = B_HEADS * B_DK
    rows_per_step = g_f.shape[0]
    n_chunks = rows_per_step // L

    row = lax.broadcasted_iota(jnp.int32, (L, L), 0)
    col = lax.broadcasted_iota(jnp.int32, (L, L), 1)
    lane8 = lax.broadcasted_iota(jnp.int32, (8, L), 1)
    log2e = float(np.log2(np.e))
    dirs = ((qk_f, v_f, o_f), (qk_b, v_b, o_b))
    masks = (row >= col, row <= col)

    def decay_sums(gate_ref, d):
        pre = _dot(gate_ref[...].astype(MXU_DTYPE), w2_ref[:, d * hk:(d + 1) * hk]) + ab_ref[:, d * hk:(d + 1) * hk]
        log_a = _log_sigmoid(pre) / B_GATE_TAU
        tri = masks[d].astype(F32)
        return jnp.concatenate(
            [jnp.dot(tri, log_a[u * L:(u + 1) * L], precision=HIGHEST, preferred_element_type=F32)
             for u in range(n_chunks)], axis=0)

    @pl.when(pl.program_id(1) == 0)
    def _():
        st_scr[...] = jnp.zeros_like(st_scr)
        bc_scr[0] = decay_sums(g_f, 0)
        bc_scr[1] = decay_sums(g_b, 1)

    bc_all = [bc_scr[0], bc_scr[1]]
    bc_next = [decay_sums(gn_f, 0), decay_sums(gn_b, 1)]

    def chunk(r0, states):
        work = []
        for d, (qk_ref, v_ref, o_ref) in enumerate(dirs):
            rs = slice(r0[d], r0[d] + L)
            for h in range(B_HEADS):
                s = d * B_HEADS + h
                q = qk_ref[0, rs, h * B_DK:(h + 1) * B_DK].astype(F32) * (B_DK ** -0.5)
                k = qk_ref[1, rs, h * B_DK:(h + 1) * B_DK].astype(F32)
                b = bc_all[d][rs, h * B_DK:(h + 1) * B_DK]
                st = states[s]
                inter = _dot_nt((q * jnp.exp(b)).astype(MXU_DTYPE), st.astype(MXU_DTYPE))

                b2 = b * log2e
                b2_rows = [jnp.broadcast_to(b2[j:j + 1], (8, B_DK)) for j in range(L)]
                pieces, where_to = [], []
                for blk in range(nsb):
                    lo = blk * SB
                    for oc in range(SB // 8):
                        t0 = lo + 8 * oc
                        js = range(0, 8 * oc + 8) if d == 0 else range(8 * oc, SB)
                        for j in js:
                            pieces.append(q[t0:t0 + 8] * jnp.exp2(b2[t0:t0 + 8] - b2_rows[lo + j]))
                            where_to.append((t0, lo + j))
                diag = _dot_nt(jnp.concatenate(pieces, axis=0).astype(MXU_DTYPE), k.astype(MXU_DTYPE))

                offs = []
                for blk in range(nsb):
                    lo, hi = blk * SB, (blk + 1) * SB
                    if d == 0 and blk > 0:
                        ref = b[lo - 1:lo]
                        kt = k[:lo] * jnp.exp(ref - b[:lo])
                        kt = jnp.concatenate([kt, jnp.zeros((L - lo, B_DK), F32)], axis=0)
                    elif d == 1 and blk < nsb - 1:
                        ref = b[hi:hi + 1]
                        kt = k[hi:] * jnp.exp(ref - b[hi:])
                        kt = jnp.concatenate([jnp.zeros((hi, B_DK), F32), kt], axis=0)
                    else:
                        offs.append(jnp.zeros((SB, L), F32))
                        continue
                    offs.append(_dot_nt((q[lo:hi] * jnp.exp(b[lo:hi] - ref)).astype(MXU_DTYPE),
                                        kt.astype(MXU_DTYPE)))
                work.append((d, h, s, rs, k, b, st, inter, diag, where_to, offs))

        for d, h, s, rs, k, b, st, inter, diag, where_to, offs in work:
            v = dirs[d][1][h * B_DV // SLAB, rs, h * B_DV % SLAB:h * B_DV % SLAB + B_DV]
            att_rows = []
            for blk in range(nsb):
                for oc in range(SB // 8):
                    t0 = blk * SB + 8 * oc
                    a8 = offs[blk][8 * oc:8 * oc + 8]
                    for n, (rr, cc) in enumerate(where_to):
                        if rr == t0:
                            a8 = jnp.where(lane8 == cc, diag[8 * n:8 * n + 8], a8)
                    att_rows.append(a8)
            att = jnp.where(masks[d], jnp.concatenate(att_rows, axis=0), 0.0)
            o = inter + _dot(att.astype(MXU_DTYPE), v)
            dirs[d][2][rs, h * B_DV:(h + 1) * B_DV] = o.astype(dirs[d][2].dtype)

        new_states = list(states)
        for d, h, s, rs, k, b, st, inter, diag, where_to, offs in work:
            v = dirs[d][1][h * B_DV // SLAB, rs, h * B_DV % SLAB:h * B_DV % SLAB + B_DV]
            g = b[L - 1:L, :] if d == 0 else b[0:1, :]
            kd = (k * jnp.exp(g - b)).astype(MXU_DTYPE)
            new_states[s] = jnp.exp(g) * st + _dot_tn(v, kd)
        return new_states

    states = [st_scr[s] for s in range(2 * B_HEADS)]
    for u in range(n_chunks):
        states = chunk((u * L, (n_chunks - 1 - u) * L), states)
    for s in range(2 * B_HEADS):
        st_scr[s] = states[s]
    bc_scr[0] = bc_next[0]
    bc_scr[1] = bc_next[1]


def _gla(proj, small, w2, ab, *, batch, seq):
    m = proj.shape[1]
    rows = min(B_CHUNKS_PER_STEP * B_CHUNK, seq)
    ns = seq // rows
    hk = B_HEADS * B_DK
    d_b = B_HEADS * B_DV
    assert hk == SLAB and AB_BK == AB_BQ + 1 and AB_BQ % 2 == 0
    v_slabs = d_b // SLAB
    fwd = lambda b, c: b * ns + c
    bwd = lambda b, c: b * ns + (ns - 1 - c)

    def specs(idx):
        nxt = lambda b, c: idx(b, jnp.minimum(c + 1, ns - 1))
        return [pl.BlockSpec((2, rows, SLAB), lambda b, c: (AB_BQ // 2, idx(b, c), 0)),
                pl.BlockSpec((v_slabs, rows, SLAB), lambda b, c: (AB_BV // v_slabs, idx(b, c), 0)),
                pl.BlockSpec((rows, LANES), lambda b, c: (idx(b, c), 0)),
                pl.BlockSpec((rows, LANES), lambda b, c: (nxt(b, c), 0))]

    out = jax.ShapeDtypeStruct((m, d_b), MXU_DTYPE)
    return pl.pallas_call(
        _gla_kernel,
        grid=(batch, ns),
        in_specs=specs(fwd) + specs(bwd) + [pl.BlockSpec((LANES, 2 * hk), lambda b, c: (0, 0)),
                                            pl.BlockSpec((1, 2 * hk), lambda b, c: (0, 0))],
        out_specs=[pl.BlockSpec((rows, d_b), lambda b, c: (fwd(b, c), 0)),
                   pl.BlockSpec((rows, d_b), lambda b, c: (bwd(b, c), 0))],
        out_shape=[out, out],
        scratch_shapes=[pltpu.VMEM((2 * B_HEADS, B_DV, B_DK), F32),
                        pltpu.VMEM((2, rows, hk), F32)],
        compiler_params=_cparams(("parallel", "arbitrary"), 32),
        name="gla",
    )(proj, proj, small, small, proj, proj, small, small, w2, ab)


def _ab_post_kernel(hf_ref, hb_ref, ao_ref, az_ref, of_ref, ob_ref, bz_ref, ogb_ref, an_ref, bn_ref,
                    w_ref, x_ref, o_ref):
    d_a = A_HEADS * A_DV

    def slabs(ref):
        return jnp.concatenate([ref[t] for t in range(ref.shape[0])], axis=-1).astype(F32)

    h_sum = hf_ref[...].astype(F32) + hb_ref[...].astype(F32)
    out_a = _sigmoid(slabs(ao_ref) + ogb_ref[...]) * h_sum
    out_a = _head_rms(out_a, an_ref[...], A_HEADS) * _silu(slabs(az_ref))
    o_sum = of_ref[...].astype(F32) + ob_ref[...].astype(F32)
    out_b = _head_rms(o_sum, bn_ref[...], B_HEADS) * _silu(slabs(bz_ref))
    y = _dot(out_a.astype(MXU_DTYPE), w_ref[:d_a, :]) + _dot(out_b.astype(MXU_DTYPE), w_ref[d_a:, :])
    o_ref[...] = x_ref[...] + y


def _ab_post(h_f, h_b, o_f, o_b, proj, ogate_b, a_norm, b_norm, w_out, x):
    m, d = x.shape
    tm = min(TM_POST, m)
    row = lambda i: (i, 0)
    vec = pl.BlockSpec((1, d), lambda i: (0, 0))
    ns = d // SLAB
    blk = lambda slab: pl.BlockSpec((ns, tm, SLAB), lambda i: (slab // ns, i, 0))
    return pl.pallas_call(
        _ab_post_kernel,
        grid=(m // tm,),
        in_specs=[pl.BlockSpec((tm, d), row), pl.BlockSpec((tm, d), row), blk(AB_O), blk(AB_Z),
                  pl.BlockSpec((tm, d), row), pl.BlockSpec((tm, d), row), blk(AB_BZ),
                  vec, vec, vec, pl.BlockSpec(w_out.shape, lambda i: (0, 0)), pl.BlockSpec((tm, d), row)],
        out_specs=pl.BlockSpec((tm, d), row),
        out_shape=jax.ShapeDtypeStruct((m, d), F32),
        compiler_params=_cparams(("parallel",), 48),
        name="ab_post",
    )(h_f, h_b, proj, proj, o_f, o_b, proj, ogate_b.reshape(1, d), a_norm.reshape(1, d), b_norm.reshape(1, d),
      w_out, x)


def _cross_kernel(x_ref, g_ref, wq_ref, kv_ref, wo_ref, fin_ref, o_ref, *, final):
    x = x_ref[...]
    hn = _rms(x, g_ref[...]).astype(MXU_DTYPE)
    q = _dot(hn, wq_ref[...]).astype(MXU_DTYPE)
    d = X_HEADS * X_DH
    outs = []
    for h in range(X_HEADS):
        k = kv_ref[:, h * X_DH:(h + 1) * X_DH]
        v = kv_ref[:, d + h * X_DH:d + (h + 1) * X_DH]
        sc = _dot_nt(q[:, h * X_DH:(h + 1) * X_DH], k) * (X_DH ** -0.5)
        e = jnp.exp(sc - jnp.max(sc, axis=-1, keepdims=True))
        p = e / jnp.sum(e, axis=-1, keepdims=True)
        outs.append(_dot(p.astype(MXU_DTYPE), v))
    o = jnp.concatenate(outs, axis=-1).astype(MXU_DTYPE)
    y = x + _dot(o, wo_ref[...])
    if final:
        y = _rms(y, fin_ref[...])
    o_ref[...] = y


def _cross_attn(x, kv, g, w_q, w_o, fin, *, seq, final):
    m, d = x.shape
    tm = min(TM_CROSS, seq)
    tiles_per_seq = seq // tm
    full = lambda i: (0, 0)
    return pl.pallas_call(
        functools.partial(_cross_kernel, final=final),
        grid=(m // tm,),
        in_specs=[pl.BlockSpec((tm, d), lambda i: (i, 0)),
                  pl.BlockSpec((1, d), full),
                  pl.BlockSpec(w_q.shape, full),
                  pl.BlockSpec((N_MEM, kv.shape[1]), lambda i: (i // tiles_per_seq, 0)),
                  pl.BlockSpec(w_o.shape, full),
                  pl.BlockSpec((1, d), full)],
        out_specs=pl.BlockSpec((tm, d), lambda i: (i, 0)),
        out_shape=jax.ShapeDtypeStruct((m, d), F32),
        compiler_params=_cparams(("parallel",), 56),
        name="cross_attn",
    )(x, g.reshape(1, d), w_q, kv, w_o, fin.reshape(1, d))


C_QK = C_NOPE + LANES


C_EXP2 = (C_NOPE + C_ROPE) ** -0.5 * float(np.log2(np.e))
C_VT = C_V + 8


def _q_up_kernel(x_ref, g_ref, w_ref, cos_ref, sin_ref, o_ref):
    hn = _rms(x_ref[...], g_ref[...]).astype(MXU_DTYPE)
    pad = jnp.zeros((C_QK - C_NOPE - C_ROPE, hn.shape[0]), o_ref.dtype)
    for h in range(C_HEADS):
        rt = _dot_nt(w_ref[h], hn) * C_EXP2
        o_ref[0, h, :C_NOPE, :] = rt[:C_NOPE].astype(o_ref.dtype)
        rope = rt[C_NOPE:C_NOPE + C_ROPE] * cos_ref[...] + rt[C_NOPE + C_ROPE:] * sin_ref[...]
        o_ref[0, h, C_NOPE:C_NOPE + C_ROPE, :] = rope.astype(o_ref.dtype)
        o_ref[0, h, C_NOPE + C_ROPE:, :] = pad


def _q_up(cin, q_norm, w, cos_t, sin_t, *, batch, seq):
    tm = min(TM_UP, seq)
    tps = seq // tm
    return pl.pallas_call(
        _q_up_kernel,
        grid=(batch * tps,),
        in_specs=[pl.BlockSpec((tm, C_Q_LORA), lambda i: (i, C_QL // C_Q_LORA)),
                  pl.BlockSpec((1, C_Q_LORA), lambda i: (0, 0)),
                  pl.BlockSpec(w.shape, lambda i: (0, 0, 0)),
                  pl.BlockSpec((C_ROPE, tm), lambda i: (0, i % tps)),
                  pl.BlockSpec((C_ROPE, tm), lambda i: (0, i % tps))],
        out_specs=pl.BlockSpec((1, C_HEADS, C_QK, tm), lambda i: (i // tps, 0, 0, i % tps)),
        out_shape=jax.ShapeDtypeStruct((batch, C_HEADS, C_QK, seq), MXU_DTYPE),
        compiler_params=_cparams(("parallel",), 32),
        name="mla_q_up",
    )(cin, q_norm.reshape(1, C_Q_LORA), w, cos_t, sin_t)


def _kv_up_kernel(x_ref, g_ref, wk_ref, wvt_ref, kr_ref, ksw_ref, cos_ref, sin_ref, k_ref, vt_ref):
    hn = _rms(x_ref[...], g_ref[...]).astype(MXU_DTYPE)
    rope = (kr_ref[...] * cos_ref[...] + ksw_ref[...] * sin_ref[...]).astype(k_ref.dtype)
    tm = hn.shape[0]
    ones_rows = (lax.broadcasted_iota(jnp.int32, (C_VT - C_V, tm), 0) == 0).astype(vt_ref.dtype)
    for h in range(C_HEADS):
        k_ref[0, h, :, :C_NOPE] = _dot(hn, wk_ref[h]).astype(k_ref.dtype)
        k_ref[0, h, :, C_NOPE:] = rope
        vt_ref[0, h, :C_V, :] = _dot_nt(wvt_ref[h], hn).astype(vt_ref.dtype)
        vt_ref[0, h, C_V:, :] = ones_rows


def _kv_up(cin, kv_norm, wk, wvt, cos, sin, *, batch, seq):
    tm = min(TM_UP, seq)
    tps = seq // tm
    return pl.pallas_call(
        _kv_up_kernel,
        grid=(batch * tps,),
        in_specs=[pl.BlockSpec((tm, C_KV_LORA), lambda i: (i, C_KV // C_KV_LORA)),
                  pl.BlockSpec((1, C_KV_LORA), lambda i: (0, 0)),
                  pl.BlockSpec(wk.shape, lambda i: (0, 0, 0)),
                  pl.BlockSpec(wvt.shape, lambda i: (0, 0, 0)),
                  pl.BlockSpec((tm, LANES), lambda i: (i, C_KR // LANES)),
                  pl.BlockSpec((tm, LANES), lambda i: (i, C_KSW // LANES)),
                  pl.BlockSpec((tm, LANES), lambda i: (i % tps, 0)),
                  pl.BlockSpec((tm, LANES), lambda i: (i % tps, 0))],
        out_specs=[pl.BlockSpec((1, C_HEADS, tm, C_QK), lambda i: (i // tps, 0, i % tps, 0)),
                   pl.BlockSpec((1, C_HEADS, C_VT, tm), lambda i: (i // tps, 0, 0, i % tps))],
        out_shape=[jax.ShapeDtypeStruct((batch, C_HEADS, seq, C_QK), MXU_DTYPE),
                   jax.ShapeDtypeStruct((batch, C_HEADS, C_VT, seq), MXU_DTYPE)],
        compiler_params=_cparams(("parallel",), 32),
        name="mla_kv_up",
    )(cin, kv_norm.reshape(1, C_KV_LORA), wk, wvt, cin, cin, cos, sin)


def _flash_kernel(*refs, n_sub, n_ksub):
    qt_refs = refs[:n_sub]
    vt_refs = refs[n_sub:n_sub + n_ksub]
    k_ref, z_ref, o_ref, m_scr, acc_scr = refs[n_sub + n_ksub:]
    j = pl.program_id(2)
    ks, qs = vt_refs[0].shape[3], qt_refs[0].shape[3]

    @pl.when(j == 0)
    def _():
        m_scr[...] = jnp.full_like(m_scr, -jnp.inf)
        acc_scr[...] = jnp.zeros_like(acc_scr)

    units = [(kb, h, a) for kb in range(n_ksub) for h in range(C_HEADS) for a in range(n_sub)]

    def scores(kb, h, a):
        return _dot(k_ref[0, h, kb * ks:(kb + 1) * ks, :], qt_refs[a][0, h]).reshape(ks // 8, 8, qs)

    st = scores(*units[0])
    for n, (kb, h, a) in enumerate(units):
        q0 = a * qs
        st_next = scores(*units[n + 1]) if n + 1 < len(units) else None
        m_prev = m_scr[h, :, q0:q0 + qs]
        m_new = jnp.maximum(m_prev, jnp.max(jnp.max(st, axis=0), axis=0, keepdims=True))
        p = jnp.exp2(st - m_new[None]).reshape(ks, qs).astype(MXU_DTYPE)
        alpha = jnp.exp2(m_prev - m_new)
        pv = _dot(vt_refs[kb][0, h], p)
        acc = acc_scr[h, :, q0:q0 + qs].reshape(C_VT // 8, 8, qs) * alpha[None]
        acc_scr[h, :, q0:q0 + qs] = acc.reshape(C_VT, qs) + pv
        m_scr[h, :, q0:q0 + qs] = m_new
        st = st_next

    @pl.when(j == pl.num_programs(2) - 1)
    def _():
        for h in range(C_HEADS):
            acc = acc_scr[h]
            o = (acc[:C_V] / acc[C_V:C_V + 1]).T
            zh = z_ref[:, h * C_V:(h + 1) * C_V]
            o_ref[:, h * C_V:(h + 1) * C_V] = (o * _silu(zh)).astype(o_ref.dtype)


def _flash(qt, k, vt, cin, *, batch, seq):
    bq, bk = min(BQ, seq), min(BK, seq)
    qs, ks = min(QSUB, bq), min(KSUB, bk)
    n_sub, n_ksub = bq // qs, bk // ks
    nq, nk = seq // bq, seq // bk
    d = C_HEADS * C_V
    qt_specs = [pl.BlockSpec((1, C_HEADS, C_QK, qs), functools.partial(lambda b, i, j, a: (b, 0, 0, i * n_sub + a), a=a))
                for a in range(n_sub)]
    vt_specs = [pl.BlockSpec((1, C_HEADS, C_VT, ks), functools.partial(lambda b, i, j, a: (b, 0, 0, j * n_ksub + a), a=a))
                for a in range(n_ksub)]
    return pl.pallas_call(
        functools.partial(_flash_kernel, n_sub=n_sub, n_ksub=n_ksub),
        grid=(batch, nq, nk),
        in_specs=qt_specs + vt_specs + [
                  pl.BlockSpec((1, C_HEADS, bk, C_QK), lambda b, i, j: (b, 0, j, 0)),
                  pl.BlockSpec((bq, d), lambda b, i, j: (b * nq + i, C_Z // d))],
        out_specs=pl.BlockSpec((bq, d), lambda b, i, j: (b * nq + i, 0)),
        out_shape=jax.ShapeDtypeStruct((batch * seq, d), MXU_DTYPE),
        scratch_shapes=[pltpu.VMEM((C_HEADS, 8, bq), F32), pltpu.VMEM((C_HEADS, C_VT, bq), F32)],
        compiler_params=_cparams(("parallel", "parallel", "arbitrary"), 56),
        name="mla_flash",
    )(*([qt] * n_sub), *([vt] * n_ksub), k, cin)


def _split_cols(w, sizes):
    idx = np.cumsum(np.array(sizes))[:-1].tolist()
    return jnp.split(w, idx, axis=-1)


def _rope_swap(w):
    half = w.shape[-1] // 2
    return jnp.concatenate([-w[..., half:], w[..., :half]], axis=-1)


def _pad_cols(w, n):
    return jnp.pad(w, [(0, 0)] * (w.ndim - 1) + [(0, n - w.shape[-1])])


def _prep_ab(w_in, alpha_w2, alpha_b, igate_b, fgate_b):
    sizes = (2 * A_HEADS * A_DK, A_HEADS * A_DV, 2 * A_HEADS, 2 * A_HEADS, A_HEADS * A_DV, A_HEADS * A_DV,
             B_HEADS * B_DK, B_HEADS * B_DK, B_HEADS * B_DV, 2 * B_GATE_RANK, B_HEADS * B_DV)
    a_qk, a_v, a_i, a_f, a_o, a_z, b_q, b_k, b_v, b_low, b_z = _split_cols(w_in, sizes)
    w_small = _pad_cols(jnp.concatenate([a_i, a_f, b_low], axis=-1), LANES).astype(MXU_DTYPE)
    w = jnp.concatenate([a_qk, a_o, a_z, b_q, b_k, b_z, a_v, b_v], axis=-1).astype(MXU_DTYPE)
    assert w.shape[-1] == AB_SLABS * SLAB
    hk = B_HEADS * B_DK
    w2 = jnp.zeros((LANES, 2 * hk), F32)
    w2 = w2.at[SM_LOW:SM_LOW + B_GATE_RANK, :hk].set(alpha_w2[0])
    w2 = w2.at[SM_LOW + B_GATE_RANK:SM_LOW + 2 * B_GATE_RANK, hk:].set(alpha_w2[1])
    ab = alpha_b.reshape(1, 2 * hk)
    gbias = _pad_cols(jnp.concatenate([igate_b.reshape(1, -1), fgate_b.reshape(1, -1)], axis=-1), LANES)
    return w, w_small, w2.astype(MXU_DTYPE), ab, gbias


def _prep_c(w_in, w_q_up, w_kv_up):
    q_lat, kv_lat, k_rope, z = _split_cols(w_in, (C_Q_LORA, C_KV_LORA, C_ROPE, C_HEADS * C_V))
    w_c = jnp.concatenate([z, kv_lat, _pad_cols(k_rope, LANES), _pad_cols(_rope_swap(k_rope), LANES), q_lat],
                          axis=-1).astype(MXU_DTYPE)
    wq = w_q_up.reshape(C_Q_LORA, C_HEADS, C_NOPE + C_ROPE)
    nope, rope = wq[..., :C_NOPE], wq[..., C_NOPE:]
    wq_t = jnp.transpose(jnp.concatenate([nope, rope, _rope_swap(rope)], axis=-1), (1, 2, 0))
    wkv = w_kv_up.reshape(C_KV_LORA, C_HEADS, C_NOPE + C_V)
    wk = jnp.transpose(wkv[..., :C_NOPE], (1, 0, 2))
    wv_t = jnp.transpose(wkv[..., C_NOPE:], (1, 2, 0))
    return w_c, wq_t.astype(MXU_DTYPE), wk.astype(MXU_DTYPE), wv_t.astype(MXU_DTYPE)


def _rope_tables(seq):
    inv = ROPE_BASE ** (-jnp.arange(0, C_ROPE, 2, dtype=F32) / C_ROPE)
    ang = jnp.arange(seq, dtype=F32)[:, None] * inv[None, :]
    cos, sin = jnp.cos(ang), jnp.sin(ang)
    cos2, sin2 = jnp.concatenate([cos, cos], axis=-1), jnp.concatenate([sin, sin], axis=-1)
    return _pad_cols(cos2, LANES), _pad_cols(sin2, LANES), cos2.T, sin2.T


def _trunk(x, mem, p, prep):
    batch, seq, d = x.shape
    x = x.reshape(batch * seq, d)
    mem = mem.reshape(batch * N_MEM, d)
    cos, sin, cos_t, sin_t = _rope_tables(seq)
    for l in range(DEPTH):
        j = l // 2
        if l % 2 == 0:
            w_in, w_small, w2, ab, gbias = prep['ab'][j]
            proj, small = _ab_proj(x, p['norm_mix'][l], w_in, w_small)
            qk, kt = _conv_silu(proj, p['ab_conv_w'][j], p['ab_conv_b'][j], seq=seq)
            h_f, h_b = _mlstm(qk, kt, proj, small, gbias, batch=batch, seq=seq)
            o_f, o_b = _gla(proj, small, w2, ab, batch=batch, seq=seq)
            x = _ab_post(h_f, h_b, o_f, o_b, proj, p['a_ogate_b'][j], p['a_head_norm'][j], p['b_head_norm'][j],
                         prep['ab_w_out'][j], x)
        else:
            w_c, wq_t, wk, wv_t = prep['c'][j]
            cin = _norm_matmul(x, p['norm_mix'][l], w_c, tm=TM_PROJ, tn=C_TN, out_dtype=F32)
            qt = _q_up(cin, p['c_q_norm'][j], wq_t, cos_t, sin_t, batch=batch, seq=seq)
            k, vt = _kv_up(cin, p['c_kv_norm'][j], wk, wv_t, cos, sin, batch=batch, seq=seq)
            og = _flash(qt, k, vt, cin, batch=batch, seq=seq)
            x = _matmul_residual(og, prep['c_w_out'][j], x, tm=TM_OUT)
        kv = _norm_matmul(mem, p['norm_mem'][l], prep['x_w_kv'][l], tm=N_MEM, tn=1024, out_dtype=MXU_DTYPE)
        x = _cross_attn(x, kv, p['norm_cross'][l], prep['x_w_q'][l], prep['x_w_o'][l], p['final_norm'],
                        seq=seq, final=(l == DEPTH - 1))
    return x.reshape(batch, seq, d)


def kernel(x_prompt, x_sample, mem_prompt, mem_sample, norm_mix, norm_cross, norm_mem, ab_w_in, ab_conv_w, ab_conv_b, a_igate_b, a_fgate_b, a_ogate_b, a_head_norm, b_alpha_w2, b_alpha_b, b_head_norm, ab_w_out, c_w_in, c_q_norm, c_kv_norm, c_w_q_up, c_w_kv_up, c_w_out, x_w_q, x_w_kv, x_w_o, final_norm):
    p = dict(norm_mix=norm_mix, norm_cross=norm_cross, norm_mem=norm_mem, ab_conv_w=ab_conv_w,
             ab_conv_b=ab_conv_b, a_ogate_b=a_ogate_b, a_head_norm=a_head_norm, b_head_norm=b_head_norm,
             c_q_norm=c_q_norm, c_kv_norm=c_kv_norm, final_norm=final_norm)
    n_even, n_odd = ab_w_in.shape[0], c_w_in.shape[0]
    prep = dict(
        ab=[_prep_ab(ab_w_in[j], b_alpha_w2[j], b_alpha_b[j], a_igate_b[j], a_fgate_b[j]) for j in range(n_even)],
        ab_w_out=[ab_w_out[j].astype(MXU_DTYPE) for j in range(n_even)],
        c=[_prep_c(c_w_in[j], c_w_q_up[j], c_w_kv_up[j]) for j in range(n_odd)],
        c_w_out=[c_w_out[j].astype(MXU_DTYPE) for j in range(n_odd)],
        x_w_q=[x_w_q[l].astype(MXU_DTYPE) for l in range(DEPTH)],
        x_w_kv=[x_w_kv[l].astype(MXU_DTYPE) for l in range(DEPTH)],
        x_w_o=[x_w_o[l].astype(MXU_DTYPE) for l in range(DEPTH)],
    )
    return (_trunk(x_prompt, mem_prompt, p, prep), _trunk(x_sample, mem_sample, p, prep))
```

```python
import functools

import numpy as np
import jax
import jax.numpy as jnp
from jax import lax
from jax.experimental import pallas as pl
from jax.experimental.pallas import tpu as pltpu

F32 = jnp.float32
MXU_DTYPE = jnp.bfloat16
EPS = 1e-6
HIGHEST = lax.Precision.HIGHEST

D_MODEL = 1024
DEPTH = 4
A_HEADS, A_DK, A_DV, A_CHUNK = 4, 128, 256, 128
B_HEADS, B_DK, B_DV, B_CHUNK = 4, 128, 256, 64
B_GATE_RANK, B_GATE_TAU = 16, 16.0
B_SUB = 16
B_CHUNKS_PER_STEP = 4
C_HEADS, C_Q_LORA, C_KV_LORA, C_NOPE, C_ROPE, C_V = 8, 384, 256, 128, 64, 128
ROPE_BASE = 10000.0
X_HEADS, X_DH, N_MEM = 4, 256, 256

LANES = 128
V7X_VMEM_BYTES = 64 * 1024 * 1024

SLAB = 512
AB_QK, AB_O, AB_Z, AB_BQ, AB_BK, AB_BZ, AB_AV, AB_BV = 0, 2, 4, 6, 7, 8, 10, 12
AB_SLABS = 14
AB_SLABS_PER_STEP = 7
SM_I, SM_F, SM_LOW = 0, 8, 16
C_Z, C_KV, C_KR, C_KSW, C_QL = 0, 1024, 1280, 1408, 1536
C_COLS = 1920
C_TN = 640

TM_PROJ = 1024
TM_CONV = 512
TM_POST = 512
TM_CROSS = 1024
TM_CROSS_MIX = 512
TM_UP = 512
TM_OUT = 1024
BQ, BK, QSUB, KSUB = 1024, 1024, 512, 512


def _cparams(sem, vmem_mb=48):
    assert vmem_mb * 1024 * 1024 <= V7X_VMEM_BYTES
    return pltpu.CompilerParams(dimension_semantics=sem, vmem_limit_bytes=vmem_mb * 1024 * 1024)


def _sigmoid(x):
    return 1.0 / (1.0 + jnp.exp(-x))


def _silu(x):
    return x * _sigmoid(x)


def _log_sigmoid(x):
    return jnp.minimum(x, 0.0) - jnp.log1p(jnp.exp(-jnp.abs(x)))


def _dot(a, b):
    return jnp.dot(a, b, preferred_element_type=F32)


def _dot_nt(a, b):
    return lax.dot_general(a, b, (((1,), (1,)), ((), ())), preferred_element_type=F32)


def _dot_tn(a, b):
    return lax.dot_general(a, b, (((0,), (0,)), ((), ())), preferred_element_type=F32)


def _rms(x, g):
    ms = jnp.mean(x * x, axis=-1, keepdims=True)
    return x * lax.rsqrt(ms + EPS) * g


def _head_rms(x, g, n_heads):
    w = x.shape[-1] // n_heads
    parts = []
    for h in range(n_heads):
        xs = x[:, h * w:(h + 1) * w]
        ms = jnp.mean(xs * xs, axis=-1, keepdims=True)
        parts.append(xs * lax.rsqrt(ms + EPS))
    return jnp.concatenate(parts, axis=-1) * g


def _norm_matmul_kernel(x_ref, g_ref, w_ref, o_ref, h_ref):
    @pl.when(pl.program_id(1) == 0)
    def _():
        h_ref[...] = _rms(x_ref[...], g_ref[...]).astype(h_ref.dtype)

    o_ref[...] = _dot(h_ref[...], w_ref[...]).astype(o_ref.dtype)


def _norm_matmul(x, g, w, *, tm, tn, out_dtype, x_col_block=0):
    m = x.shape[0]
    k, n = w.shape
    tm = min(tm, m)
    return pl.pallas_call(
        _norm_matmul_kernel,
        grid=(m // tm, n // tn),
        in_specs=[pl.BlockSpec((tm, k), lambda i, j: (i, x_col_block)),
                  pl.BlockSpec((1, k), lambda i, j: (0, 0)),
                  pl.BlockSpec((k, tn), lambda i, j: (0, j))],
        out_specs=pl.BlockSpec((tm, tn), lambda i, j: (i, j)),
        out_shape=jax.ShapeDtypeStruct((m, n), out_dtype),
        scratch_shapes=[pltpu.VMEM((tm, k), MXU_DTYPE)],
        compiler_params=_cparams(("parallel", "arbitrary"), 56),
        name="norm_matmul",
    )(x, g.reshape(1, k), w)


def _ab_proj_kernel(x_ref, g_ref, w_ref, ws_ref, o_ref, os_ref, h_ref):
    @pl.when(pl.program_id(1) == 0)
    def _():
        h_ref[...] = _rms(x_ref[...], g_ref[...]).astype(h_ref.dtype)
        os_ref[...] = _dot(h_ref[...], ws_ref[...])

    for t in range(o_ref.shape[0]):
        o_ref[t] = _dot(h_ref[...], w_ref[:, t * SLAB:(t + 1) * SLAB]).astype(o_ref.dtype)


def _ab_proj(x, g, w, w_small):
    m, k = x.shape
    tm = min(TM_PROJ, m)
    ns = AB_SLABS_PER_STEP
    return pl.pallas_call(
        _ab_proj_kernel,
        grid=(m // tm, AB_SLABS // ns),
        in_specs=[pl.BlockSpec((tm, k), lambda i, j: (i, 0)),
                  pl.BlockSpec((1, k), lambda i, j: (0, 0)),
                  pl.BlockSpec((k, ns * SLAB), lambda i, j: (0, j)),
                  pl.BlockSpec((k, LANES), lambda i, j: (0, 0))],
        out_specs=[pl.BlockSpec((ns, tm, SLAB), lambda i, j: (j, i, 0)),
                   pl.BlockSpec((tm, LANES), lambda i, j: (i, 0))],
        out_shape=[jax.ShapeDtypeStruct((AB_SLABS, m, SLAB), MXU_DTYPE),
                   jax.ShapeDtypeStruct((m, LANES), F32)],
        scratch_shapes=[pltpu.VMEM((tm, k), MXU_DTYPE)],
        compiler_params=_cparams(("parallel", "arbitrary"), 56),
        name="ab_proj",
    )(x, g.reshape(1, k), w, w_small)


def _matmul_res_kernel(a_ref, w_ref, x_ref, o_ref):
    o_ref[...] = x_ref[...] + _dot(a_ref[...], w_ref[...])


def _matmul_residual(a, w, x, *, tm):
    m, k = a.shape
    n = w.shape[1]
    tm = min(tm, m)
    return pl.pallas_call(
        _matmul_res_kernel,
        grid=(m // tm,),
        in_specs=[pl.BlockSpec((tm, k), lambda i: (i, 0)),
                  pl.BlockSpec((k, n), lambda i: (0, 0)),
                  pl.BlockSpec((tm, n), lambda i: (i, 0))],
        out_specs=pl.BlockSpec((tm, n), lambda i: (i, 0)),
        out_shape=jax.ShapeDtypeStruct((m, n), F32),
        compiler_params=_cparams(("parallel",), 40),
        name="matmul_residual",
    )(a, w, x)


def _conv_kernel(x_ref, xp_ref, xn_ref, w_ref, b_ref, o_ref, kt_ref, *, tiles_per_seq):
    i = pl.program_id(0)
    ts = x_ref.shape[1]
    t_in_seq = i % tiles_per_seq
    rows = lax.broadcasted_iota(jnp.int32, (ts, SLAB), 0)
    for t in range(2):
        cs = slice(t * SLAB, (t + 1) * SLAB)
        x = x_ref[t].astype(F32)
        last = xp_ref.shape[1] - 1
        prev_row = jnp.where(t_in_seq == 0, 0.0, xp_ref[t, last:last + 1, :].astype(F32))
        next_row = jnp.where(t_in_seq == tiles_per_seq - 1, 0.0, xn_ref[t, 0:1, :].astype(F32))
        x_prev = jnp.where(rows == 0, prev_row, pltpu.roll(x, 1, axis=0))
        x_next = jnp.where(rows == ts - 1, next_row, pltpu.roll(x, ts - 1, axis=0))
        y = _silu(w_ref[0:1, cs] * x_prev + w_ref[1:2, cs] * x + w_ref[2:3, cs] * x_next + b_ref[:, cs])
        if t == 0:
            o_ref[:, cs] = (y * (A_DK ** -0.5)).astype(o_ref.dtype)
        else:
            o_ref[:, cs] = y.astype(o_ref.dtype)
            kt_ref[...] = y.T.astype(kt_ref.dtype)


def _conv_silu(proj, conv_w, conv_b, *, seq):
    m = proj.shape[1]
    c = conv_w.shape[1]
    assert c == 2 * SLAB and AB_QK == 0
    ts = min(TM_CONV, seq)
    nt = m // ts
    halo = 16
    sub = ts // halo
    return pl.pallas_call(
        functools.partial(_conv_kernel, tiles_per_seq=seq // ts),
        grid=(nt,),
        in_specs=[pl.BlockSpec((2, ts, SLAB), lambda i: (0, i, 0)),
                  pl.BlockSpec((2, halo, SLAB), lambda i: (0, jnp.maximum(i * sub - 1, 0), 0)),
                  pl.BlockSpec((2, halo, SLAB), lambda i: (0, jnp.minimum((i + 1) * sub, nt * sub - 1), 0)),
                  pl.BlockSpec((3, c), lambda i: (0, 0)),
                  pl.BlockSpec((1, c), lambda i: (0, 0))],
        out_specs=[pl.BlockSpec((ts, c), lambda i: (i, 0)),
                   pl.BlockSpec((c // 2, ts), lambda i: (0, i))],
        out_shape=[jax.ShapeDtypeStruct((m, c), MXU_DTYPE),
                   jax.ShapeDtypeStruct((c // 2, m), MXU_DTYPE)],
        compiler_params=_cparams(("parallel",), 32),
        name="conv_silu",
    )(proj, proj, proj, conv_w, conv_b.reshape(1, c))


def _mlstm_kernel(qk_f, kt_f, v_f, g_f, gn_f, qk_b, kt_b, v_b, g_b, gn_b, gbias_ref, h_f, h_b,
                  c_scr, n_scr, m_scr, gate_scr):
    L = A_CHUNK

    row = lax.broadcasted_iota(jnp.int32, (L, L), 0)
    col = lax.broadcasted_iota(jnp.int32, (L, L), 1)
    sub8 = lax.broadcasted_iota(jnp.int32, (8, L), 0)
    lane_row = lax.broadcasted_iota(jnp.int32, (1, LANES), 1)
    neg_inf = float("-inf")
    masks = (row >= col, row <= col)

    def gate_sums(gate_ref, d):
        gates = gate_ref[...] + gbias_ref[...]
        csum = jnp.dot(masks[d].astype(F32), _log_sigmoid(gates), precision=HIGHEST, preferred_element_type=F32)
        a_c = pltpu.roll(csum, LANES - (SM_F - SM_I), axis=1)
        e_c = gates - a_c
        run = e_c
        sh = 1
        while sh < L:
            pad = jnp.full((sh, LANES), neg_inf, F32)
            moved = (jnp.concatenate([pad, run[:L - sh]], axis=0) if d == 0
                     else jnp.concatenate([run[sh:], pad], axis=0))
            run = jnp.maximum(run, moved)
            sh *= 2
        return a_c, run, e_c.T

    def put_gate_sums(d, vals):
        for n, val in enumerate(vals):
            gate_scr[d, n] = val

    @pl.when(pl.program_id(1) == 0)
    def _():
        c_scr[...] = jnp.zeros_like(c_scr)
        n_scr[...] = jnp.zeros_like(n_scr)
        m_scr[...] = jnp.zeros_like(m_scr)
        put_gate_sums(0, gate_sums(g_f, 0))
        put_gate_sums(1, gate_sums(g_b, 1))

    m_all = m_scr[0:1, :]
    m_next = m_all
    dirs = ((qk_f, kt_f, v_f, g_f, h_f), (qk_b, kt_b, v_b, g_b, h_b))
    gate_next = [gate_sums(gn_f, 0), gate_sums(gn_b, 1)]
    pro = []
    for d in range(2):
        a_c, run, e_t = gate_scr[d, 0], gate_scr[d, 1], gate_scr[d, 2]
        m_c = jnp.maximum(run, m_all)
        w_inter_c = jnp.exp(m_all - m_c)
        inv_floor_c = jnp.exp(-(a_c + m_c))
        pro.append((masks[d], a_c, m_c, w_inter_c, inv_floor_c, e_t, m_c.T))

    for d, (qk_ref, kt_ref, v_ref, g_ref, h_ref) in enumerate(dirs):
        mask, a_c, m_c, w_inter_c, inv_floor_c, e_t, m_t_rows = pro[d]
        end = L - 1 if d == 0 else 0
        for h in range(A_HEADS):
            s = d * A_HEADS + h
            n_prev = n_scr[s]
            c_prev = c_scr[s]
            q = qk_ref[:, h * A_DK:(h + 1) * A_DK]
            k = qk_ref[:, (A_HEADS + h) * A_DK:(A_HEADS + h + 1) * A_DK]
            kt = kt_ref[h * A_DK:(h + 1) * A_DK, :]
            v0 = h * A_DV
            v = v_ref[v0 // SLAB, :, v0 % SLAB:v0 % SLAB + A_DV]
            e_r, m_r = e_t[s:s + 1, :], m_t_rows[s:s + 1, :]

            lhs = jnp.where(sub8 == 0, -m_r, jnp.where(sub8 == 1, 1.0, 0.0))
            rhs = jnp.where(sub8 == 0, 1.0, jnp.where(sub8 == 1, e_r, 0.0))
            x = lax.dot_general(lhs, rhs, (((0,), (0,)), ((), ())), precision=HIGHEST,
                                preferred_element_type=F32)
            sc = _dot_nt(q, k) * jnp.exp(jnp.where(mask, x, neg_inf))
            w_inter = w_inter_c[:, s:s + 1]
            num = _dot(sc.astype(MXU_DTYPE), v) + w_inter * _dot(q, c_prev.astype(MXU_DTYPE))
            qn = _dot_nt(q, n_prev.astype(MXU_DTYPE))[:, 0:1]
            den = jnp.sum(sc, axis=1, keepdims=True) + w_inter * qn
            hv = num / jnp.maximum(jnp.abs(den), inv_floor_c[:, s:s + 1])
            h_ref[:, h * A_DV:(h + 1) * A_DV] = hv.astype(h_ref.dtype)

            g = a_c[end:end + 1, s:s + 1]
            m_end = m_c[end:end + 1, s:s + 1]
            w = jnp.exp(e_r - m_end)
            decay = w_inter_c[end:end + 1, s:s + 1]
            kwt = (kt.astype(F32) * w).astype(MXU_DTYPE)
            c_scr[s] = decay * c_prev + _dot(kwt, v)
            n_scr[s] = decay * n_prev + _dot(jnp.broadcast_to(w, (8, L)).astype(MXU_DTYPE), k)
            m_next = jnp.where(lane_row == s, g + m_end, m_next)
    m_scr[...] = jnp.broadcast_to(m_next, m_scr.shape)
    put_gate_sums(0, gate_next[0])
    put_gate_sums(1, gate_next[1])


def _mlstm(qk, kt, proj, small, gbias, *, batch, seq):
    m = qk.shape[0]
    L = A_CHUNK
    nc = seq // L
    d_a = A_HEADS * A_DV
    v_slabs = d_a // SLAB
    v_blk = AB_AV // v_slabs
    assert L == LANES
    fwd = lambda b, c: b * nc + c
    bwd = lambda b, c: b * nc + (nc - 1 - c)
    nxt = lambda c: jnp.minimum(c + 1, nc - 1)
    out = jax.ShapeDtypeStruct((m, d_a), MXU_DTYPE)
    return pl.pallas_call(
        _mlstm_kernel,
        grid=(batch, nc),
        in_specs=[pl.BlockSpec((L, 2 * A_HEADS * A_DK), lambda b, c: (fwd(b, c), 0)),
                  pl.BlockSpec((A_HEADS * A_DK, L), lambda b, c: (0, fwd(b, c))),
                  pl.BlockSpec((v_slabs, L, SLAB), lambda b, c: (v_blk, fwd(b, c), 0)),
                  pl.BlockSpec((L, LANES), lambda b, c: (fwd(b, c), 0)),
                  pl.BlockSpec((L, LANES), lambda b, c: (fwd(b, nxt(c)), 0)),
                  pl.BlockSpec((L, 2 * A_HEADS * A_DK), lambda b, c: (bwd(b, c), 0)),
                  pl.BlockSpec((A_HEADS * A_DK, L), lambda b, c: (0, bwd(b, c))),
                  pl.BlockSpec((v_slabs, L, SLAB), lambda b, c: (v_blk, bwd(b, c), 0)),
                  pl.BlockSpec((L, LANES), lambda b, c: (bwd(b, c), 0)),
                  pl.BlockSpec((L, LANES), lambda b, c: (bwd(b, nxt(c)), 0)),
                  pl.BlockSpec((1, LANES), lambda b, c: (0, 0))],
        out_specs=[pl.BlockSpec((L, d_a), lambda b, c: (fwd(b, c), 0)),
                   pl.BlockSpec((L, d_a), lambda b, c: (bwd(b, c), 0))],
        out_shape=[out, out],
        scratch_shapes=[pltpu.VMEM((2 * A_HEADS, A_DK, A_DV), F32),
                        pltpu.VMEM((2 * A_HEADS, 8, A_DK), F32),
                        pltpu.VMEM((8, LANES), F32),
                        pltpu.VMEM((2, 3, L, LANES), F32)],
        compiler_params=_cparams(("parallel", "arbitrary"), 32),
        name="mlstm",
    )(qk, kt, proj, small, small, qk, kt, proj, small, small, gbias)


def _gla_kernel(qk_f, v_f, g_f, gn_f, qk_b, v_b, g_b, gn_b, w2_ref, ab_ref, o_f, o_b, st_scr, bc_scr):
    L, SB = B_CHUNK, B_SUB
    nsb = L // SB
    hk = B_HEADS * B_DK
    rows_per_step = g_f.shape[0]
    n_chunks = rows_per_step // L

    row = lax.broadcasted_iota(jnp.int32, (L, L), 0)
    col = lax.broadcasted_iota(jnp.int32, (L, L), 1)
    lane8 = lax.broadcasted_iota(jnp.int32, (8, L), 1)
    log2e = float(np.log2(np.e))
    dirs = ((qk_f, v_f, o_f), (qk_b, v_b, o_b))
    masks = (row >= col, row <= col)

    def decay_sums(gate_ref, d):
        pre = _dot(gate_ref[...].astype(MXU_DTYPE), w2_ref[:, d * hk:(d + 1) * hk]) + ab_ref[:, d * hk:(d + 1) * hk]
        log_a = _log_sigmoid(pre) / B_GATE_TAU
        tri = masks[d].astype(F32)
        return jnp.concatenate(
            [jnp.dot(tri, log_a[u * L:(u + 1) * L], precision=HIGHEST, preferred_element_type=F32)
             for u in range(n_chunks)], axis=0)

    @pl.when(pl.program_id(1) == 0)
    def _():
        st_scr[...] = jnp.zeros_like(st_scr)
        bc_scr[0] = decay_sums(g_f, 0)
        bc_scr[1] = decay_sums(g_b, 1)

    bc_all = [bc_scr[0], bc_scr[1]]
    bc_next = [decay_sums(gn_f, 0), decay_sums(gn_b, 1)]

    def chunk(r0, states):
        work = []
        for d, (qk_ref, v_ref, o_ref) in enumerate(dirs):
            rs = slice(r0[d], r0[d] + L)
            for h in range(B_HEADS):
                s = d * B_HEADS + h
                q = qk_ref[0, rs, h * B_DK:(h + 1) * B_DK].astype(F32) * (B_DK ** -0.5)
                k = qk_ref[1, rs, h * B_DK:(h + 1) * B_DK].astype(F32)
                b = bc_all[d][rs, h * B_DK:(h + 1) * B_DK]
                st = states[s]
                inter = _dot_nt((q * jnp.exp(b)).astype(MXU_DTYPE), st.astype(MXU_DTYPE))

                b2 = b * log2e
                b2_rows = [jnp.broadcast_to(b2[j:j + 1], (8, B_DK)) for j in range(L)]
                pieces, where_to = [], []
                for blk in range(nsb):
                    lo = blk * SB
                    for oc in range(SB // 8):
                        t0 = lo + 8 * oc
                        js = range(0, 8 * oc + 8) if d == 0 else range(8 * oc, SB)
                        for j in js:
                            pieces.append(q[t0:t0 + 8] * jnp.exp2(b2[t0:t0 + 8] - b2_rows[lo + j]))
                            where_to.append((t0, lo + j))
                diag = _dot_nt(jnp.concatenate(pieces, axis=0).astype(MXU_DTYPE), k.astype(MXU_DTYPE))

                offs = []
                for blk in range(nsb):
                    lo, hi = blk * SB, (blk + 1) * SB
                    if d == 0 and blk > 0:
                        ref = b[lo - 1:lo]
                        kt = k[:lo] * jnp.exp(ref - b[:lo])
                        kt = jnp.concatenate([kt, jnp.zeros((L - lo, B_DK), F32)], axis=0)
                    elif d == 1 and blk < nsb - 1:
                        ref = b[hi:hi + 1]
                        kt = k[hi:] * jnp.exp(ref - b[hi:])
                        kt = jnp.concatenate([jnp.zeros((hi, B_DK), F32), kt], axis=0)
                    else:
                        offs.append(jnp.zeros((SB, L), F32))
                        continue
                    offs.append(_dot_nt((q[lo:hi] * jnp.exp(b[lo:hi] - ref)).astype(MXU_DTYPE),
                                        kt.astype(MXU_DTYPE)))
                work.append((d, h, s, rs, k, b, st, inter, diag, where_to, offs))

        for d, h, s, rs, k, b, st, inter, diag, where_to, offs in work:
            v = dirs[d][1][h * B_DV // SLAB, rs, h * B_DV % SLAB:h * B_DV % SLAB + B_DV]
            att_rows = []
            for blk in range(nsb):
                for oc in range(SB // 8):
                    t0 = blk * SB + 8 * oc
                    a8 = offs[blk][8 * oc:8 * oc + 8]
                    for n, (rr, cc) in enumerate(where_to):
                        if rr == t0:
                            a8 = jnp.where(lane8 == cc, diag[8 * n:8 * n + 8], a8)
                    att_rows.append(a8)
            att = jnp.where(masks[d], jnp.concatenate(att_rows, axis=0), 0.0)
            o = inter + _dot(att.astype(MXU_DTYPE), v)
            dirs[d][2][rs, h * B_DV:(h + 1) * B_DV] = o.astype(dirs[d][2].dtype)

        new_states = list(states)
        for d, h, s, rs, k, b, st, inter, diag, where_to, offs in work:
            v = dirs[d][1][h * B_DV // SLAB, rs, h * B_DV % SLAB:h * B_DV % SLAB + B_DV]
            g = b[L - 1:L, :] if d == 0 else b[0:1, :]
            kd = (k * jnp.exp(g - b)).astype(MXU_DTYPE)
            new_states[s] = jnp.exp(g) * st + _dot_tn(v, kd)
        return new_states

    states = [st_scr[s] for s in range(2 * B_HEADS)]
    for u in range(n_chunks):
        states = chunk((u * L, (n_chunks - 1 - u) * L), states)
    for s in range(2 * B_HEADS):
        st_scr[s] = states[s]
    bc_scr[0] = bc_next[0]
    bc_scr[1] = bc_next[1]


def _gla(proj, small, w2, ab, *, batch, seq):
    m = proj.shape[1]
    rows = min(B_CHUNKS_PER_STEP * B_CHUNK, seq)
    ns = seq // rows
    hk = B_HEADS * B_DK
    d_b = B_HEADS * B_DV
    assert hk == SLAB and AB_BK == AB_BQ + 1 and AB_BQ % 2 == 0
    v_slabs = d_b // SLAB
    fwd = lambda b, c: b * ns + c
    bwd = lambda b, c: b * ns + (ns - 1 - c)

    def specs(idx):
        nxt = lambda b, c: idx(b, jnp.minimum(c + 1, ns - 1))
        return [pl.BlockSpec((2, rows, SLAB), lambda b, c: (AB_BQ // 2, idx(b, c), 0)),
                pl.BlockSpec((v_slabs, rows, SLAB), lambda b, c: (AB_BV // v_slabs, idx(b, c), 0)),
                pl.BlockSpec((rows, LANES), lambda b, c: (idx(b, c), 0)),
                pl.BlockSpec((rows, LANES), lambda b, c: (nxt(b, c), 0))]

    out = jax.ShapeDtypeStruct((m, d_b), MXU_DTYPE)
    return pl.pallas_call(
        _gla_kernel,
        grid=(batch, ns),
        in_specs=specs(fwd) + specs(bwd) + [pl.BlockSpec((LANES, 2 * hk), lambda b, c: (0, 0)),
                                            pl.BlockSpec((1, 2 * hk), lambda b, c: (0, 0))],
        out_specs=[pl.BlockSpec((rows, d_b), lambda b, c: (fwd(b, c), 0)),
                   pl.BlockSpec((rows, d_b), lambda b, c: (bwd(b, c), 0))],
        out_shape=[out, out],
        scratch_shapes=[pltpu.VMEM((2 * B_HEADS, B_DV, B_DK), F32),
                        pltpu.VMEM((2, rows, hk), F32)],
        compiler_params=_cparams(("parallel", "arbitrary"), 32),
        name="gla",
    )(proj, proj, small, small, proj, proj, small, small, w2, ab)


def _ab_post_kernel(hf_ref, hb_ref, ao_ref, az_ref, of_ref, ob_ref, bz_ref, ogb_ref, an_ref, bn_ref,
                    w_ref, x_ref, o_ref):
    d_a = A_HEADS * A_DV

    def slabs(ref):
        return jnp.concatenate([ref[t] for t in range(ref.shape[0])], axis=-1).astype(F32)

    h_sum = hf_ref[...].astype(F32) + hb_ref[...].astype(F32)
    out_a = _sigmoid(slabs(ao_ref) + ogb_ref[...]) * h_sum
    out_a = _head_rms(out_a, an_ref[...], A_HEADS) * _silu(slabs(az_ref))
    o_sum = of_ref[...].astype(F32) + ob_ref[...].astype(F32)
    out_b = _head_rms(o_sum, bn_ref[...], B_HEADS) * _silu(slabs(bz_ref))
    y = _dot(out_a.astype(MXU_DTYPE), w_ref[:d_a, :]) + _dot(out_b.astype(MXU_DTYPE), w_ref[d_a:, :])
    o_ref[...] = x_ref[...] + y


def _ab_post(h_f, h_b, o_f, o_b, proj, ogate_b, a_norm, b_norm, w_out, x):
    m, d = x.shape
    tm = min(TM_POST, m)
    row = lambda i: (i, 0)
    vec = pl.BlockSpec((1, d), lambda i: (0, 0))
    ns = d // SLAB
    blk = lambda slab: pl.BlockSpec((ns, tm, SLAB), lambda i: (slab // ns, i, 0))
    return pl.pallas_call(
        _ab_post_kernel,
        grid=(m // tm,),
        in_specs=[pl.BlockSpec((tm, d), row), pl.BlockSpec((tm, d), row), blk(AB_O), blk(AB_Z),
                  pl.BlockSpec((tm, d), row), pl.BlockSpec((tm, d), row), blk(AB_BZ),
                  vec, vec, vec, pl.BlockSpec(w_out.shape, lambda i: (0, 0)), pl.BlockSpec((tm, d), row)],
        out_specs=pl.BlockSpec((tm, d), row),
        out_shape=jax.ShapeDtypeStruct((m, d), F32),
        compiler_params=_cparams(("parallel",), 48),
        name="ab_post",
    )(h_f, h_b, proj, proj, o_f, o_b, proj, ogate_b.reshape(1, d), a_norm.reshape(1, d), b_norm.reshape(1, d),
      w_out, x)


def _cross_kernel(*refs, final, mix):
    if mix:
        a_ref, wa_ref = refs[:2]
        refs = refs[2:]
    x_ref, g_ref, wq_ref, kv_ref, wo_ref, fin_ref, o_ref = refs
    x = x_ref[...]
    if mix:
        x = x + _dot(a_ref[...], wa_ref[...])
    hn = _rms(x, g_ref[...]).astype(MXU_DTYPE)
    q = _dot(hn, wq_ref[...]).astype(MXU_DTYPE)
    d = X_HEADS * X_DH
    outs = []
    for h in range(X_HEADS):
        k = kv_ref[:, h * X_DH:(h + 1) * X_DH]
        v = kv_ref[:, d + h * X_DH:d + (h + 1) * X_DH]
        sc = _dot_nt(q[:, h * X_DH:(h + 1) * X_DH], k) * (X_DH ** -0.5)
        e = jnp.exp(sc - jnp.max(sc, axis=-1, keepdims=True))
        p = e / jnp.sum(e, axis=-1, keepdims=True)
        outs.append(_dot(p.astype(MXU_DTYPE), v))
    o = jnp.concatenate(outs, axis=-1).astype(MXU_DTYPE)
    y = x + _dot(o, wo_ref[...])
    if final:
        y = _rms(y, fin_ref[...])
    o_ref[...] = y


def _cross_attn(x, kv, g, w_q, w_o, fin, *, seq, final, mix=None):
    m, d = x.shape
    tm = min(TM_CROSS if mix is None else TM_CROSS_MIX, seq)
    tiles_per_seq = seq // tm
    full = lambda i: (0, 0)
    mix_specs, mix_args = [], []
    if mix is not None:
        mix_specs = [pl.BlockSpec((tm, mix[0].shape[1]), lambda i: (i, 0)), pl.BlockSpec(mix[1].shape, full)]
        mix_args = list(mix)
    return pl.pallas_call(
        functools.partial(_cross_kernel, final=final, mix=mix is not None),
        grid=(m // tm,),
        in_specs=mix_specs + [pl.BlockSpec((tm, d), lambda i: (i, 0)),
                  pl.BlockSpec((1, d), full),
                  pl.BlockSpec(w_q.shape, full),
                  pl.BlockSpec((N_MEM, kv.shape[1]), lambda i: (i // tiles_per_seq, 0)),
                  pl.BlockSpec(w_o.shape, full),
                  pl.BlockSpec((1, d), full)],
        out_specs=pl.BlockSpec((tm, d), lambda i: (i, 0)),
        out_shape=jax.ShapeDtypeStruct((m, d), F32),
        compiler_params=_cparams(("parallel",), 56),
        name="cross_attn",
    )(*mix_args, x, g.reshape(1, d), w_q, kv, w_o, fin.reshape(1, d))


C_QK = C_NOPE + LANES


C_EXP2 = (C_NOPE + C_ROPE) ** -0.5 * float(np.log2(np.e))
C_VT = C_V + 8


def _q_up_kernel(x_ref, g_ref, w_ref, cos_ref, sin_ref, o_ref):
    hn = _rms(x_ref[...], g_ref[...]).astype(MXU_DTYPE)
    pad = jnp.zeros((C_QK - C_NOPE - C_ROPE, hn.shape[0]), o_ref.dtype)
    for h in range(C_HEADS):
        rt = _dot_nt(w_ref[h], hn) * C_EXP2
        o_ref[0, h, :C_NOPE, :] = rt[:C_NOPE].astype(o_ref.dtype)
        rope = rt[C_NOPE:C_NOPE + C_ROPE] * cos_ref[...] + rt[C_NOPE + C_ROPE:] * sin_ref[...]
        o_ref[0, h, C_NOPE:C_NOPE + C_ROPE, :] = rope.astype(o_ref.dtype)
        o_ref[0, h, C_NOPE + C_ROPE:, :] = pad


def _q_up(cin, q_norm, w, cos_t, sin_t, *, batch, seq):
    tm = min(TM_UP, seq)
    tps = seq // tm
    return pl.pallas_call(
        _q_up_kernel,
        grid=(batch * tps,),
        in_specs=[pl.BlockSpec((tm, C_Q_LORA), lambda i: (i, C_QL // C_Q_LORA)),
                  pl.BlockSpec((1, C_Q_LORA), lambda i: (0, 0)),
                  pl.BlockSpec(w.shape, lambda i: (0, 0, 0)),
                  pl.BlockSpec((C_ROPE, tm), lambda i: (0, i % tps)),
                  pl.BlockSpec((C_ROPE, tm), lambda i: (0, i % tps))],
        out_specs=pl.BlockSpec((1, C_HEADS, C_QK, tm), lambda i: (i // tps, 0, 0, i % tps)),
        out_shape=jax.ShapeDtypeStruct((batch, C_HEADS, C_QK, seq), MXU_DTYPE),
        compiler_params=_cparams(("parallel",), 32),
        name="mla_q_up",
    )(cin, q_norm.reshape(1, C_Q_LORA), w, cos_t, sin_t)


def _kv_up_kernel(x_ref, g_ref, wk_ref, wvt_ref, kr_ref, ksw_ref, cos_ref, sin_ref, k_ref, vt_ref):
    hn = _rms(x_ref[...], g_ref[...]).astype(MXU_DTYPE)
    rope = (kr_ref[...] * cos_ref[...] + ksw_ref[...] * sin_ref[...]).astype(k_ref.dtype)
    tm = hn.shape[0]
    ones_rows = (lax.broadcasted_iota(jnp.int32, (C_VT - C_V, tm), 0) == 0).astype(vt_ref.dtype)
    for h in range(C_HEADS):
        k_ref[0, h, :, :C_NOPE] = _dot(hn, wk_ref[h]).astype(k_ref.dtype)
        k_ref[0, h, :, C_NOPE:] = rope
        vt_ref[0, h, :C_V, :] = _dot_nt(wvt_ref[h], hn).astype(vt_ref.dtype)
        vt_ref[0, h, C_V:, :] = ones_rows


def _kv_up(cin, kv_norm, wk, wvt, cos, sin, *, batch, seq):
    tm = min(TM_UP, seq)
    tps = seq // tm
    return pl.pallas_call(
        _kv_up_kernel,
        grid=(batch * tps,),
        in_specs=[pl.BlockSpec((tm, C_KV_LORA), lambda i: (i, C_KV // C_KV_LORA)),
                  pl.BlockSpec((1, C_KV_LORA), lambda i: (0, 0)),
                  pl.BlockSpec(wk.shape, lambda i: (0, 0, 0)),
                  pl.BlockSpec(wvt.shape, lambda i: (0, 0, 0)),
                  pl.BlockSpec((tm, LANES), lambda i: (i, C_KR // LANES)),
                  pl.BlockSpec((tm, LANES), lambda i: (i, C_KSW // LANES)),
                  pl.BlockSpec((tm, LANES), lambda i: (i % tps, 0)),
                  pl.BlockSpec((tm, LANES), lambda i: (i % tps, 0))],
        out_specs=[pl.BlockSpec((1, C_HEADS, tm, C_QK), lambda i: (i // tps, 0, i % tps, 0)),
                   pl.BlockSpec((1, C_HEADS, C_VT, tm), lambda i: (i // tps, 0, 0, i % tps))],
        out_shape=[jax.ShapeDtypeStruct((batch, C_HEADS, seq, C_QK), MXU_DTYPE),
                   jax.ShapeDtypeStruct((batch, C_HEADS, C_VT, seq), MXU_DTYPE)],
        compiler_params=_cparams(("parallel",), 32),
        name="mla_kv_up",
    )(cin, kv_norm.reshape(1, C_KV_LORA), wk, wvt, cin, cin, cos, sin)


def _flash_kernel(*refs, n_sub, n_ksub):
    qt_refs = refs[:n_sub]
    vt_refs = refs[n_sub:n_sub + n_ksub]
    k_ref, z_ref, o_ref, m_scr, acc_scr = refs[n_sub + n_ksub:]
    j = pl.program_id(2)
    ks, qs = vt_refs[0].shape[3], qt_refs[0].shape[3]

    @pl.when(j == 0)
    def _():
        m_scr[...] = jnp.full_like(m_scr, -jnp.inf)
        acc_scr[...] = jnp.zeros_like(acc_scr)

    units = [(kb, h, a) for kb in range(n_ksub) for h in range(C_HEADS) for a in range(n_sub)]

    def scores(kb, h, a):
        return _dot(k_ref[0, h, kb * ks:(kb + 1) * ks, :], qt_refs[a][0, h]).reshape(ks // 8, 8, qs)

    st = scores(*units[0])
    for n, (kb, h, a) in enumerate(units):
        q0 = a * qs
        st_next = scores(*units[n + 1]) if n + 1 < len(units) else None
        m_prev = m_scr[h, :, q0:q0 + qs]
        m_new = jnp.maximum(m_prev, jnp.max(jnp.max(st, axis=0), axis=0, keepdims=True))
        p = jnp.exp2(st - m_new[None]).reshape(ks, qs).astype(MXU_DTYPE)
        alpha = jnp.exp2(m_prev - m_new)
        pv = _dot(vt_refs[kb][0, h], p)
        acc = acc_scr[h, :, q0:q0 + qs].reshape(C_VT // 8, 8, qs) * alpha[None]
        acc_scr[h, :, q0:q0 + qs] = acc.reshape(C_VT, qs) + pv
        m_scr[h, :, q0:q0 + qs] = m_new
        st = st_next

    @pl.when(j == pl.num_programs(2) - 1)
    def _():
        for h in range(C_HEADS):
            acc = acc_scr[h]
            o = (acc[:C_V] / acc[C_V:C_V + 1]).T
            zh = z_ref[:, h * C_V:(h + 1) * C_V]
            o_ref[:, h * C_V:(h + 1) * C_V] = (o * _silu(zh)).astype(o_ref.dtype)


def _flash(qt, k, vt, cin, *, batch, seq):
    bq, bk = min(BQ, seq), min(BK, seq)
    qs, ks = min(QSUB, bq), min(KSUB, bk)
    n_sub, n_ksub = bq // qs, bk // ks
    nq, nk = seq // bq, seq // bk
    d = C_HEADS * C_V
    qt_specs = [pl.BlockSpec((1, C_HEADS, C_QK, qs), functools.partial(lambda b, i, j, a: (b, 0, 0, i * n_sub + a), a=a))
                for a in range(n_sub)]
    vt_specs = [pl.BlockSpec((1, C_HEADS, C_VT, ks), functools.partial(lambda b, i, j, a: (b, 0, 0, j * n_ksub + a), a=a))
                for a in range(n_ksub)]
    return pl.pallas_call(
        functools.partial(_flash_kernel, n_sub=n_sub, n_ksub=n_ksub),
        grid=(batch, nq, nk),
        in_specs=qt_specs + vt_specs + [
                  pl.BlockSpec((1, C_HEADS, bk, C_QK), lambda b, i, j: (b, 0, j, 0)),
                  pl.BlockSpec((bq, d), lambda b, i, j: (b * nq + i, C_Z // d))],
        out_specs=pl.BlockSpec((bq, d), lambda b, i, j: (b * nq + i, 0)),
        out_shape=jax.ShapeDtypeStruct((batch * seq, d), MXU_DTYPE),
        scratch_shapes=[pltpu.VMEM((C_HEADS, 8, bq), F32), pltpu.VMEM((C_HEADS, C_VT, bq), F32)],
        compiler_params=_cparams(("parallel", "parallel", "arbitrary"), 56),
        name="mla_flash",
    )(*([qt] * n_sub), *([vt] * n_ksub), k, cin)


def _split_cols(w, sizes):
    idx = np.cumsum(np.array(sizes))[:-1].tolist()
    return jnp.split(w, idx, axis=-1)


def _rope_swap(w):
    half = w.shape[-1] // 2
    return jnp.concatenate([-w[..., half:], w[..., :half]], axis=-1)


def _pad_cols(w, n):
    return jnp.pad(w, [(0, 0)] * (w.ndim - 1) + [(0, n - w.shape[-1])])


def _prep_ab(w_in, alpha_w2, alpha_b, igate_b, fgate_b):
    sizes = (2 * A_HEADS * A_DK, A_HEADS * A_DV, 2 * A_HEADS, 2 * A_HEADS, A_HEADS * A_DV, A_HEADS * A_DV,
             B_HEADS * B_DK, B_HEADS * B_DK, B_HEADS * B_DV, 2 * B_GATE_RANK, B_HEADS * B_DV)
    a_qk, a_v, a_i, a_f, a_o, a_z, b_q, b_k, b_v, b_low, b_z = _split_cols(w_in, sizes)
    w_small = _pad_cols(jnp.concatenate([a_i, a_f, b_low], axis=-1), LANES).astype(MXU_DTYPE)
    w = jnp.concatenate([a_qk, a_o, a_z, b_q, b_k, b_z, a_v, b_v], axis=-1).astype(MXU_DTYPE)
    assert w.shape[-1] == AB_SLABS * SLAB
    hk = B_HEADS * B_DK
    w2 = jnp.zeros((LANES, 2 * hk), F32)
    w2 = w2.at[SM_LOW:SM_LOW + B_GATE_RANK, :hk].set(alpha_w2[0])
    w2 = w2.at[SM_LOW + B_GATE_RANK:SM_LOW + 2 * B_GATE_RANK, hk:].set(alpha_w2[1])
    ab = alpha_b.reshape(1, 2 * hk)
    gbias = _pad_cols(jnp.concatenate([igate_b.reshape(1, -1), fgate_b.reshape(1, -1)], axis=-1), LANES)
    return w, w_small, w2.astype(MXU_DTYPE), ab, gbias


def _prep_c(w_in, w_q_up, w_kv_up):
    q_lat, kv_lat, k_rope, z = _split_cols(w_in, (C_Q_LORA, C_KV_LORA, C_ROPE, C_HEADS * C_V))
    w_c = jnp.concatenate([z, kv_lat, _pad_cols(k_rope, LANES), _pad_cols(_rope_swap(k_rope), LANES), q_lat],
                          axis=-1).astype(MXU_DTYPE)
    wq = w_q_up.reshape(C_Q_LORA, C_HEADS, C_NOPE + C_ROPE)
    nope, rope = wq[..., :C_NOPE], wq[..., C_NOPE:]
    wq_t = jnp.transpose(jnp.concatenate([nope, rope, _rope_swap(rope)], axis=-1), (1, 2, 0))
    wkv = w_kv_up.reshape(C_KV_LORA, C_HEADS, C_NOPE + C_V)
    wk = jnp.transpose(wkv[..., :C_NOPE], (1, 0, 2))
    wv_t = jnp.transpose(wkv[..., C_NOPE:], (1, 2, 0))
    return w_c, wq_t.astype(MXU_DTYPE), wk.astype(MXU_DTYPE), wv_t.astype(MXU_DTYPE)


def _rope_tables(seq):
    inv = ROPE_BASE ** (-jnp.arange(0, C_ROPE, 2, dtype=F32) / C_ROPE)
    ang = jnp.arange(seq, dtype=F32)[:, None] * inv[None, :]
    cos, sin = jnp.cos(ang), jnp.sin(ang)
    cos2, sin2 = jnp.concatenate([cos, cos], axis=-1), jnp.concatenate([sin, sin], axis=-1)
    return _pad_cols(cos2, LANES), _pad_cols(sin2, LANES), cos2.T, sin2.T


def _trunk(x, mem, p, prep):
    batch, seq, d = x.shape
    x = x.reshape(batch * seq, d)
    mem = mem.reshape(batch * N_MEM, d)
    cos, sin, cos_t, sin_t = _rope_tables(seq)
    for l in range(DEPTH):
        j = l // 2
        if l % 2 == 0:
            w_in, w_small, w2, ab, gbias = prep['ab'][j]
            proj, small = _ab_proj(x, p['norm_mix'][l], w_in, w_small)
            qk, kt = _conv_silu(proj, p['ab_conv_w'][j], p['ab_conv_b'][j], seq=seq)
            h_f, h_b = _mlstm(qk, kt, proj, small, gbias, batch=batch, seq=seq)
            o_f, o_b = _gla(proj, small, w2, ab, batch=batch, seq=seq)
            x = _ab_post(h_f, h_b, o_f, o_b, proj, p['a_ogate_b'][j], p['a_head_norm'][j], p['b_head_norm'][j],
                         prep['ab_w_out'][j], x)
            mix = None
        else:
            w_c, wq_t, wk, wv_t = prep['c'][j]
            cin = _norm_matmul(x, p['norm_mix'][l], w_c, tm=TM_PROJ, tn=C_TN, out_dtype=F32)
            qt = _q_up(cin, p['c_q_norm'][j], wq_t, cos_t, sin_t, batch=batch, seq=seq)
            k, vt = _kv_up(cin, p['c_kv_norm'][j], wk, wv_t, cos, sin, batch=batch, seq=seq)
            mix = (_flash(qt, k, vt, cin, batch=batch, seq=seq), prep['c_w_out'][j])
        kv =_norm_matmul(mem, p['norm_mem'][l], prep['x_w_kv'][l], tm=N_MEM, tn=1024, out_dtype=MXU_DTYPE)
        x = _cross_attn(x, kv, p['norm_cross'][l], prep['x_w_q'][l], prep['x_w_o'][l], p['final_norm'],
                        seq=seq, final=(l == DEPTH - 1), mix=mix)
    return x.reshape(batch, seq, d)


def kernel(x_prompt, x_sample, mem_prompt, mem_sample, norm_mix, norm_cross, norm_mem, ab_w_in, ab_conv_w, ab_conv_b, a_igate_b, a_fgate_b, a_ogate_b, a_head_norm, b_alpha_w2, b_alpha_b, b_head_norm, ab_w_out, c_w_in, c_q_norm, c_kv_norm, c_w_q_up, c_w_kv_up, c_w_out, x_w_q, x_w_kv, x_w_o, final_norm):
    p = dict(norm_mix=norm_mix, norm_cross=norm_cross, norm_mem=norm_mem, ab_conv_w=ab_conv_w,
             ab_conv_b=ab_conv_b, a_ogate_b=a_ogate_b, a_head_norm=a_head_norm, b_head_norm=b_head_norm,
             c_q_norm=c_q_norm, c_kv_norm=c_kv_norm, final_norm=final_norm)
    n_even, n_odd = ab_w_in.shape[0], c_w_in.shape[0]
    prep = dict(
        ab=[_prep_ab(ab_w_in[j], b_alpha_w2[j], b_alpha_b[j], a_igate_b[j], a_fgate_b[j]) for j in range(n_even)],
        ab_w_out=[ab_w_out[j].astype(MXU_DTYPE) for j in range(n_even)],
        c=[_prep_c(c_w_in[j], c_w_q_up[j], c_w_kv_up[j]) for j in range(n_odd)],
        c_w_out=[c_w_out[j].astype(MXU_DTYPE) for j in range(n_odd)],
        x_w_q=[x_w_q[l].astype(MXU_DTYPE) for l in range(DEPTH)],
        x_w_kv=[x_w_kv[l].astype(MXU_DTYPE) for l in range(DEPTH)],
        x_w_o=[x_w_o[l].astype(MXU_DTYPE) for l in range(DEPTH)],
    )
    return (_trunk(x_prompt, mem_prompt, p, prep), _trunk(x_sample, mem_sample, p, prep))
```
